```python
import math
import jax, jax.numpy as jnp
from jax import lax
import numpy as np

D_MODEL = 1024
BATCH = 16
SEQ = 2048
DEPTH = 4

GRID_W = 64
CTX_LEN = 256
N_MIXERS = 3
HEAD_DIM = 64
ROPE_BASE = 10000.0
Q_BLOCK = 128
DA_HEADS = 8
DA_WIDTH = DA_HEADS * 2 * HEAD_DIM
WA_Q_HEADS = 16
WA_KV_HEADS = 4
WINDOW = 128
SSM_GROUP = 16
SSM_GROUPS = D_MODEL // SSM_GROUP
SSM_STATE = 64
SSM_STEP_MIN = 1e-3
SSM_STEP_MAX = 1e-1
PEER_HEADS = 8
PEER_NKEYS = 128
PEER_EXPERTS = PEER_NKEYS * PEER_NKEYS
PEER_QDIM = 256
PEER_TOPK = 16
PEER_CHUNK = 128
LN_EPS = 1e-5
DEEPNORM_ALPHA = (2 * DEPTH) ** 0.25
DEEPNORM_BETA = (8 * DEPTH) ** -0.25

kernel_name = 'hybrid_diffattn_swa_s5_peer_trunk'


def layer_norm(x, g, b):
    xf = x.astype(jnp.float32)
    mu = jnp.mean(xf, -1, keepdims=True)
    var = jnp.mean(jnp.square(xf - mu), -1, keepdims=True)
    return ((xf - mu) * lax.rsqrt(var + LN_EPS) * g + b).astype(x.dtype)


def modulate(h, shift, scale):
    return h * (1 + scale) + shift


def axial_rope_tables(n_tok, dim):
    rows = n_tok // GRID_W
    row = jnp.repeat(jnp.arange(rows, dtype=jnp.float32), GRID_W)
    col = jnp.tile(jnp.arange(GRID_W, dtype=jnp.float32), rows)
    n_freq = dim // 4
    inv = ROPE_BASE ** (-jnp.arange(n_freq, dtype=jnp.float32) / n_freq)
    ang = jnp.concatenate([row[:, None] * inv, col[:, None] * inv], -1)
    return jnp.cos(ang), jnp.sin(ang)


def apply_rope(x, cos, sin):
    shape = x.shape
    half = shape[-1] // 2
    bshape = (1, shape[1]) + (1,) * (len(shape) - 3) + (half,)
    cs, sn = cos.reshape(bshape), sin.reshape(bshape)
    xf = x.astype(jnp.float32).reshape(shape[:-1] + (half, 2))
    xe, xo = xf[..., 0], xf[..., 1]
    out = jnp.stack([xe * cs - xo * sn, xe * sn + xo * cs], -1)
    return out.reshape(shape).astype(x.dtype)


def diff_attention(hc, hl, w_qkv, w_o, lam_p, subln_g, lam_init, cos, sin, ctx_out):
    B, T, _ = hl.shape
    lp = lam_p.astype(jnp.float32)
    lam = jnp.exp(jnp.sum(lp[0] * lp[1])) - jnp.exp(jnp.sum(lp[2] * lp[3])) + lam_init
    scale = HEAD_DIM ** -0.5

    def project(h):
        S = h.shape[1]
        q, k, v = jnp.split(h @ w_qkv, 3, axis=-1)
        return (q.reshape(B, S, DA_HEADS, 2, HEAD_DIM), k.reshape(B, S, DA_HEADS, 2, HEAD_DIM),
                v.reshape(B, S, DA_HEADS, 2 * HEAD_DIM))

    def attend(q, k, v):
        s = jnp.einsum('bqhjd,bkhjd->bhjqk', q, k).astype(jnp.float32) * scale
        p = jax.nn.softmax(s, axis=-1)
        a = p[:, :, 0] - lam * p[:, :, 1]
        o = jnp.einsum('bhqk,bkhe->bqhe', a, v.astype(jnp.float32))
        o = o * lax.rsqrt(jnp.mean(o * o, -1, keepdims=True) + LN_EPS) * subln_g * (1.0 - lam_init)
        return o.reshape(o.shape[0], o.shape[1], -1).astype(hl.dtype)

    qc, kc, vc = project(hc)
    ql, kl, vl = project(hl)
    ql = apply_rope(ql, cos, sin)
    kl = apply_rope(kl, cos, sin)
    k_all = jnp.concatenate([kc, kl], 1)
    v_all = jnp.concatenate([vc, vl], 1)
    nb = T // Q_BLOCK
    q_blocks = jnp.moveaxis(ql.reshape(B, nb, Q_BLOCK, DA_HEADS, 2, HEAD_DIM), 1, 0)
    o_blocks = lax.map(lambda qb: attend(qb, k_all, v_all), q_blocks)
    ol = jnp.moveaxis(o_blocks, 0, 1).reshape(B, T, -1) @ w_o
    oc = attend(qc, kc, vc) @ w_o if ctx_out else None
    return oc, ol


def window_attention(hc, hl, w_qkv, w_o, sink, cos, sin, ctx_out):
    B, T, _ = hl.shape
    L = hc.shape[1]
    R = WA_Q_HEADS // WA_KV_HEADS
    scale = HEAD_DIM ** -0.5
    sink_g = sink.astype(jnp.float32).reshape(WA_KV_HEADS, R)

    def project(h):
        S = h.shape[1]
        q, k, v = jnp.split(h @ w_qkv, [WA_Q_HEADS * HEAD_DIM, (WA_Q_HEADS + WA_KV_HEADS) * HEAD_DIM], axis=-1)
        return (q.reshape(B, S, WA_Q_HEADS, HEAD_DIM), k.reshape(B, S, WA_KV_HEADS, HEAD_DIM),
                v.reshape(B, S, WA_KV_HEADS, HEAD_DIM))

    def attend(q, k, v, mask):
        Q = q.shape[1]
        qg = q.reshape(B, Q, WA_KV_HEADS, R, HEAD_DIM)
        s = jnp.einsum('bqgrd,bkgd->bgrqk', qg, k).astype(jnp.float32) * scale
        if mask is not None:
            s = jnp.where(mask, s, -jnp.inf)
        sink_col = jnp.broadcast_to(sink_g[None, :, :, None, None], s.shape[:-1] + (1,))
        p = jax.nn.softmax(jnp.concatenate([s, sink_col], -1), axis=-1)[..., :-1]
        o = jnp.einsum('bgrqk,bkgd->bqgrd', p, v.astype(jnp.float32))
        return o.reshape(B, Q, -1).astype(hl.dtype)

    qc, kc, vc = project(hc)
    ql, kl, vl = project(hl)
    ql = apply_rope(ql, cos, sin)
    kl = apply_rope(kl, cos, sin)
    pad = ((0, 0), (WINDOW, WINDOW), (0, 0), (0, 0))
    k_pad, v_pad = jnp.pad(kl, pad), jnp.pad(vl, pad)
    span = Q_BLOCK + 2 * WINDOW
    nb = T // Q_BLOCK
    q_blocks = jnp.moveaxis(ql.reshape(B, nb, Q_BLOCK, WA_Q_HEADS, HEAD_DIM), 1, 0)

    def block(args):
        qb, b = args
        start = b * Q_BLOCK
        kb = lax.dynamic_slice_in_dim(k_pad, start, span, axis=1)
        vb = lax.dynamic_slice_in_dim(v_pad, start, span, axis=1)
        q_pos = start + jnp.arange(Q_BLOCK)
        k_pos = start - WINDOW + jnp.arange(span)
        band = ((jnp.abs(q_pos[:, None] - k_pos[None, :]) <= WINDOW)
                & (k_pos >= 0)[None, :] & (k_pos < T)[None, :])
        mask = jnp.concatenate([jnp.ones((Q_BLOCK, L), dtype=bool), band], 1)
        return attend(qb, jnp.concatenate([kc, kb], 1), jnp.concatenate([vc, vb], 1), mask)

    o_blocks = lax.map(block, (q_blocks, jnp.arange(nb)))
    ol = jnp.moveaxis(o_blocks, 0, 1).reshape(B, T, -1) @ w_o
    oc = attend(qc, kc, vc, None) @ w_o if ctx_out else None
    return oc, ol


def diag_scan(lam_bar, bu, reverse):
    a = jnp.broadcast_to(lam_bar, (1, bu.shape[1]) + lam_bar.shape)

    def combine(e1, e2):
        a1, b1 = e1
        a2, b2 = e2
        return a1 * a2, a2 * b1 + b2

    _, x = lax.associative_scan(combine, (a, bu), reverse=reverse, axis=1)
    return x


def s5_mixer(hc, hl, lam_re, lam_im, log_step, b_re, b_im, c_re, c_im, d_skip, w_glu, ctx_out):
    B, T, D = hl.shape
    L = hc.shape[1]
    u_c = hc.astype(jnp.float32).reshape(B, L, SSM_GROUPS, SSM_GROUP)
    u_l = hl.astype(jnp.float32).reshape(B, T, SSM_GROUPS, SSM_GROUP)
    d = d_skip.astype(jnp.float32).reshape(SSM_GROUPS, SSM_GROUP)
    y_l = d * u_l
    y_c = d * u_c if ctx_out else None
    for r, reverse in ((0, False), (1, True)):
        lam = lax.complex(lam_re[r].astype(jnp.float32), lam_im[r].astype(jnp.float32))
        step = jnp.exp(log_step[r].astype(jnp.float32))[:, None]
        lam_bar = jnp.exp(lam * step)
        b_bar = lax.complex(b_re[r].astype(jnp.float32), b_im[r].astype(jnp.float32)) * ((lam_bar - 1) / lam)[..., None]
        c_mat = lax.complex(c_re[r].astype(jnp.float32), c_im[r].astype(jnp.float32))
        x_c = diag_scan(lam_bar, jnp.einsum('blgh,gph->blgp', u_c.astype(jnp.complex64), b_bar), reverse)
        s0 = x_c[:, 0] if reverse else x_c[:, -1]
        bu_l = jnp.einsum('btgh,gph->btgp', u_l.astype(jnp.complex64), b_bar)
        bu_l = bu_l.at[:, -1 if reverse else 0].add(lam_bar * s0)
        x_l = diag_scan(lam_bar, bu_l, reverse)
        y_l = y_l + jnp.real(jnp.einsum('btgp,ghp->btgh', x_l, c_mat))
        if ctx_out:
            y_c = y_c + jnp.real(jnp.einsum('blgp,ghp->blgh', x_c, c_mat))

    def glu_out(y, S):
        z = jax.nn.gelu(y.reshape(B, S, D)).astype(hl.dtype) @ w_glu
        a, g = jnp.split(z, 2, axis=-1)
        return a * jax.nn.sigmoid(g)

    return (glu_out(y_c, L) if ctx_out else None), glu_out(y_l, T)


def peer_ffn(h, w_q, sub_keys, u_tab, v_tab):
    B, S, D = h.shape
    tokens = h.reshape(-1, PEER_CHUNK, D)

    def chunk(xc):
        n = xc.shape[0]
        q = (xc @ w_q).reshape(n, PEER_HEADS, 2, PEER_QDIM // 2)
        s = jnp.einsum('nhjc,jhkc->nhjk', q, sub_keys).astype(jnp.float32)
        sv, si = lax.top_k(s, PEER_TOPK)
        comb = (sv[:, :, 0, :, None] + sv[:, :, 1, None, :]).reshape(n, PEER_HEADS, -1)
        cidx = (si[:, :, 0, :, None] * PEER_NKEYS + si[:, :, 1, None, :]).reshape(n, PEER_HEADS, -1)
        top_s, pos = lax.top_k(comb, PEER_TOPK)
        eidx = jnp.take_along_axis(cidx, pos, -1)
        g = jax.nn.softmax(top_s, axis=-1)
        act = jax.nn.gelu(jnp.einsum('nd,nhkd->nhk', xc, u_tab[eidx]).astype(jnp.float32))
        return jnp.einsum('nhk,nhkd->nd', (g * act).astype(xc.dtype), v_tab[eidx])

    return lax.map(chunk, tokens).reshape(B, S, D)


def setup_inputs(seed: int = 0) -> dict:
    key = jax.random.key(seed)
    ks = jax.random.split(key, 32)
    f32 = jnp.float32
    D = D_MODEL
    n_a = len(range(0, DEPTH, N_MIXERS))
    n_b = len(range(1, DEPTH, N_MIXERS))
    n_c = len(range(2, DEPTH, N_MIXERS))
    s_in = D ** -0.5
    beta = DEEPNORM_BETA

    def nrm(k, shape, s):
        return s * jax.random.normal(k, shape, f32)

    G, P, Hg = SSM_GROUPS, SSM_STATE, SSM_GROUP
    lam_im = jnp.pi * jnp.arange(P, dtype=f32) + nrm(ks[24], (n_c, 2, G, P), 0.01)
    return {
        'x': nrm(ks[0], (BATCH, SEQ, D), 1.0),
        'c': nrm(ks[1], (BATCH, D), 1.0),
        'ctx': nrm(ks[2], (BATCH, CTX_LEN, D), 1.0),
        'c_ctx': nrm(ks[3], (D,), 1.0),
        'mod_w': nrm(ks[4], (DEPTH, D, 6 * D), s_in),
        'mod_b': nrm(ks[5], (DEPTH, 6 * D), 0.02),
        'ln_g': 1.0 + nrm(ks[6], (DEPTH, 2, D), 0.02),
        'ln_b': nrm(ks[7], (DEPTH, 2, D), 0.02),
        'peer_wq': nrm(ks[8], (DEPTH, D, PEER_HEADS * PEER_QDIM), s_in),
        'peer_keys': nrm(ks[9], (DEPTH, 2, PEER_HEADS, PEER_NKEYS, PEER_QDIM // 2), (PEER_QDIM // 2) ** -0.5),
        'peer_u': nrm(ks[10], (DEPTH, PEER_EXPERTS, D), s_in),
        'peer_v': nrm(ks[11], (DEPTH, PEER_EXPERTS, D), beta),
        'da_wqkv': jnp.concatenate([nrm(ks[12], (n_a, D, 2 * DA_WIDTH), s_in),
                                    nrm(ks[13], (n_a, D, DA_WIDTH), s_in * beta)], -1),
        'da_wo': nrm(ks[14], (n_a, DA_WIDTH, D), DA_WIDTH ** -0.5 * beta),
        'da_lambda': nrm(ks[15], (n_a, 4, HEAD_DIM), 0.1),
        'da_subln': 1.0 + nrm(ks[16], (n_a, 2 * HEAD_DIM), 0.02),
        'wa_wqkv': jnp.concatenate([nrm(ks[17], (n_b, D, (WA_Q_HEADS + WA_KV_HEADS) * HEAD_DIM), s_in),
                                    nrm(ks[18], (n_b, D, WA_KV_HEADS * HEAD_DIM), s_in * beta)], -1),
        'wa_wo': nrm(ks[19], (n_b, WA_Q_HEADS * HEAD_DIM, D), (WA_Q_HEADS * HEAD_DIM) ** -0.5 * beta),
        'wa_sink': nrm(ks[20], (n_b, WA_Q_HEADS), 1.0),
        'ssm_lam_re': -0.5 + nrm(ks[21], (n_c, 2, G, P), 0.01),
        'ssm_lam_im': lam_im,
        'ssm_log_step': jax.random.uniform(ks[22], (n_c, 2, G), f32, math.log(SSM_STEP_MIN), math.log(SSM_STEP_MAX)),
        'ssm_b_re': nrm(ks[23], (n_c, 2, G, P, Hg), (2 * Hg) ** -0.5),
        'ssm_b_im': nrm(ks[25], (n_c, 2, G, P, Hg), (2 * Hg) ** -0.5),
        'ssm_c_re': nrm(ks[26], (n_c, 2, G, Hg, P), P ** -0.5),
        'ssm_c_im': nrm(ks[27], (n_c, 2, G, Hg, P), P ** -0.5),
        'ssm_d': nrm(ks[28], (n_c, D), 1.0),
        'ssm_w_glu': nrm(ks[29], (n_c, D, 2 * D), s_in * beta),
    }


def reference(x, c, ctx, c_ctx, mod_w, mod_b, ln_g, ln_b, peer_wq, peer_keys, peer_u, peer_v,
              da_wqkv, da_wo, da_lambda, da_subln, wa_wqkv, wa_wo, wa_sink,
              ssm_lam_re, ssm_lam_im, ssm_log_step, ssm_b_re, ssm_b_im, ssm_c_re, ssm_c_im, ssm_d, ssm_w_glu):
    B, T, D = x.shape
    L = ctx.shape[1]
    cos, sin = axial_rope_tables(T, HEAD_DIM)
    silu_c = jax.nn.silu(c)
    silu_cc = jax.nn.silu(c_ctx)
    lat, cx = x, ctx
    for i in range(DEPTH):
        kind, j = i % N_MIXERS, i // N_MIXERS
        last = i == DEPTH - 1
        m_l = [t[:, None] for t in jnp.split(silu_c @ mod_w[i] + mod_b[i], 6, axis=-1)]
        m_c = jnp.split(silu_cc @ mod_w[i] + mod_b[i], 6, axis=-1)
        hl = modulate(lat, m_l[0], m_l[1])
        hc = modulate(cx, m_c[0], m_c[1])
        if kind == 0:
            lam_init = 0.8 - 0.6 * math.exp(-0.3 * i)
            oc, ol = diff_attention(hc, hl, da_wqkv[j], da_wo[j], da_lambda[j], da_subln[j], lam_init,
                                    cos, sin, not last)
        elif kind == 1:
            oc, ol = window_attention(hc, hl, wa_wqkv[j], wa_wo[j], wa_sink[j], cos, sin, not last)
        else:
            oc, ol = s5_mixer(hc, hl, ssm_lam_re[j], ssm_lam_im[j], ssm_log_step[j], ssm_b_re[j], ssm_b_im[j],
                              ssm_c_re[j], ssm_c_im[j], ssm_d[j], ssm_w_glu[j], not last)
        lat = layer_norm(DEEPNORM_ALPHA * lat + m_l[2] * ol, ln_g[i, 0], ln_b[i, 0])
        hl = modulate(lat, m_l[3], m_l[4])
        if last:
            ol = peer_ffn(hl, peer_wq[i], peer_keys[i], peer_u[i], peer_v[i])
        else:
            cx = layer_norm(DEEPNORM_ALPHA * cx + m_c[2] * oc, ln_g[i, 0], ln_b[i, 0])
            hc = modulate(cx, m_c[3], m_c[4])
            o = peer_ffn(jnp.concatenate([hc, hl], 1), peer_wq[i], peer_keys[i], peer_u[i], peer_v[i])
            oc, ol = o[:, :L], o[:, L:]
            cx = layer_norm(DEEPNORM_ALPHA * cx + m_c[5] * oc, ln_g[i, 1], ln_b[i, 1])
        lat = layer_norm(DEEPNORM_ALPHA * lat + m_l[5] * ol, ln_g[i, 1], ln_b[i, 1])
    return lat
```

```python
import functools
import math

import jax
import jax.numpy as jnp
from jax import lax
from jax.experimental import pallas as pl
from jax.experimental.pallas import tpu as pltpu

F32 = jnp.float32
BF16 = jnp.bfloat16

DEPTH = 4
N_MIXERS = 3
GRID_W = 64
HEAD_DIM = 64
ROPE_BASE = 10000.0
DA_HEADS = 8
WA_Q_HEADS = 16
WA_KV_HEADS = 4
WA_REP = WA_Q_HEADS // WA_KV_HEADS
WINDOW = 128
SSM_GROUP = 16
SSM_STATE = 64
PEER_HEADS = 8
PEER_NKEYS = 128
PEER_QDIM = 256
PEER_TOPK = 16
LN_EPS = 1e-5
DEEPNORM_ALPHA = (2 * DEPTH) ** 0.25
ATT_SCALE = HEAD_DIM ** -0.5

LANES = 128
SUBLANES = 8
ROW_TILE = 256
WIN_TILE = 128
SCAN_CHUNK = 128
SSM_SLAB_GROUPS = 8
PEER_TOK_BLOCK = 32
PEER_TOK_GROUP = 8
VMEM_LIMIT = 48 * 1024 * 1024
NEG_INF = float("-inf")


def _cparams(n_axes):
    return pltpu.CompilerParams(dimension_semantics=("arbitrary",) * n_axes, vmem_limit_bytes=VMEM_LIMIT)


def _gelu_tanh(x):
    return 0.5 * x * (1.0 + jnp.tanh(math.sqrt(2.0 / math.pi) * (x + 0.044715 * (x * x * x))))


def _sigmoid(x):
    return 1.0 / (1.0 + jnp.exp(-x))


def _layer_norm_rows(z, g, b):
    mu = jnp.mean(z, axis=-1, keepdims=True)
    zc = z - mu
    var = jnp.mean(zc * zc, axis=-1, keepdims=True)
    return zc * lax.rsqrt(var + LN_EPS) * g + b


def _modulate(x, m, shift_idx):
    return x * (1.0 + m[shift_idx + 1:shift_idx + 2]) + m[shift_idx:shift_idx + 1]


def _mod_kernel(a_ref, w_ref, b_ref, o_ref):
    a = a_ref[...]
    a = a * _sigmoid(a)
    o_ref[0] = jnp.dot(a, w_ref[0], preferred_element_type=F32, precision=lax.Precision.HIGHEST) + b_ref[0]


def _mod_vectors(c, c_ctx, mod_w, mod_b):
    B, D = c.shape
    depth, _, n6 = mod_w.shape
    rows = -(-(B + 1) // SUBLANES) * SUBLANES
    a = jnp.zeros((rows, D), F32).at[:B].set(c).at[B].set(c_ctx)
    tn = n6 // 4
    out = pl.pallas_call(
        _mod_kernel,
        grid=(depth, n6 // tn),
        in_specs=[pl.BlockSpec((rows, D), lambda i, j: (0, 0)),
                  pl.BlockSpec((1, D, tn), lambda i, j: (i, 0, j)),
                  pl.BlockSpec((1, 1, tn), lambda i, j: (i, 0, j))],
        out_specs=pl.BlockSpec((1, rows, tn), lambda i, j: (i, 0, j)),
        out_shape=jax.ShapeDtypeStruct((depth, rows, n6), F32),
        compiler_params=_cparams(2),
    )(a, mod_w, mod_b.reshape(depth, 1, n6))
    lat = out[:, :B].reshape(depth, B, 6, D)
    ctx = jnp.broadcast_to(out[:, B].reshape(depth, 1, 6, D), (depth, B, 6, D))
    m = jnp.stack([ctx, lat], axis=2)
    return jnp.pad(m, ((0, 0), (0, 0), (0, 0), (0, 2), (0, 0)))


def _mod_spec(D, nctx_blocks, j0):
    return pl.BlockSpec((1, 1, SUBLANES, D), lambda b, j: (b, jnp.minimum((j + j0) // nctx_blocks, 1), 0, 0))


def _proj_kernel(s_ref, m_ref, w_ref, cos_ref, sin_ref, o_ref, *, n_rope, tn):
    xb = _modulate(s_ref[0], m_ref[0, 0], 0).astype(BF16)
    tm = xb.shape[0]
    n_out = w_ref.shape[1]
    cos_t = cos_ref[...]
    sin_t = sin_ref[...]
    lane = lax.broadcasted_iota(jnp.int32, (tm, LANES), 1)
    first_half = (lane % HEAD_DIM) < (HEAD_DIM // 2)
    for c0 in range(0, n_out, tn):
        y = jnp.dot(xb, w_ref[:, c0:c0 + tn], preferred_element_type=F32)
        if c0 < n_rope:
            pieces = []
            for l0 in range(0, tn, LANES):
                yc = y[:, l0:l0 + LANES]
                partner = jnp.where(first_half,
                                    pltpu.roll(yc, LANES - HEAD_DIM // 2, 1),
                                    pltpu.roll(yc, HEAD_DIM // 2, 1))
                pieces.append(yc * cos_t + partner * sin_t)
            y = jnp.concatenate(pieces, axis=1)
        o_ref[0, :, c0:c0 + tn] = y.astype(o_ref.dtype)


def _project(S, M, w, cos_t, sin_t, n_rope, ctx_len):
    B, P, D = S.shape
    n_out = w.shape[1]
    tm = ROW_TILE
    kern = functools.partial(_proj_kernel, n_rope=n_rope, tn=512)
    return pl.pallas_call(
        kern,
        grid=(B, P // tm),
        in_specs=[pl.BlockSpec((1, tm, D), lambda b, j: (b, j, 0)),
                  _mod_spec(D, ctx_len // tm, 0),
                  pl.BlockSpec((D, n_out), lambda b, j: (0, 0)),
                  pl.BlockSpec((tm, LANES), lambda b, j: (j, 0)),
                  pl.BlockSpec((tm, LANES), lambda b, j: (j, 0))],
        out_specs=pl.BlockSpec((1, tm, n_out), lambda b, j: (b, j, 0)),
        out_shape=jax.ShapeDtypeStruct((B, P, n_out), BF16),
        compiler_params=_cparams(2),
    )(S, M, w, cos_t, sin_t)


def _dattn_kernel(lam_ref, g_ref, q_ref, k_ref, v_ref, o_ref, *, lam_init, ctx_len, ctx_queries):
    lp = lam_ref[...]
    lam = (jnp.exp(jnp.sum(lp[0:1] * lp[1:2], axis=-1, keepdims=True))
           - jnp.exp(jnp.sum(lp[2:3] * lp[3:4], axis=-1, keepdims=True)) + lam_init)
    q = q_ref[0]
    lane = lax.broadcasted_iota(jnp.int32, q.shape, 1)
    zero = jnp.zeros_like(q)
    q_maps = (jnp.where(lane < HEAD_DIM, q, zero), jnp.where(lane >= HEAD_DIM, q, zero))

    def attend(nk):
        k = k_ref[0, 0:nk, :]
        v = v_ref[0, 0:nk, :]
        probs = []
        for qm in q_maps:
            s = lax.dot_general(qm, k, (((1,), (1,)), ((), ())), preferred_element_type=F32) * ATT_SCALE
            p = jnp.exp(s - jnp.max(s, axis=-1, keepdims=True))
            probs.append((p, 1.0 / jnp.sum(p, axis=-1, keepdims=True)))
        a = probs[0][0] * probs[0][1] - probs[1][0] * (lam * probs[1][1])
        o = jnp.dot(a.astype(BF16), v, preferred_element_type=F32)
        o = o * lax.rsqrt(jnp.mean(o * o, axis=-1, keepdims=True) + LN_EPS) * g_ref[...] * (1.0 - lam_init)
        o_ref[0] = o.astype(o_ref.dtype)

    if ctx_queries:
        qi = pl.program_id(2)

        @pl.when(qi == 0)
        def _():
            attend(ctx_len)

        @pl.when(qi > 0)
        def _():
            attend(k_ref.shape[1])
    else:
        attend(k_ref.shape[1])


def _diff_attention(qkv, lam_p, subln_g, lam_init, ctx_len, ctx_queries):
    B, P, _ = qkv.shape
    tq = ROW_TILE
    assert ctx_len == tq
    j0 = 0 if ctx_queries else 1
    H = DA_HEADS
    kern = functools.partial(_dattn_kernel, lam_init=lam_init, ctx_len=ctx_len, ctx_queries=ctx_queries)
    return pl.pallas_call(
        kern,
        grid=(B, H, P // tq - j0),
        in_specs=[pl.BlockSpec((4, HEAD_DIM), lambda b, h, i: (0, 0)),
                  pl.BlockSpec((1, 2 * HEAD_DIM), lambda b, h, i: (0, 0)),
                  pl.BlockSpec((1, tq, LANES), lambda b, h, i: (b, i + j0, h)),
                  pl.BlockSpec((1, P, LANES), lambda b, h, i: (b, 0, H + h)),
                  pl.BlockSpec((1, P, LANES), lambda b, h, i: (b, 0, 2 * H + h))],
        out_specs=pl.BlockSpec((1, tq, LANES), lambda b, h, i: (b, i + j0, h)),
        out_shape=jax.ShapeDtypeStruct((B, P, H * LANES), BF16),
        compiler_params=_cparams(3),
    )(lam_p, subln_g.reshape(1, -1), qkv, qkv, qkv)


def _wattn_kernel(sink_ref, q_ref, kc_ref, k0_ref, k1_ref, k2_ref, vc_ref, v0_ref, v1_ref, v2_ref, o_ref,
                  *, n_ctx_blocks, n_lat_blocks, j0):
    qb = pl.program_id(1) + j0
    g = pl.program_id(2)
    tq = q_ref.shape[1]
    width = q_ref.shape[2]
    q = q_ref[0]
    lane = lax.broadcasted_iota(jnp.int32, q.shape, 1)
    zero = jnp.zeros_like(q)
    qm = jnp.concatenate(
        [jnp.where((lane >= r * HEAD_DIM) & (lane < (r + 1) * HEAD_DIM), q, zero) for r in range(WA_REP)], axis=0)
    sink_col = jnp.concatenate([jnp.full((tq, 1), sink_ref[g * WA_REP + r], F32) for r in range(WA_REP)], axis=0)
    dims = (((1,), (1,)), ((), ()))

    def finish(o_all, denom):
        o_all = o_all / denom
        lane_o = lax.broadcasted_iota(jnp.int32, (tq, width), 1)
        o = jnp.zeros((tq, width), F32)
        for r in range(WA_REP):
            sel = (lane_o >= r * HEAD_DIM) & (lane_o < (r + 1) * HEAD_DIM)
            o = o + jnp.where(sel, o_all[r * tq:(r + 1) * tq, :], 0.0)
        o_ref[0] = o.astype(o_ref.dtype)

    @pl.when(qb < n_ctx_blocks)
    def _():
        s_c = lax.dot_general(qm, kc_ref[0], dims, preferred_element_type=F32) * ATT_SCALE
        m = jnp.maximum(jnp.max(s_c, axis=-1, keepdims=True), sink_col)
        p_c = jnp.exp(s_c - m)
        denom = jnp.sum(p_c, axis=-1, keepdims=True) + jnp.exp(sink_col - m)
        finish(jnp.dot(p_c.astype(BF16), vc_ref[0], preferred_element_type=F32), denom)

    @pl.when(qb >= n_ctx_blocks)
    def _():
        lb = qb - n_ctx_blocks
        kw = jnp.concatenate([k0_ref[0], k1_ref[0], k2_ref[0]], axis=0)
        vw = jnp.concatenate([v0_ref[0], v1_ref[0], v2_ref[0]], axis=0)
        s_c = lax.dot_general(qm, kc_ref[0], dims, preferred_element_type=F32) * ATT_SCALE
        s_w = lax.dot_general(qm, kw, dims, preferred_element_type=F32) * ATT_SCALE
        iq = lax.broadcasted_iota(jnp.int32, s_w.shape, 0) % tq
        ik = lax.broadcasted_iota(jnp.int32, s_w.shape, 1)
        lo = jnp.where(lb == 0, tq, 0)
        hi = jnp.where(lb == n_lat_blocks - 1, 2 * tq, 3 * tq)
        band = (ik >= iq) & (ik <= iq + 2 * WINDOW) & (ik >= lo) & (ik < hi)
        s_w = jnp.where(band, s_w, NEG_INF)
        m = jnp.maximum(jnp.maximum(jnp.max(s_c, axis=-1, keepdims=True), jnp.max(s_w, axis=-1, keepdims=True)),
                        sink_col)
        p_c = jnp.exp(s_c - m)
        p_w = jnp.exp(s_w - m)
        denom = (jnp.sum(p_c, axis=-1, keepdims=True) + jnp.sum(p_w, axis=-1, keepdims=True)
                 + jnp.exp(sink_col - m))
        o_all = (jnp.dot(p_c.astype(BF16), vc_ref[0], preferred_element_type=F32)
                 + jnp.dot(p_w.astype(BF16), vw, preferred_element_type=F32))
        finish(o_all, denom)


def _window_attention(qkv, sink, ctx_len, ctx_queries):
    B, P, _ = qkv.shape
    tq = WIN_TILE
    assert tq == WINDOW
    nctx = ctx_len // tq
    nlat = P // tq - nctx
    j0 = 0 if ctx_queries else nctx
    G = WA_KV_HEADS
    width = WA_REP * HEAD_DIM

    def lat_spec(off, col0):
        return pl.BlockSpec((1, tq, width),
                            lambda b, j, g: (b, jnp.clip(j + j0 - nctx + off, 0, nlat - 1) + nctx, col0 + g))

    q_spec = pl.BlockSpec((1, tq, width), lambda b, j, g: (b, j + j0, g))
    kc_spec = pl.BlockSpec((1, ctx_len, width), lambda b, j, g: (b, 0, G + g))
    vc_spec = pl.BlockSpec((1, ctx_len, width), lambda b, j, g: (b, 0, 2 * G + g))
    kern = functools.partial(_wattn_kernel, n_ctx_blocks=nctx, n_lat_blocks=nlat, j0=j0)
    return pl.pallas_call(
        kern,
        grid=(B, P // tq - j0, G),
        in_specs=([pl.BlockSpec(memory_space=pltpu.SMEM), q_spec, kc_spec]
                  + [lat_spec(off, G) for off in (-1, 0, 1)] + [vc_spec]
                  + [lat_spec(off, 2 * G) for off in (-1, 0, 1)]),
        out_specs=pl.BlockSpec((1, tq, width), lambda b, j, g: (b, j + j0, g)),
        out_shape=jax.ShapeDtypeStruct((B, P, G * width), BF16),
        compiler_params=_cparams(3),
    )(sink, qkv, qkv, qkv, qkv, qkv, qkv, qkv, qkv, qkv)


def _resid_ln_kernel(o_ref, w_ref, s_ref, m_ref, g_ref, b_ref, out_ref):
    y = jnp.dot(o_ref[0], w_ref[...], preferred_element_type=F32)
    z = DEEPNORM_ALPHA * s_ref[0] + m_ref[0, 0][2:3] * y
    out_ref[0] = _layer_norm_rows(z, g_ref[...], b_ref[...])


def _out_proj_resid_ln(O, w, S, M, ln_g, ln_b, ctx_len, skip_ctx):
    B, P, D = S.shape
    kdim = O.shape[2]
    tm = ROW_TILE
    j0 = ctx_len // tm if skip_ctx else 0
    row = lambda b, j: (b, j + j0, 0)
    return pl.pallas_call(
        _resid_ln_kernel,
        grid=(B, P // tm - j0),
        in_specs=[pl.BlockSpec((1, tm, kdim), row),
                  pl.BlockSpec((kdim, D), lambda b, j: (0, 0)),
                  pl.BlockSpec((1, tm, D), row),
                  _mod_spec(D, ctx_len // tm, j0),
                  pl.BlockSpec((1, D), lambda b, j: (0, 0)),
                  pl.BlockSpec((1, D), lambda b, j: (0, 0))],
        out_specs=pl.BlockSpec((1, tm, D), row),
        out_shape=jax.ShapeDtypeStruct((B, P, D), F32),
        input_output_aliases={2: 0},
        compiler_params=_cparams(2),
    )(O, w, S, M, ln_g.reshape(1, D), ln_b.reshape(1, D))


def _s5_kernel(s_ref, mt_ref, bm_ref, cm_ref, lr_ref, li_ref, y_ref, xbuf, state):
    d = pl.program_id(0)
    c = pl.program_id(2)
    tc, nb, w_in = s_ref.shape
    half = lr_ref.shape[-1]

    @pl.when(c == 0)
    def _():
        state[...] = jnp.zeros_like(state)

    mt = mt_ref[0]
    u = s_ref[...] * (1.0 + mt[1][None]) + mt[0][None]
    u2 = u.reshape(tc * nb, w_in).astype(BF16)
    xbuf[...] = jnp.dot(u2, bm_ref[0, 0], preferred_element_type=F32)
    lr = jnp.broadcast_to(lr_ref[0, 0], (nb, half))
    li = jnp.broadcast_to(li_ref[0, 0], (nb, half))

    def step(t, carry):
        sr, si = carry
        tt = jnp.where(d == 0, t, tc - 1 - t)
        r0 = pl.multiple_of(tt * nb, nb)
        nr = lr * sr - li * si + xbuf[pl.ds(r0, nb), 0:half]
        ni = lr * si + li * sr + xbuf[pl.ds(r0, nb), half:2 * half]
        xbuf[pl.ds(r0, nb), 0:half] = nr
        xbuf[pl.ds(r0, nb), half:2 * half] = ni
        return nr, ni

    sr, si = lax.fori_loop(0, tc, step, (state[:, 0:half], state[:, half:2 * half]), unroll=4)
    state[:, 0:half] = sr
    state[:, half:2 * half] = si
    y = jnp.dot(xbuf[...].astype(BF16), cm_ref[0, 0], preferred_element_type=F32)
    y_ref[0] = y.reshape(tc, nb, w_in)


def _s5_scan(S_tm, Mt, bmat, cmat, lam_r, lam_i, ctx_len):
    P, B, D = S_tm.shape
    tc = SCAN_CHUNK
    w_in = SSM_SLAB_GROUPS * SSM_GROUP
    n_slab = D // w_in
    half = SSM_SLAB_GROUPS * SSM_STATE
    nch = P // tc
    nctx = ctx_len // tc
    nlat = nch - nctx

    def chunk(d, c):
        q = nch - 1 - c
        back = jnp.where(q < nlat, q + nctx, q - nlat)
        return jnp.where(d == 0, c, back)

    return pl.pallas_call(
        _s5_kernel,
        grid=(2, n_slab, nch),
        in_specs=[pl.BlockSpec((tc, B, w_in), lambda d, s, c: (chunk(d, c), 0, s)),
                  pl.BlockSpec((1, 2, B, w_in), lambda d, s, c: (jnp.minimum(chunk(d, c) // nctx, 1), 0, 0, s)),
                  pl.BlockSpec((1, 1, w_in, 2 * half), lambda d, s, c: (d, s, 0, 0)),
                  pl.BlockSpec((1, 1, 2 * half, w_in), lambda d, s, c: (d, s, 0, 0)),
                  pl.BlockSpec((1, 1, 1, half), lambda d, s, c: (d, s, 0, 0)),
                  pl.BlockSpec((1, 1, 1, half), lambda d, s, c: (d, s, 0, 0))],
        out_specs=pl.BlockSpec((1, tc, B, w_in), lambda d, s, c: (d, chunk(d, c), 0, s)),
        out_shape=jax.ShapeDtypeStruct((2, P, B, D), F32),
        scratch_shapes=[pltpu.VMEM((tc * B, 2 * half), F32), pltpu.VMEM((B, 2 * half), F32)],
        compiler_params=_cparams(3),
    )(S_tm, Mt, bmat, cmat, lam_r, lam_i)


def _s5_params(lam_re, lam_im, log_step, b_re, b_im, c_re, c_im):
    lam = lax.complex(lam_re.astype(F32), lam_im.astype(F32))
    step = jnp.exp(log_step.astype(F32))[..., None]
    lam_bar = jnp.exp(lam * step)
    b_bar = lax.complex(b_re.astype(F32), b_im.astype(F32)) * ((lam_bar - 1) / lam)[..., None]
    G, Pst, Hg = b_bar.shape[1:]
    ng = SSM_SLAB_GROUPS
    ns = G // ng
    eye = jnp.eye(ng, dtype=F32)

    def bdiag_in(x):
        x = x.reshape(2, ns, ng, Pst, Hg)
        return jnp.einsum('rsgph,gk->rsghkp', x, eye).reshape(2, ns, ng * Hg, ng * Pst)

    def bdiag_out(x):
        x = x.reshape(2, ns, ng, Hg, Pst)
        return jnp.einsum('rsghp,gk->rsgpkh', x, eye).reshape(2, ns, ng * Pst, ng * Hg)

    bmat = jnp.concatenate([bdiag_in(jnp.real(b_bar)), bdiag_in(jnp.imag(b_bar))], axis=-1).astype(BF16)
    cmat = jnp.concatenate([bdiag_out(c_re.astype(F32)), bdiag_out(-c_im.astype(F32))], axis=-2).astype(BF16)
    lam_r = jnp.real(lam_bar).reshape(2, ns, 1, ng * Pst)
    lam_i = jnp.imag(lam_bar).reshape(2, ns, 1, ng * Pst)
    return bmat, cmat, lam_r, lam_i


def _glu_ln_kernel(s_ref, y0_ref, y1_ref, d_ref, w_ref, m_ref, g_ref, b_ref, out_ref):
    m = m_ref[0, 0]
    s = s_ref[0]
    y = d_ref[...] * _modulate(s, m, 0) + y0_ref[0, 0] + y1_ref[0, 0]
    z = jnp.dot(_gelu_tanh(y).astype(BF16), w_ref[...], preferred_element_type=F32)
    D = s.shape[1]
    o = z[:, 0:D] * _sigmoid(z[:, D:2 * D])
    out_ref[0] = _layer_norm_rows(DEEPNORM_ALPHA * s + m[2:3] * o, g_ref[...], b_ref[...])


def _glu_resid_ln(S, Y, d_skip, w_glu, M, ln_g, ln_b, ctx_len):
    B, P, D = S.shape
    tm = ROW_TILE
    row = lambda b, j: (b, j, 0)
    return pl.pallas_call(
        _glu_ln_kernel,
        grid=(B, P // tm),
        in_specs=[pl.BlockSpec((1, tm, D), row),
                  pl.BlockSpec((1, 1, tm, D), lambda b, j: (0, b, j, 0)),
                  pl.BlockSpec((1, 1, tm, D), lambda b, j: (1, b, j, 0)),
                  pl.BlockSpec((1, D), lambda b, j: (0, 0)),
                  pl.BlockSpec((D, 2 * D), lambda b, j: (0, 0)),
                  _mod_spec(D, ctx_len // tm, 0),
                  pl.BlockSpec((1, D), lambda b, j: (0, 0)),
                  pl.BlockSpec((1, D), lambda b, j: (0, 0))],
        out_specs=pl.BlockSpec((1, tm, D), row),
        out_shape=jax.ShapeDtypeStruct((B, P, D), F32),
        compiler_params=_cparams(2),
    )(S, Y, Y, d_skip.reshape(1, D), w_glu, M, ln_g.reshape(1, D), ln_b.reshape(1, D))


def _peer_stair():
    blocks = []
    for a in range(PEER_TOPK):
        nb = PEER_TOPK // (a + 1)
        blocks.append((a, nb, PEER_TOPK if a == 0 else SUBLANES))
    return blocks


def _peer_topk_kernel(s_ref, m_ref, wqt_ref, keys_ref, g_out, e_out, qt):
    tn = s_ref.shape[1]
    nk = PEER_NKEYS
    k = PEER_TOPK
    xb = _modulate(s_ref[0], m_ref[0, 0], 3).astype(BF16)
    qt[...] = lax.dot_general(wqt_ref[...], xb, (((1,), (1,)), ((), ())), preferred_element_type=F32)
    stair = _peer_stair()

    def head(h, carry):
        krow = lax.broadcasted_iota(jnp.int32, (nk, tn), 0).astype(F32)
        rank = lax.broadcasted_iota(jnp.int32, (k, tn), 0)
        rank8 = lax.broadcasted_iota(jnp.int32, (SUBLANES, tn), 0)
        pos_const = jnp.concatenate(
            [(a * k + lax.broadcasted_iota(jnp.int32, (rows, tn), 0)).astype(F32) for a, _, rows in stair], axis=0)
        tops = []
        for j in range(2):
            qs = qt[pl.ds(pl.multiple_of(h * 2 * nk + j * nk, nk), nk), :].astype(BF16)
            s = jnp.dot(keys_ref[j, h], qs, preferred_element_type=F32)
            top = []
            for it in range(k):
                mx = jnp.max(s, axis=0, keepdims=True)
                ix = jnp.min(jnp.where(s == mx, krow, float(nk)), axis=0, keepdims=True)
                s = jnp.where(krow == ix, NEG_INF, s)
                top.append((mx, ix))
            tops.append(top)
        sv16 = jnp.zeros((k, tn), F32)
        si16 = jnp.zeros((k, tn), F32)
        sv8 = jnp.zeros((SUBLANES, tn), F32)
        si8 = jnp.zeros((SUBLANES, tn), F32)
        for it, (mx, ix) in enumerate(tops[1]):
            sv16 = jnp.where(rank == it, mx, sv16)
            si16 = jnp.where(rank == it, ix, si16)
            if it < SUBLANES:
                sv8 = jnp.where(rank8 == it, mx, sv8)
                si8 = jnp.where(rank8 == it, ix, si8)
        cs, ci = [], []
        for a, nb, rows in stair:
            mx0, ix0 = tops[0][a]
            cv = mx0 + (sv16 if rows == k else sv8)
            ce = ix0 * float(nk) + (si16 if rows == k else si8)
            if nb < rows:
                cv = jnp.where(rank8 < nb, cv, NEG_INF)
            cs.append(cv)
            ci.append(ce)
        cand = jnp.concatenate(cs, axis=0)
        cexp = jnp.concatenate(ci, axis=0)
        ts = jnp.zeros((k, tn), F32)
        te = jnp.zeros((k, tn), F32)
        best = None
        for it in range(k):
            mx = jnp.max(cand, axis=0, keepdims=True)
            sel = jnp.min(jnp.where(cand == mx, pos_const, float(k * k)), axis=0, keepdims=True)
            hit = pos_const == sel
            ex = jnp.max(jnp.where(hit, cexp, -1.0), axis=0, keepdims=True)
            cand = jnp.where(hit, NEG_INF, cand)
            ts = jnp.where(rank == it, mx, ts)
            te = jnp.where(rank == it, ex, te)
            best = mx if best is None else best
        p = jnp.exp(ts - best)
        gate = p / jnp.sum(p, axis=0, keepdims=True)
        r0 = pl.multiple_of(h * k, k)
        g_out[0, pl.ds(r0, k), :] = gate
        e_out[0, pl.ds(r0, k), :] = te.astype(jnp.int32)
        return carry

    lax.fori_loop(0, PEER_HEADS, head, 0)


def _peer_topk(S, M, wq_t, keys, ctx_len, skip_ctx):
    B, P, D = S.shape
    tn = ROW_TILE
    j0 = ctx_len // tn if skip_ctx else 0
    nblk = P // tn - j0
    hk = PEER_HEADS * PEER_TOPK
    out_spec = pl.BlockSpec((1, hk, tn), lambda b, j: (b, 0, j))
    return pl.pallas_call(
        _peer_topk_kernel,
        grid=(B, nblk),
        in_specs=[pl.BlockSpec((1, tn, D), lambda b, j: (b, j + j0, 0)),
                  _mod_spec(D, ctx_len // tn, j0),
                  pl.BlockSpec(wq_t.shape, lambda b, j: (0, 0)),
                  pl.BlockSpec(keys.shape, lambda b, j: (0, 0, 0, 0))],
        out_specs=[out_spec, out_spec],
        out_shape=[jax.ShapeDtypeStruct((B, hk, nblk * tn), F32),
                   jax.ShapeDtypeStruct((B, hk, nblk * tn), jnp.int32)],
        scratch_shapes=[pltpu.VMEM((wq_t.shape[0], tn), F32)],
        compiler_params=_cparams(2),
    )(S, M, wq_t, keys)


def _peer_gather_kernel(idx_ref, nxt_ref, g_ref, s_ref, m_ref, lg_ref, lb_ref, tab_ref, out_ref, buf, sem, obuf):
    i = pl.program_id(0)
    n_steps = pl.num_programs(0)
    tb, D = s_ref.shape
    hk = g_ref.shape[1]
    gt = PEER_TOK_GROUP
    rows = gt * hk
    n_groups = tb // gt

    def issue(ref, grp, slot):
        def body(r, carry):
            e = ref[grp * rows + r]
            pltpu.make_async_copy(tab_ref.at[pl.ds(e, 1)], buf.at[slot, pl.ds(r, 1)], sem.at[slot]).start()
            return carry
        lax.fori_loop(0, rows, body, 0, unroll=8)

    def wait(slot):
        pltpu.make_async_copy(tab_ref.at[pl.ds(0, rows)], buf.at[slot], sem.at[slot]).wait()

    @pl.when(i == 0)
    def _():
        issue(idx_ref, 0, 0)

    m = m_ref[0]
    s_rows = s_ref[...]
    h_rows = _modulate(s_rows, m, 3)
    gates = g_ref[0]
    for grp in range(n_groups):
        slot = grp % 2
        if grp + 1 < n_groups:
            issue(idx_ref, grp + 1, 1 - slot)
        else:
            @pl.when(i + 1 < n_steps)
            def _():
                issue(nxt_ref, 0, 1 - slot)
        wait(slot)
        for t in range(gt):
            tok = grp * gt + t
            r = buf[slot, t * hk:(t + 1) * hk, :]
            sc = jnp.sum(r[:, 0:D] * h_rows[tok:tok + 1, :], axis=1, keepdims=True)
            coef = gates[:, tok:tok + 1] * _gelu_tanh(sc)
            obuf[tok:tok + 1, :] = jnp.sum(coef * r[:, D:2 * D], axis=0, keepdims=True)
    z = DEEPNORM_ALPHA * s_rows + m[5:6] * obuf[...]
    out_ref[...] = _layer_norm_rows(z, lg_ref[...], lb_ref[...])


def _peer_gather(S, M, gates_t, eidx, table, ln_g, ln_b, ctx_len, skip_ctx):
    B, P, D = S.shape
    tb = PEER_TOK_BLOCK
    assert (tb // PEER_TOK_GROUP) % 2 == 0
    hk = PEER_HEADS * PEER_TOPK
    j0 = ctx_len // tb if skip_ctx else 0
    per_b = P // tb - j0
    n_steps = B * per_b
    nctx = ctx_len // tb
    S2 = S.reshape(B * P, D)
    M2 = M.reshape(B * 2, SUBLANES, D)
    row_blk = lambda i: (i // per_b) * (P // tb) + i % per_b + j0
    kern = _peer_gather_kernel
    out = pl.pallas_call(
        kern,
        grid=(n_steps,),
        in_specs=[pl.BlockSpec((tb * hk,), lambda i: (i,), memory_space=pltpu.SMEM),
                  pl.BlockSpec((tb * hk,), lambda i: (jnp.minimum(i + 1, n_steps - 1),), memory_space=pltpu.SMEM),
                  pl.BlockSpec((1, hk, tb), lambda i: (i, 0, 0)),
                  pl.BlockSpec((tb, D), lambda i: (row_blk(i), 0)),
                  pl.BlockSpec((1, SUBLANES, D),
                               lambda i: ((i // per_b) * 2 + jnp.minimum((i % per_b + j0) // nctx, 1), 0, 0)),
                  pl.BlockSpec((1, D), lambda i: (0, 0)),
                  pl.BlockSpec((1, D), lambda i: (0, 0)),
                  pl.BlockSpec(memory_space=pl.ANY)],
        out_specs=pl.BlockSpec((tb, D), lambda i: (row_blk(i), 0)),
        out_shape=jax.ShapeDtypeStruct((B * P, D), F32),
        scratch_shapes=[pltpu.VMEM((2, PEER_TOK_GROUP * hk, 2 * D), F32),
                        pltpu.SemaphoreType.DMA((2,)),
                        pltpu.VMEM((tb, D), F32)],
        input_output_aliases={3: 0},
        compiler_params=_cparams(1),
    )(eidx, eidx, gates_t, S2, M2, ln_g.reshape(1, D), ln_b.reshape(1, D), table)
    return out.reshape(B, P, D)


def _peer_ffn_resid_ln(S, M, wq, keys, u_tab, v_tab, ln_g, ln_b, ctx_len, skip_ctx):
    B, P, D = S.shape
    wq_t = wq.T.astype(BF16)
    gates, eidx = _peer_topk(S, M, wq_t, keys.astype(BF16), ctx_len, skip_ctx)
    hk = gates.shape[1]
    tb = PEER_TOK_BLOCK
    gates_t = jnp.transpose(gates.reshape(B, hk, -1, tb), (0, 2, 1, 3)).reshape(-1, hk, tb)
    eidx_flat = jnp.transpose(eidx, (0, 2, 1)).reshape(-1)
    table = jnp.concatenate([u_tab, v_tab], axis=1)
    return _peer_gather(S, M, gates_t, eidx_flat, table, ln_g, ln_b, ctx_len, skip_ctx)


def _deinterleave_heads(w):
    d_in, n = w.shape
    return w.reshape(d_in, n // HEAD_DIM, HEAD_DIM // 2, 2).transpose(0, 1, 3, 2).reshape(d_in, n)


def _rope_tables(ctx_len, n_lat):
    rows = n_lat // GRID_W
    row = jnp.repeat(jnp.arange(rows, dtype=F32), GRID_W)
    col = jnp.tile(jnp.arange(GRID_W, dtype=F32), rows)
    n_freq = HEAD_DIM // 4
    inv = ROPE_BASE ** (-jnp.arange(n_freq, dtype=F32) / n_freq)
    ang = jnp.concatenate([row[:, None] * inv, col[:, None] * inv], -1)
    cos, sin = jnp.cos(ang), jnp.sin(ang)
    reps = LANES // HEAD_DIM
    cos_t = jnp.tile(jnp.concatenate([cos, cos], -1), (1, reps))
    sin_t = jnp.tile(jnp.concatenate([-sin, sin], -1), (1, reps))
    cos_t = jnp.concatenate([jnp.ones((ctx_len, LANES), F32), cos_t], 0)
    sin_t = jnp.concatenate([jnp.zeros((ctx_len, LANES), F32), sin_t], 0)
    return cos_t, sin_t


def _mixer_layer(i, S, M, cos_t, sin_t, L, last, da_wqkv, da_wo, da_lambda, da_subln, wa_wqkv, wa_wo, wa_sink,
                 ssm_lam_re, ssm_lam_im, ssm_log_step, ssm_b_re, ssm_b_im, ssm_c_re, ssm_c_im, ssm_d, ssm_w_glu,
                 ln_g, ln_b):
    D = S.shape[2]
    kind, j = i % N_MIXERS, i // N_MIXERS
    if kind == 0:
        lam_init = 0.8 - 0.6 * math.exp(-0.3 * i)
        w = da_wqkv[j]
        n_qk = 2 * DA_HEADS * 2 * HEAD_DIM
        w = jnp.concatenate([_deinterleave_heads(w[:, :n_qk]), w[:, n_qk:]], axis=1).astype(BF16)
        qkv = _project(S, M, w, cos_t, sin_t, n_qk, L)
        O = _diff_attention(qkv, da_lambda[j], da_subln[j], lam_init, L, not last)
        return _out_proj_resid_ln(O, da_wo[j].astype(BF16), S, M, ln_g[i, 0], ln_b[i, 0], L, last)
    if kind == 1:
        w = wa_wqkv[j]
        nq = WA_Q_HEADS * HEAD_DIM
        nkv = WA_KV_HEADS * HEAD_DIM
        rep = lambda m: jnp.tile(m.reshape(D, WA_KV_HEADS, 1, HEAD_DIM), (1, 1, WA_REP, 1)).reshape(D, nq)
        w = jnp.concatenate([_deinterleave_heads(w[:, :nq]),
                             rep(_deinterleave_heads(w[:, nq:nq + nkv])),
                             rep(w[:, nq + nkv:])], axis=1).astype(BF16)
        qkv = _project(S, M, w, cos_t, sin_t, 2 * nq, L)
        O = _window_attention(qkv, wa_sink[j], L, not last)
        return _out_proj_resid_ln(O, wa_wo[j].astype(BF16), S, M, ln_g[i, 0], ln_b[i, 0], L, last)
    bmat, cmat, lam_r, lam_i = _s5_params(ssm_lam_re[j], ssm_lam_im[j], ssm_log_step[j], ssm_b_re[j],
                                          ssm_b_im[j], ssm_c_re[j], ssm_c_im[j])
    Mt = jnp.transpose(M[:, :, 0:2, :], (1, 2, 0, 3))
    Y_tm = _s5_scan(jnp.transpose(S, (1, 0, 2)), Mt, bmat, cmat, lam_r, lam_i, L)
    Y = jnp.transpose(Y_tm, (0, 2, 1, 3))
    return _glu_resid_ln(S, Y, ssm_d[j], ssm_w_glu[j].astype(BF16), M, ln_g[i, 0], ln_b[i, 0], L)


def kernel(x, c, ctx, c_ctx, mod_w, mod_b, ln_g, ln_b, peer_wq, peer_keys, peer_u, peer_v, da_wqkv, da_wo, da_lambda, da_subln, wa_wqkv, wa_wo, wa_sink, ssm_lam_re, ssm_lam_im, ssm_log_step, ssm_b_re, ssm_b_im, ssm_c_re, ssm_c_im, ssm_d, ssm_w_glu):
    B, T, D = x.shape
    L = ctx.shape[1]
    depth = mod_w.shape[0]
    cos_t, sin_t = _rope_tables(L, T)
    M_all = _mod_vectors(c, c_ctx, mod_w, mod_b)
    S = jnp.concatenate([ctx, x], axis=1)
    for i in range(depth):
        last = i == depth - 1
        S = _mixer_layer(i, S, M_all[i], cos_t, sin_t, L, last, da_wqkv, da_wo, da_lambda, da_subln, wa_wqkv, wa_wo,
                         wa_sink, ssm_lam_re, ssm_lam_im, ssm_log_step, ssm_b_re, ssm_b_im, ssm_c_re, ssm_c_im,
                         ssm_d, ssm_w_glu, ln_g, ln_b)
        S = _peer_ffn_resid_ln(S, M_all[i], peer_wq[i], peer_keys[i], peer_u[i], peer_v[i], ln_g[i, 1], ln_b[i, 1],
                               L, last)
    return S[:, L:, :]
```

```python
import functools
import math

import jax
import jax.numpy as jnp
from jax import lax
from jax.experimental import pallas as pl
from jax.experimental.pallas import tpu as pltpu

F32 = jnp.float32
BF16 = jnp.bfloat16

DEPTH = 4
N_MIXERS = 3
GRID_W = 64
HEAD_DIM = 64
ROPE_BASE = 10000.0
DA_HEADS = 8
WA_Q_HEADS = 16
WA_KV_HEADS = 4
WA_REP = WA_Q_HEADS // WA_KV_HEADS
WINDOW = 128
SSM_GROUP = 16
SSM_STATE = 64
PEER_HEADS = 8
PEER_NKEYS = 128
PEER_QDIM = 256
PEER_TOPK = 16
LN_EPS = 1e-5
DEEPNORM_ALPHA = (2 * DEPTH) ** 0.25
ATT_SCALE = HEAD_DIM ** -0.5

LANES = 128
SUBLANES = 8
ROW_TILE = 256
WIN_TILE = 128
SCAN_CHUNK = 128
SSM_SLAB_GROUPS = 8
PEER_TOK_BLOCK = 32
PEER_TOK_GROUP = 8
PEER_LOOKAHEAD = 2
VMEM_LIMIT = 48 * 1024 * 1024
NEG_INF = float("-inf")


def _cparams(n_axes):
    return pltpu.CompilerParams(dimension_semantics=("arbitrary",) * n_axes, vmem_limit_bytes=VMEM_LIMIT)


def _gelu_tanh(x):
    return 0.5 * x * (1.0 + jnp.tanh(math.sqrt(2.0 / math.pi) * (x + 0.044715 * (x * x * x))))


def _sigmoid(x):
    return 1.0 / (1.0 + jnp.exp(-x))


def _layer_norm_rows(z, g, b):
    mu = jnp.mean(z, axis=-1, keepdims=True)
    zc = z - mu
    var = jnp.mean(zc * zc, axis=-1, keepdims=True)
    return zc * lax.rsqrt(var + LN_EPS) * g + b


def _modulate(x, m, shift_idx):
    return x * (1.0 + m[shift_idx + 1:shift_idx + 2]) + m[shift_idx:shift_idx + 1]


def _mod_kernel(a_ref, w_ref, b_ref, o_ref):
    a = a_ref[...]
    a = a * _sigmoid(a)
    o_ref[0] = jnp.dot(a, w_ref[0], preferred_element_type=F32, precision=lax.Precision.HIGHEST) + b_ref[0]


def _mod_vectors(c, c_ctx, mod_w, mod_b):
    B, D = c.shape
    depth, _, n6 = mod_w.shape
    rows = -(-(B + 1) // SUBLANES) * SUBLANES
    a = jnp.zeros((rows, D), F32).at[:B].set(c).at[B].set(c_ctx)
    tn = n6 // 4
    out = pl.pallas_call(
        _mod_kernel,
        grid=(depth, n6 // tn),
        in_specs=[pl.BlockSpec((rows, D), lambda i, j: (0, 0)),
                  pl.BlockSpec((1, D, tn), lambda i, j: (i, 0, j)),
                  pl.BlockSpec((1, 1, tn), lambda i, j: (i, 0, j))],
        out_specs=pl.BlockSpec((1, rows, tn), lambda i, j: (i, 0, j)),
        out_shape=jax.ShapeDtypeStruct((depth, rows, n6), F32),
        compiler_params=_cparams(2),
    )(a, mod_w, mod_b.reshape(depth, 1, n6))
    lat = out[:, :B].reshape(depth, B, 6, D)
    ctx = jnp.broadcast_to(out[:, B].reshape(depth, 1, 6, D), (depth, B, 6, D))
    m = jnp.stack([ctx, lat], axis=2)
    return jnp.pad(m, ((0, 0), (0, 0), (0, 0), (0, 2), (0, 0)))


def _mod_spec(D, nctx_blocks, j0):
    return pl.BlockSpec((1, 1, SUBLANES, D), lambda b, j: (b, jnp.minimum((j + j0) // nctx_blocks, 1), 0, 0))


def _proj_kernel(s_ref, m_ref, w_ref, cos_ref, sin_ref, o_ref, *, n_rope, tn):
    xb = _modulate(s_ref[0], m_ref[0, 0], 0).astype(BF16)
    tm = xb.shape[0]
    n_out = w_ref.shape[1]
    cos_t = cos_ref[...]
    sin_t = sin_ref[...]
    lane = lax.broadcasted_iota(jnp.int32, (tm, LANES), 1)
    first_half = (lane % HEAD_DIM) < (HEAD_DIM // 2)
    for c0 in range(0, n_out, tn):
        y = jnp.dot(xb, w_ref[:, c0:c0 + tn], preferred_element_type=F32)
        if c0 < n_rope:
            pieces = []
            for l0 in range(0, tn, LANES):
                yc = y[:, l0:l0 + LANES]
                partner = jnp.where(first_half,
                                    pltpu.roll(yc, LANES - HEAD_DIM // 2, 1),
                                    pltpu.roll(yc, HEAD_DIM // 2, 1))
                pieces.append(yc * cos_t + partner * sin_t)
            y = jnp.concatenate(pieces, axis=1)
        o_ref[0, :, c0:c0 + tn] = y.astype(o_ref.dtype)


def _project(S, M, w, cos_t, sin_t, n_rope, ctx_len):
    B, P, D = S.shape
    n_out = w.shape[1]
    tm = ROW_TILE
    kern = functools.partial(_proj_kernel, n_rope=n_rope, tn=512)
    return pl.pallas_call(
        kern,
        grid=(B, P // tm),
        in_specs=[pl.BlockSpec((1, tm, D), lambda b, j: (b, j, 0)),
                  _mod_spec(D, ctx_len // tm, 0),
                  pl.BlockSpec((D, n_out), lambda b, j: (0, 0)),
                  pl.BlockSpec((tm, LANES), lambda b, j: (j, 0)),
                  pl.BlockSpec((tm, LANES), lambda b, j: (j, 0))],
        out_specs=pl.BlockSpec((1, tm, n_out), lambda b, j: (b, j, 0)),
        out_shape=jax.ShapeDtypeStruct((B, P, n_out), BF16),
        compiler_params=_cparams(2),
    )(S, M, w, cos_t, sin_t)


def _dattn_kernel(lam_ref, g_ref, q_ref, k_ref, v_ref, o_ref, *, lam_init, ctx_len, ctx_queries):
    lp = lam_ref[...]
    lam = (jnp.exp(jnp.sum(lp[0:1] * lp[1:2], axis=-1, keepdims=True))
           - jnp.exp(jnp.sum(lp[2:3] * lp[3:4], axis=-1, keepdims=True)) + lam_init)
    q = q_ref[0]
    lane = lax.broadcasted_iota(jnp.int32, q.shape, 1)
    zero = jnp.zeros_like(q)
    q_maps = (jnp.where(lane < HEAD_DIM, q, zero), jnp.where(lane >= HEAD_DIM, q, zero))

    def attend(nk):
        k = k_ref[0, 0:nk, :]
        v = v_ref[0, 0:nk, :]
        probs = []
        for qm in q_maps:
            s = lax.dot_general(qm, k, (((1,), (1,)), ((), ())), preferred_element_type=F32) * ATT_SCALE
            p = jnp.exp(s - jnp.max(s, axis=-1, keepdims=True))
            probs.append((p, 1.0 / jnp.sum(p, axis=-1, keepdims=True)))
        a = probs[0][0] * probs[0][1] - probs[1][0] * (lam * probs[1][1])
        o = jnp.dot(a.astype(BF16), v, preferred_element_type=F32)
        o = o * lax.rsqrt(jnp.mean(o * o, axis=-1, keepdims=True) + LN_EPS) * g_ref[...] * (1.0 - lam_init)
        o_ref[0] = o.astype(o_ref.dtype)

    if ctx_queries:
        qi = pl.program_id(2)

        @pl.when(qi == 0)
        def _():
            attend(ctx_len)

        @pl.when(qi > 0)
        def _():
            attend(k_ref.shape[1])
    else:
        attend(k_ref.shape[1])


def _diff_attention(qkv, lam_p, subln_g, lam_init, ctx_len, ctx_queries):
    B, P, _ = qkv.shape
    tq = ROW_TILE
    assert ctx_len == tq
    j0 = 0 if ctx_queries else 1
    H = DA_HEADS
    kern = functools.partial(_dattn_kernel, lam_init=lam_init, ctx_len=ctx_len, ctx_queries=ctx_queries)
    return pl.pallas_call(
        kern,
        grid=(B, H, P // tq - j0),
        in_specs=[pl.BlockSpec((4, HEAD_DIM), lambda b, h, i: (0, 0)),
                  pl.BlockSpec((1, 2 * HEAD_DIM), lambda b, h, i: (0, 0)),
                  pl.BlockSpec((1, tq, LANES), lambda b, h, i: (b, i + j0, h)),
                  pl.BlockSpec((1, P, LANES), lambda b, h, i: (b, 0, H + h)),
                  pl.BlockSpec((1, P, LANES), lambda b, h, i: (b, 0, 2 * H + h))],
        out_specs=pl.BlockSpec((1, tq, LANES), lambda b, h, i: (b, i + j0, h)),
        out_shape=jax.ShapeDtypeStruct((B, P, H * LANES), BF16),
        compiler_params=_cparams(3),
    )(lam_p, subln_g.reshape(1, -1), qkv, qkv, qkv)


def _wattn_kernel(sink_ref, q_ref, kc_ref, k0_ref, k1_ref, k2_ref, vc_ref, v0_ref, v1_ref, v2_ref, o_ref,
                  *, n_ctx_blocks, n_lat_blocks, j0):
    qb = pl.program_id(1) + j0
    g = pl.program_id(2)
    tq = q_ref.shape[1]
    width = q_ref.shape[2]
    q = q_ref[0]
    lane = lax.broadcasted_iota(jnp.int32, q.shape, 1)
    zero = jnp.zeros_like(q)
    qm = jnp.concatenate(
        [jnp.where((lane >= r * HEAD_DIM) & (lane < (r + 1) * HEAD_DIM), q, zero) for r in range(WA_REP)], axis=0)
    sink_col = jnp.concatenate([jnp.full((tq, 1), sink_ref[g * WA_REP + r], F32) for r in range(WA_REP)], axis=0)
    dims = (((1,), (1,)), ((), ()))

    def finish(o_all, denom):
        o_all = o_all / denom
        lane_o = lax.broadcasted_iota(jnp.int32, (tq, width), 1)
        o = jnp.zeros((tq, width), F32)
        for r in range(WA_REP):
            sel = (lane_o >= r * HEAD_DIM) & (lane_o < (r + 1) * HEAD_DIM)
            o = o + jnp.where(sel, o_all[r * tq:(r + 1) * tq, :], 0.0)
        o_ref[0] = o.astype(o_ref.dtype)

    @pl.when(qb < n_ctx_blocks)
    def _():
        s_c = lax.dot_general(qm, kc_ref[0], dims, preferred_element_type=F32) * ATT_SCALE
        m = jnp.maximum(jnp.max(s_c, axis=-1, keepdims=True), sink_col)
        p_c = jnp.exp(s_c - m)
        denom = jnp.sum(p_c, axis=-1, keepdims=True) + jnp.exp(sink_col - m)
        finish(jnp.dot(p_c.astype(BF16), vc_ref[0], preferred_element_type=F32), denom)

    @pl.when(qb >= n_ctx_blocks)
    def _():
        lb = qb - n_ctx_blocks
        kw = jnp.concatenate([k0_ref[0], k1_ref[0], k2_ref[0]], axis=0)
        vw = jnp.concatenate([v0_ref[0], v1_ref[0], v2_ref[0]], axis=0)
        s_c = lax.dot_general(qm, kc_ref[0], dims, preferred_element_type=F32) * ATT_SCALE
        s_w = lax.dot_general(qm, kw, dims, preferred_element_type=F32) * ATT_SCALE
        iq = lax.broadcasted_iota(jnp.int32, s_w.shape, 0) % tq
        ik = lax.broadcasted_iota(jnp.int32, s_w.shape, 1)
        lo = jnp.where(lb == 0, tq, 0)
        hi = jnp.where(lb == n_lat_blocks - 1, 2 * tq, 3 * tq)
        band = (ik >= iq) & (ik <= iq + 2 * WINDOW) & (ik >= lo) & (ik < hi)
        s_w = jnp.where(band, s_w, NEG_INF)
        m = jnp.maximum(jnp.maximum(jnp.max(s_c, axis=-1, keepdims=True), jnp.max(s_w, axis=-1, keepdims=True)),
                        sink_col)
        p_c = jnp.exp(s_c - m)
        p_w = jnp.exp(s_w - m)
        denom = (jnp.sum(p_c, axis=-1, keepdims=True) + jnp.sum(p_w, axis=-1, keepdims=True)
                 + jnp.exp(sink_col - m))
        o_all = (jnp.dot(p_c.astype(BF16), vc_ref[0], preferred_element_type=F32)
                 + jnp.dot(p_w.astype(BF16), vw, preferred_element_type=F32))
        finish(o_all, denom)


def _window_attention(qkv, sink, ctx_len, ctx_queries):
    B, P, _ = qkv.shape
    tq = WIN_TILE
    assert tq == WINDOW
    nctx = ctx_len // tq
    nlat = P // tq - nctx
    j0 = 0 if ctx_queries else nctx
    G = WA_KV_HEADS
    width = WA_REP * HEAD_DIM

    def lat_spec(off, col0):
        return pl.BlockSpec((1, tq, width),
                            lambda b, j, g: (b, jnp.clip(j + j0 - nctx + off, 0, nlat - 1) + nctx, col0 + g))

    q_spec = pl.BlockSpec((1, tq, width), lambda b, j, g: (b, j + j0, g))
    kc_spec = pl.BlockSpec((1, ctx_len, width), lambda b, j, g: (b, 0, G + g))
    vc_spec = pl.BlockSpec((1, ctx_len, width), lambda b, j, g: (b, 0, 2 * G + g))
    kern = functools.partial(_wattn_kernel, n_ctx_blocks=nctx, n_lat_blocks=nlat, j0=j0)
    return pl.pallas_call(
        kern,
        grid=(B, P // tq - j0, G),
        in_specs=([pl.BlockSpec(memory_space=pltpu.SMEM), q_spec, kc_spec]
                  + [lat_spec(off, G) for off in (-1, 0, 1)] + [vc_spec]
                  + [lat_spec(off, 2 * G) for off in (-1, 0, 1)]),
        out_specs=pl.BlockSpec((1, tq, width), lambda b, j, g: (b, j + j0, g)),
        out_shape=jax.ShapeDtypeStruct((B, P, G * width), BF16),
        compiler_params=_cparams(3),
    )(sink, qkv, qkv, qkv, qkv, qkv, qkv, qkv, qkv, qkv)


def _resid_ln_kernel(o_ref, w_ref, s_ref, m_ref, g_ref, b_ref, out_ref):
    y = jnp.dot(o_ref[0], w_ref[...], preferred_element_type=F32)
    z = DEEPNORM_ALPHA * s_ref[0] + m_ref[0, 0][2:3] * y
    out_ref[0] = _layer_norm_rows(z, g_ref[...], b_ref[...])


def _out_proj_resid_ln(O, w, S, M, ln_g, ln_b, ctx_len, skip_ctx):
    B, P, D = S.shape
    kdim = O.shape[2]
    tm = ROW_TILE
    j0 = ctx_len // tm if skip_ctx else 0
    row = lambda b, j: (b, j + j0, 0)
    return pl.pallas_call(
        _resid_ln_kernel,
        grid=(B, P // tm - j0),
        in_specs=[pl.BlockSpec((1, tm, kdim), row),
                  pl.BlockSpec((kdim, D), lambda b, j: (0, 0)),
                  pl.BlockSpec((1, tm, D), row),
                  _mod_spec(D, ctx_len // tm, j0),
                  pl.BlockSpec((1, D), lambda b, j: (0, 0)),
                  pl.BlockSpec((1, D), lambda b, j: (0, 0))],
        out_specs=pl.BlockSpec((1, tm, D), row),
        out_shape=jax.ShapeDtypeStruct((B, P, D), F32),
        input_output_aliases={2: 0},
        compiler_params=_cparams(2),
    )(O, w, S, M, ln_g.reshape(1, D), ln_b.reshape(1, D))


def _s5_kernel(s_ref, mt_ref, bm_ref, cm_ref, lr_ref, li_ref, y_ref, xbuf, state):
    d = pl.program_id(0)
    c = pl.program_id(2)
    tc, nb, w_in = s_ref.shape
    half = lr_ref.shape[-1]

    @pl.when(c == 0)
    def _():
        state[...] = jnp.zeros_like(state)

    mt = mt_ref[0]
    u = s_ref[...] * (1.0 + mt[1][None]) + mt[0][None]
    u2 = u.reshape(tc * nb, w_in).astype(BF16)
    xbuf[...] = jnp.dot(u2, bm_ref[0, 0], preferred_element_type=F32)
    lr = jnp.broadcast_to(lr_ref[0, 0], (nb, half))
    li = jnp.broadcast_to(li_ref[0, 0], (nb, half))

    def step(t, carry):
        sr, si = carry
        tt = jnp.where(d == 0, t, tc - 1 - t)
        r0 = pl.multiple_of(tt * nb, nb)
        nr = lr * sr - li * si + xbuf[pl.ds(r0, nb), 0:half]
        ni = lr * si + li * sr + xbuf[pl.ds(r0, nb), half:2 * half]
        xbuf[pl.ds(r0, nb), 0:half] = nr
        xbuf[pl.ds(r0, nb), half:2 * half] = ni
        return nr, ni

    sr, si = lax.fori_loop(0, tc, step, (state[:, 0:half], state[:, half:2 * half]), unroll=4)
    state[:, 0:half] = sr
    state[:, half:2 * half] = si
    y = jnp.dot(xbuf[...].astype(BF16), cm_ref[0, 0], preferred_element_type=F32)
    y_ref[0] = y.reshape(tc, nb, w_in)


def _s5_scan(S_tm, Mt, bmat, cmat, lam_r, lam_i, ctx_len):
    P, B, D = S_tm.shape
    tc = SCAN_CHUNK
    w_in = SSM_SLAB_GROUPS * SSM_GROUP
    n_slab = D // w_in
    half = SSM_SLAB_GROUPS * SSM_STATE
    nch = P // tc
    nctx = ctx_len // tc
    nlat = nch - nctx

    def chunk(d, c):
        q = nch - 1 - c
        back = jnp.where(q < nlat, q + nctx, q - nlat)
        return jnp.where(d == 0, c, back)

    return pl.pallas_call(
        _s5_kernel,
        grid=(2, n_slab, nch),
        in_specs=[pl.BlockSpec((tc, B, w_in), lambda d, s, c: (chunk(d, c), 0, s)),
                  pl.BlockSpec((1, 2, B, w_in), lambda d, s, c: (jnp.minimum(chunk(d, c) // nctx, 1), 0, 0, s)),
                  pl.BlockSpec((1, 1, w_in, 2 * half), lambda d, s, c: (d, s, 0, 0)),
                  pl.BlockSpec((1, 1, 2 * half, w_in), lambda d, s, c: (d, s, 0, 0)),
                  pl.BlockSpec((1, 1, 1, half), lambda d, s, c: (d, s, 0, 0)),
                  pl.BlockSpec((1, 1, 1, half), lambda d, s, c: (d, s, 0, 0))],
        out_specs=pl.BlockSpec((1, tc, B, w_in), lambda d, s, c: (d, chunk(d, c), 0, s)),
        out_shape=jax.ShapeDtypeStruct((2, P, B, D), F32),
        scratch_shapes=[pltpu.VMEM((tc * B, 2 * half), F32), pltpu.VMEM((B, 2 * half), F32)],
        compiler_params=_cparams(3),
    )(S_tm, Mt, bmat, cmat, lam_r, lam_i)


def _s5_params(lam_re, lam_im, log_step, b_re, b_im, c_re, c_im):
    lam = lax.complex(lam_re.astype(F32), lam_im.astype(F32))
    step = jnp.exp(log_step.astype(F32))[..., None]
    lam_bar = jnp.exp(lam * step)
    b_bar = lax.complex(b_re.astype(F32), b_im.astype(F32)) * ((lam_bar - 1) / lam)[..., None]
    G, Pst, Hg = b_bar.shape[1:]
    ng = SSM_SLAB_GROUPS
    ns = G // ng
    eye = jnp.eye(ng, dtype=F32)

    def bdiag_in(x):
        x = x.reshape(2, ns, ng, Pst, Hg)
        return jnp.einsum('rsgph,gk->rsghkp', x, eye).reshape(2, ns, ng * Hg, ng * Pst)

    def bdiag_out(x):
        x = x.reshape(2, ns, ng, Hg, Pst)
        return jnp.einsum('rsghp,gk->rsgpkh', x, eye).reshape(2, ns, ng * Pst, ng * Hg)

    bmat = jnp.concatenate([bdiag_in(jnp.real(b_bar)), bdiag_in(jnp.imag(b_bar))], axis=-1).astype(BF16)
    cmat = jnp.concatenate([bdiag_out(c_re.astype(F32)), bdiag_out(-c_im.astype(F32))], axis=-2).astype(BF16)
    lam_r = jnp.real(lam_bar).reshape(2, ns, 1, ng * Pst)
    lam_i = jnp.imag(lam_bar).reshape(2, ns, 1, ng * Pst)
    return bmat, cmat, lam_r, lam_i


def _glu_ln_kernel(s_ref, y0_ref, y1_ref, d_ref, w_ref, m_ref, g_ref, b_ref, out_ref):
    m = m_ref[0, 0]
    s = s_ref[0]
    y = d_ref[...] * _modulate(s, m, 0) + y0_ref[0, 0] + y1_ref[0, 0]
    z = jnp.dot(_gelu_tanh(y).astype(BF16), w_ref[...], preferred_element_type=F32)
    D = s.shape[1]
    o = z[:, 0:D] * _sigmoid(z[:, D:2 * D])
    out_ref[0] = _layer_norm_rows(DEEPNORM_ALPHA * s + m[2:3] * o, g_ref[...], b_ref[...])


def _glu_resid_ln(S, Y, d_skip, w_glu, M, ln_g, ln_b, ctx_len):
    B, P, D = S.shape
    tm = ROW_TILE
    row = lambda b, j: (b, j, 0)
    return pl.pallas_call(
        _glu_ln_kernel,
        grid=(B, P // tm),
        in_specs=[pl.BlockSpec((1, tm, D), row),
                  pl.BlockSpec((1, 1, tm, D), lambda b, j: (0, b, j, 0)),
                  pl.BlockSpec((1, 1, tm, D), lambda b, j: (1, b, j, 0)),
                  pl.BlockSpec((1, D), lambda b, j: (0, 0)),
                  pl.BlockSpec((D, 2 * D), lambda b, j: (0, 0)),
                  _mod_spec(D, ctx_len // tm, 0),
                  pl.BlockSpec((1, D), lambda b, j: (0, 0)),
                  pl.BlockSpec((1, D), lambda b, j: (0, 0))],
        out_specs=pl.BlockSpec((1, tm, D), row),
        out_shape=jax.ShapeDtypeStruct((B, P, D), F32),
        compiler_params=_cparams(2),
    )(S, Y, Y, d_skip.reshape(1, D), w_glu, M, ln_g.reshape(1, D), ln_b.reshape(1, D))


def _peer_stair():
    blocks = []
    for a in range(PEER_TOPK):
        nb = PEER_TOPK // (a + 1)
        blocks.append((a, nb, PEER_TOPK if a == 0 else SUBLANES))
    return blocks


def _peer_topk_kernel(s_ref, m_ref, wqt_ref, keys_ref, g_out, e_out, qt):
    tn = s_ref.shape[1]
    nk = PEER_NKEYS
    k = PEER_TOPK
    xb = _modulate(s_ref[0], m_ref[0, 0], 3).astype(BF16)
    qt[...] = lax.dot_general(wqt_ref[...], xb, (((1,), (1,)), ((), ())), preferred_element_type=F32)
    stair = _peer_stair()

    def head(h, carry):
        krow = lax.broadcasted_iota(jnp.int32, (nk, tn), 0).astype(F32)
        rank = lax.broadcasted_iota(jnp.int32, (k, tn), 0)
        rank8 = lax.broadcasted_iota(jnp.int32, (SUBLANES, tn), 0)
        pos_const = jnp.concatenate(
            [(a * k + lax.broadcasted_iota(jnp.int32, (rows, tn), 0)).astype(F32) for a, _, rows in stair], axis=0)
        tops = []
        for j in range(2):
            qs = qt[pl.ds(pl.multiple_of(h * 2 * nk + j * nk, nk), nk), :].astype(BF16)
            s = jnp.dot(keys_ref[j, h], qs, preferred_element_type=F32)
            top = []
            for it in range(k):
                mx = jnp.max(s, axis=0, keepdims=True)
                ix = jnp.min(jnp.where(s == mx, krow, float(nk)), axis=0, keepdims=True)
                s = jnp.where(krow == ix, NEG_INF, s)
                top.append((mx, ix))
            tops.append(top)
        sv16 = jnp.zeros((k, tn), F32)
        si16 = jnp.zeros((k, tn), F32)
        sv8 = jnp.zeros((SUBLANES, tn), F32)
        si8 = jnp.zeros((SUBLANES, tn), F32)
        for it, (mx, ix) in enumerate(tops[1]):
            sv16 = jnp.where(rank == it, mx, sv16)
            si16 = jnp.where(rank == it, ix, si16)
            if it < SUBLANES:
                sv8 = jnp.where(rank8 == it, mx, sv8)
                si8 = jnp.where(rank8 == it, ix, si8)
        cs, ci = [], []
        for a, nb, rows in stair:
            mx0, ix0 = tops[0][a]
            cv = mx0 + (sv16 if rows == k else sv8)
            ce = ix0 * float(nk) + (si16 if rows == k else si8)
            if nb < rows:
                cv = jnp.where(rank8 < nb, cv, NEG_INF)
            cs.append(cv)
            ci.append(ce)
        cand = jnp.concatenate(cs, axis=0)
        cexp = jnp.concatenate(ci, axis=0)
        ts = jnp.zeros((k, tn), F32)
        te = jnp.zeros((k, tn), F32)
        best = None
        for it in range(k):
            mx = jnp.max(cand, axis=0, keepdims=True)
            sel = jnp.min(jnp.where(cand == mx, pos_const, float(k * k)), axis=0, keepdims=True)
            hit = pos_const == sel
            ex = jnp.max(jnp.where(hit, cexp, -1.0), axis=0, keepdims=True)
            cand = jnp.where(hit, NEG_INF, cand)
            ts = jnp.where(rank == it, mx, ts)
            te = jnp.where(rank == it, ex, te)
            best = mx if best is None else best
        p = jnp.exp(ts - best)
        gate = p / jnp.sum(p, axis=0, keepdims=True)
        r0 = pl.multiple_of(h * k, k)
        g_out[0, pl.ds(r0, k), :] = gate
        e_out[0, pl.ds(r0, k), :] = te.astype(jnp.int32)
        return carry

    lax.fori_loop(0, PEER_HEADS, head, 0)


def _peer_topk(S, M, wq_t, keys, ctx_len, skip_ctx):
    B, P, D = S.shape
    tn = ROW_TILE
    j0 = ctx_len // tn if skip_ctx else 0
    nblk = P // tn - j0
    hk = PEER_HEADS * PEER_TOPK
    out_spec = pl.BlockSpec((1, hk, tn), lambda b, j: (b, 0, j))
    return pl.pallas_call(
        _peer_topk_kernel,
        grid=(B, nblk),
        in_specs=[pl.BlockSpec((1, tn, D), lambda b, j: (b, j + j0, 0)),
                  _mod_spec(D, ctx_len // tn, j0),
                  pl.BlockSpec(wq_t.shape, lambda b, j: (0, 0)),
                  pl.BlockSpec(keys.shape, lambda b, j: (0, 0, 0, 0))],
        out_specs=[out_spec, out_spec],
        out_shape=[jax.ShapeDtypeStruct((B, hk, nblk * tn), F32),
                   jax.ShapeDtypeStruct((B, hk, nblk * tn), jnp.int32)],
        scratch_shapes=[pltpu.VMEM((wq_t.shape[0], tn), F32)],
        compiler_params=_cparams(2),
    )(S, M, wq_t, keys)


def _peer_gather_kernel(idx_ref, nxt_ref, g_ref, s_ref, m_ref, lg_ref, lb_ref, tab_ref, out_ref, buf, sem, obuf):
    i = pl.program_id(0)
    n_steps = pl.num_programs(0)
    tb, D = s_ref.shape
    hk = g_ref.shape[1]
    gt = PEER_TOK_GROUP
    rows = gt * hk
    n_groups = tb // gt
    nlt = D // LANES
    pgs = hk // SUBLANES

    def issue(ref, grp, slot):
        vregs_per_it = 4
        per_it = vregs_per_it * SUBLANES

        def body(it, carry):
            for k in range(per_it):
                e = ref[grp * rows + it * per_it + k]
                pltpu.make_async_copy(tab_ref.at[e],
                                      buf.at[slot, it * vregs_per_it + k // SUBLANES, :, k % SUBLANES],
                                      sem.at[slot]).start(priority=k % 2)
            return carry

        lax.fori_loop(0, rows // per_it, body, 0)

    def issue_token(ref, grp, slot, t):
        for r in range(t * hk, (t + 1) * hk):
            e = ref[grp * rows + r]
            pltpu.make_async_copy(tab_ref.at[e], buf.at[slot, r // SUBLANES, :, r % SUBLANES],
                                  sem.at[slot]).start(priority=r % 2)

    def wait(slot):
        pltpu.make_async_copy(buf.at[(slot + 1) % n_groups], buf.at[slot], sem.at[slot]).wait()

    @pl.when(i == 0)
    def _():
        issue(idx_ref, 0, 0)
        issue(idx_ref, 1, 1)

    m = m_ref[0]
    s_rows = s_ref[...]
    h_rows = _modulate(s_rows, m, 3)
    gates = g_ref[0]
    for grp in range(n_groups):
        slot = grp
        ahead = grp + PEER_LOOKAHEAD
        ahead_ref, ahead_grp = (idx_ref, ahead) if ahead < n_groups else (nxt_ref, ahead - n_groups)
        wait(slot)

        def expert_scores(t):
            tok = grp * gt + t
            h_t = [jnp.broadcast_to(h_rows[tok:tok + 1, j * LANES:(j + 1) * LANES], (SUBLANES, LANES))
                   for j in range(nlt)]
            scs = []
            for q in range(pgs):
                base = t * pgs + q
                acc = buf[slot, base, 0] * h_t[0]
                for j in range(1, nlt):
                    acc = acc + buf[slot, base, j] * h_t[j]
                scs.append(jnp.sum(acc, axis=1, keepdims=True))
            return jnp.concatenate(scs, axis=0)

        def expert_mix(t, sc):
            tok = grp * gt + t
            coef = gates[:, tok:tok + 1] * _gelu_tanh(sc)
            o_acc = [None] * nlt
            for q in range(pgs):
                base = t * pgs + q
                c_q = jnp.broadcast_to(coef[q * SUBLANES:(q + 1) * SUBLANES, :], (SUBLANES, LANES))
                for j in range(nlt):
                    term = c_q * buf[slot, base, nlt + j]
                    o_acc[j] = term if o_acc[j] is None else o_acc[j] + term
            for j in range(nlt):
                obuf[tok:tok + 1, j * LANES:(j + 1) * LANES] = jnp.sum(o_acc[j], axis=0, keepdims=True)

        sc_prev = expert_scores(0)
        for t in range(gt):
            sc_next = expert_scores(t + 1) if t + 1 < gt else None
            issue_token(ahead_ref, ahead_grp, ahead % n_groups, t)
            expert_mix(t, sc_prev)
            sc_prev = sc_next
    z = DEEPNORM_ALPHA * s_rows + m[5:6] * obuf[...]
    out_ref[...] = _layer_norm_rows(z, lg_ref[...], lb_ref[...])

    @pl.when(i == n_steps - 1)
    def _():
        for slot in range(PEER_LOOKAHEAD):
            wait(slot)


def _peer_gather(S, M, gates_t, eidx, table, ln_g, ln_b, ctx_len, skip_ctx):
    B, P, D = S.shape
    tb = PEER_TOK_BLOCK
    n_slots = tb // PEER_TOK_GROUP
    assert PEER_LOOKAHEAD < n_slots
    hk = PEER_HEADS * PEER_TOPK
    j0 = ctx_len // tb if skip_ctx else 0
    per_b = P // tb - j0
    n_steps = B * per_b
    nctx = ctx_len // tb
    S2 = S.reshape(B * P, D)
    M2 = M.reshape(B * 2, SUBLANES, D)
    row_blk = lambda i: (i // per_b) * (P // tb) + i % per_b + j0
    kern = _peer_gather_kernel
    out = pl.pallas_call(
        kern,
        grid=(n_steps,),
        in_specs=[pl.BlockSpec((tb * hk,), lambda i: (i,), memory_space=pltpu.SMEM),
                  pl.BlockSpec((tb * hk,), lambda i: (jnp.minimum(i + 1, n_steps - 1),), memory_space=pltpu.SMEM),
                  pl.BlockSpec((1, hk, tb), lambda i: (i, 0, 0)),
                  pl.BlockSpec((tb, D), lambda i: (row_blk(i), 0)),
                  pl.BlockSpec((1, SUBLANES, D),
                               lambda i: ((i // per_b) * 2 + jnp.minimum((i % per_b + j0) // nctx, 1), 0, 0)),
                  pl.BlockSpec((1, D), lambda i: (0, 0)),
                  pl.BlockSpec((1, D), lambda i: (0, 0)),
                  pl.BlockSpec(memory_space=pl.ANY)],
        out_specs=pl.BlockSpec((tb, D), lambda i: (row_blk(i), 0)),
        out_shape=jax.ShapeDtypeStruct((B * P, D), F32),
        scratch_shapes=[pltpu.VMEM((n_slots, PEER_TOK_GROUP * hk // SUBLANES, 2 * D // LANES, SUBLANES, LANES), F32),
                        pltpu.SemaphoreType.DMA((n_slots,)),
                        pltpu.VMEM((tb, D), F32)],
        input_output_aliases={3: 0},
        compiler_params=_cparams(1),
    )(eidx, eidx, gates_t, S2, M2, ln_g.reshape(1, D), ln_b.reshape(1, D), table)
    return out.reshape(B, P, D)


def _peer_ffn_resid_ln(S, M, wq, keys, u_tab, v_tab, ln_g, ln_b, ctx_len, skip_ctx):
    B, P, D = S.shape
    wq_t = wq.T.astype(BF16)
    gates, eidx = _peer_topk(S, M, wq_t, keys.astype(BF16), ctx_len, skip_ctx)
    hk = gates.shape[1]
    tb = PEER_TOK_BLOCK
    gates_t = jnp.transpose(gates.reshape(B, hk, -1, tb), (0, 2, 1, 3)).reshape(-1, hk, tb)
    eidx_flat = jnp.transpose(eidx, (0, 2, 1)).reshape(-1)
    table = jnp.concatenate([u_tab, v_tab], axis=1).reshape(-1, 2 * D // LANES, LANES)
    return _peer_gather(S, M, gates_t, eidx_flat, table, ln_g, ln_b, ctx_len, skip_ctx)


def _deinterleave_heads(w):
    d_in, n = w.shape
    return w.reshape(d_in, n // HEAD_DIM, HEAD_DIM // 2, 2).transpose(0, 1, 3, 2).reshape(d_in, n)


def _rope_tables(ctx_len, n_lat):
    rows = n_lat // GRID_W
    row = jnp.repeat(jnp.arange(rows, dtype=F32), GRID_W)
    col = jnp.tile(jnp.arange(GRID_W, dtype=F32), rows)
    n_freq = HEAD_DIM // 4
    inv = ROPE_BASE ** (-jnp.arange(n_freq, dtype=F32) / n_freq)
    ang = jnp.concatenate([row[:, None] * inv, col[:, None] * inv], -1)
    cos, sin = jnp.cos(ang), jnp.sin(ang)
    reps = LANES // HEAD_DIM
    cos_t = jnp.tile(jnp.concatenate([cos, cos], -1), (1, reps))
    sin_t = jnp.tile(jnp.concatenate([-sin, sin], -1), (1, reps))
    cos_t = jnp.concatenate([jnp.ones((ctx_len, LANES), F32), cos_t], 0)
    sin_t = jnp.concatenate([jnp.zeros((ctx_len, LANES), F32), sin_t], 0)
    return cos_t, sin_t


def _mixer_layer(i, S, M, cos_t, sin_t, L, last, da_wqkv, da_wo, da_lambda, da_subln, wa_wqkv, wa_wo, wa_sink,
                 ssm_lam_re, ssm_lam_im, ssm_log_step, ssm_b_re, ssm_b_im, ssm_c_re, ssm_c_im, ssm_d, ssm_w_glu,
                 ln_g, ln_b):
    D = S.shape[2]
    kind, j = i % N_MIXERS, i // N_MIXERS
    if kind == 0:
        lam_init = 0.8 - 0.6 * math.exp(-0.3 * i)
        w = da_wqkv[j]
        n_qk = 2 * DA_HEADS * 2 * HEAD_DIM
        w = jnp.concatenate([_deinterleave_heads(w[:, :n_qk]), w[:, n_qk:]], axis=1).astype(BF16)
        qkv = _project(S, M, w, cos_t, sin_t, n_qk, L)
        O = _diff_attention(qkv, da_lambda[j], da_subln[j], lam_init, L, not last)
        return _out_proj_resid_ln(O, da_wo[j].astype(BF16), S, M, ln_g[i, 0], ln_b[i, 0], L, last)
    if kind == 1:
        w = wa_wqkv[j]
        nq = WA_Q_HEADS * HEAD_DIM
        nkv = WA_KV_HEADS * HEAD_DIM
        rep = lambda m: jnp.tile(m.reshape(D, WA_KV_HEADS, 1, HEAD_DIM), (1, 1, WA_REP, 1)).reshape(D, nq)
        w = jnp.concatenate([_deinterleave_heads(w[:, :nq]),
                             rep(_deinterleave_heads(w[:, nq:nq + nkv])),
                             rep(w[:, nq + nkv:])], axis=1).astype(BF16)
        qkv = _project(S, M, w, cos_t, sin_t, 2 * nq, L)
        O = _window_attention(qkv, wa_sink[j], L, not last)
        return _out_proj_resid_ln(O, wa_wo[j].astype(BF16), S, M, ln_g[i, 0], ln_b[i, 0], L, last)
    bmat, cmat, lam_r, lam_i = _s5_params(ssm_lam_re[j], ssm_lam_im[j], ssm_log_step[j], ssm_b_re[j],
                                          ssm_b_im[j], ssm_c_re[j], ssm_c_im[j])
    Mt = jnp.transpose(M[:, :, 0:2, :], (1, 2, 0, 3))
    Y_tm = _s5_scan(jnp.transpose(S, (1, 0, 2)), Mt, bmat, cmat, lam_r, lam_i, L)
    Y = jnp.transpose(Y_tm, (0, 2, 1, 3))
    return _glu_resid_ln(S, Y, ssm_d[j], ssm_w_glu[j].astype(BF16), M, ln_g[i, 0], ln_b[i, 0], L)


def kernel(x, c, ctx, c_ctx, mod_w, mod_b, ln_g, ln_b, peer_wq, peer_keys, peer_u, peer_v, da_wqkv, da_wo, da_lambda, da_subln, wa_wqkv, wa_wo, wa_sink, ssm_lam_re, ssm_lam_im, ssm_log_step, ssm_b_re, ssm_b_im, ssm_c_re, ssm_c_im, ssm_d, ssm_w_glu):
    B, T, D = x.shape
    L = ctx.shape[1]
    depth = mod_w.shape[0]
    cos_t, sin_t = _rope_tables(L, T)
    M_all = _mod_vectors(c, c_ctx, mod_w, mod_b)
    S = jnp.concatenate([ctx, x], axis=1)
    for i in range(depth):
        last = i == depth - 1
        S = _mixer_layer(i, S, M_all[i], cos_t, sin_t, L, last, da_wqkv, da_wo, da_lambda, da_subln, wa_wqkv, wa_wo,
                         wa_sink, ssm_lam_re, ssm_lam_im, ssm_log_step, ssm_b_re, ssm_b_im, ssm_c_re, ssm_c_im,
                         ssm_d, ssm_w_glu, ln_g, ln_b)
        S = _peer_ffn_resid_ln(S, M_all[i], peer_wq[i], peer_keys[i], peer_u[i], peer_v[i], ln_g[i, 1], ln_b[i, 1],
                               L, last)
    return S[:, L:, :]
```

```python
import functools
import math

import jax
import jax.numpy as jnp
from jax import lax
from jax.experimental import pallas as pl
from jax.experimental.pallas import tpu as pltpu

F32 = jnp.float32
BF16 = jnp.bfloat16

DEPTH = 4
N_MIXERS = 3
GRID_W = 64
HEAD_DIM = 64
ROPE_BASE = 10000.0
DA_HEADS = 8
WA_Q_HEADS = 16
WA_KV_HEADS = 4
WA_REP = WA_Q_HEADS // WA_KV_HEADS
WINDOW = 128
SSM_GROUP = 16
SSM_STATE = 64
PEER_HEADS = 8
PEER_NKEYS = 128
PEER_QDIM = 256
PEER_TOPK = 16
LN_EPS = 1e-5
DEEPNORM_ALPHA = (2 * DEPTH) ** 0.25
ATT_SCALE = HEAD_DIM ** -0.5

LANES = 128
SUBLANES = 8
ROW_TILE = 256
WIN_TILE = 128
SCAN_CHUNK = 128
SSM_SLAB_GROUPS = 8
PEER_TOK_BLOCK = 32
PEER_TOK_GROUP = 8
PEER_LOOKAHEAD = 2
VMEM_LIMIT = 48 * 1024 * 1024
NEG_INF = float("-inf")


def _cparams(n_axes):
    return pltpu.CompilerParams(dimension_semantics=("arbitrary",) * n_axes, vmem_limit_bytes=VMEM_LIMIT)


def _gelu_tanh(x):
    return 0.5 * x * (1.0 + jnp.tanh(math.sqrt(2.0 / math.pi) * (x + 0.044715 * (x * x * x))))


def _sigmoid(x):
    return 1.0 / (1.0 + jnp.exp(-x))


def _layer_norm_rows(z, g, b):
    mu = jnp.mean(z, axis=-1, keepdims=True)
    zc = z - mu
    var = jnp.mean(zc * zc, axis=-1, keepdims=True)
    return zc * lax.rsqrt(var + LN_EPS) * g + b


def _modulate(x, m, shift_idx):
    return x * (1.0 + m[shift_idx + 1:shift_idx + 2]) + m[shift_idx:shift_idx + 1]


def _mod_kernel(a_ref, w_ref, b_ref, o_ref):
    a = a_ref[...]
    a = a * _sigmoid(a)
    o_ref[0] = jnp.dot(a, w_ref[0], preferred_element_type=F32, precision=lax.Precision.HIGHEST) + b_ref[0]


def _mod_vectors(c, c_ctx, mod_w, mod_b):
    B, D = c.shape
    depth, _, n6 = mod_w.shape
    rows = -(-(B + 1) // SUBLANES) * SUBLANES
    a = jnp.zeros((rows, D), F32).at[:B].set(c).at[B].set(c_ctx)
    tn = n6 // 4
    out = pl.pallas_call(
        _mod_kernel,
        grid=(depth, n6 // tn),
        in_specs=[pl.BlockSpec((rows, D), lambda i, j: (0, 0)),
                  pl.BlockSpec((1, D, tn), lambda i, j: (i, 0, j)),
                  pl.BlockSpec((1, 1, tn), lambda i, j: (i, 0, j))],
        out_specs=pl.BlockSpec((1, rows, tn), lambda i, j: (i, 0, j)),
        out_shape=jax.ShapeDtypeStruct((depth, rows, n6), F32),
        compiler_params=_cparams(2),
    )(a, mod_w, mod_b.reshape(depth, 1, n6))
    lat = out[:, :B].reshape(depth, B, 6, D)
    ctx = jnp.broadcast_to(out[:, B].reshape(depth, 1, 6, D), (depth, B, 6, D))
    m = jnp.stack([ctx, lat], axis=2)
    return jnp.pad(m, ((0, 0), (0, 0), (0, 0), (0, 2), (0, 0)))


def _mod_spec(D, nctx_blocks, j0):
    return pl.BlockSpec((1, 1, SUBLANES, D), lambda b, j: (b, jnp.minimum((j + j0) // nctx_blocks, 1), 0, 0))


def _proj_kernel(s_ref, m_ref, w_ref, cos_ref, sin_ref, o_ref, *, n_rope, tn):
    xb = _modulate(s_ref[0], m_ref[0, 0], 0).astype(BF16)
    tm = xb.shape[0]
    n_out = w_ref.shape[1]
    cos_t = cos_ref[...]
    sin_t = sin_ref[...]
    lane = lax.broadcasted_iota(jnp.int32, (tm, LANES), 1)
    first_half = (lane % HEAD_DIM) < (HEAD_DIM // 2)
    for c0 in range(0, n_out, tn):
        y = jnp.dot(xb, w_ref[:, c0:c0 + tn], preferred_element_type=F32)
        if c0 < n_rope:
            pieces = []
            for l0 in range(0, tn, LANES):
                yc = y[:, l0:l0 + LANES]
                partner = jnp.where(first_half,
                                    pltpu.roll(yc, LANES - HEAD_DIM // 2, 1),
                                    pltpu.roll(yc, HEAD_DIM // 2, 1))
                pieces.append(yc * cos_t + partner * sin_t)
            y = jnp.concatenate(pieces, axis=1)
        o_ref[0, :, c0:c0 + tn] = y.astype(o_ref.dtype)


def _project(S, M, w, cos_t, sin_t, n_rope, ctx_len):
    B, P, D = S.shape
    n_out = w.shape[1]
    tm = ROW_TILE
    kern = functools.partial(_proj_kernel, n_rope=n_rope, tn=512)
    return pl.pallas_call(
        kern,
        grid=(B, P // tm),
        in_specs=[pl.BlockSpec((1, tm, D), lambda b, j: (b, j, 0)),
                  _mod_spec(D, ctx_len // tm, 0),
                  pl.BlockSpec((D, n_out), lambda b, j: (0, 0)),
                  pl.BlockSpec((tm, LANES), lambda b, j: (j, 0)),
                  pl.BlockSpec((tm, LANES), lambda b, j: (j, 0))],
        out_specs=pl.BlockSpec((1, tm, n_out), lambda b, j: (b, j, 0)),
        out_shape=jax.ShapeDtypeStruct((B, P, n_out), BF16),
        compiler_params=_cparams(2),
    )(S, M, w, cos_t, sin_t)


def _dattn_kernel(lam_ref, g_ref, q_ref, k_ref, v_ref, o_ref, *, lam_init, ctx_len, ctx_queries):
    lp = lam_ref[...]
    lam = (jnp.exp(jnp.sum(lp[0:1] * lp[1:2], axis=-1, keepdims=True))
           - jnp.exp(jnp.sum(lp[2:3] * lp[3:4], axis=-1, keepdims=True)) + lam_init)
    q = q_ref[0]
    lane = lax.broadcasted_iota(jnp.int32, q.shape, 1)
    zero = jnp.zeros_like(q)
    q_maps = (jnp.where(lane < HEAD_DIM, q, zero), jnp.where(lane >= HEAD_DIM, q, zero))

    def attend(nk):
        k = k_ref[0, 0:nk, :]
        v = v_ref[0, 0:nk, :]
        probs = []
        for qm in q_maps:
            s = lax.dot_general(qm, k, (((1,), (1,)), ((), ())), preferred_element_type=F32) * ATT_SCALE
            p = jnp.exp(s - jnp.max(s, axis=-1, keepdims=True))
            probs.append((p, 1.0 / jnp.sum(p, axis=-1, keepdims=True)))
        a = probs[0][0] * probs[0][1] - probs[1][0] * (lam * probs[1][1])
        o = jnp.dot(a.astype(BF16), v, preferred_element_type=F32)
        o = o * lax.rsqrt(jnp.mean(o * o, axis=-1, keepdims=True) + LN_EPS) * g_ref[...] * (1.0 - lam_init)
        o_ref[0] = o.astype(o_ref.dtype)

    if ctx_queries:
        qi = pl.program_id(2)

        @pl.when(qi == 0)
        def _():
            attend(ctx_len)

        @pl.when(qi > 0)
        def _():
            attend(k_ref.shape[1])
    else:
        attend(k_ref.shape[1])


def _diff_attention(qkv, lam_p, subln_g, lam_init, ctx_len, ctx_queries):
    B, P, _ = qkv.shape
    tq = ROW_TILE
    assert ctx_len == tq
    j0 = 0 if ctx_queries else 1
    H = DA_HEADS
    kern = functools.partial(_dattn_kernel, lam_init=lam_init, ctx_len=ctx_len, ctx_queries=ctx_queries)
    return pl.pallas_call(
        kern,
        grid=(B, H, P // tq - j0),
        in_specs=[pl.BlockSpec((4, HEAD_DIM), lambda b, h, i: (0, 0)),
                  pl.BlockSpec((1, 2 * HEAD_DIM), lambda b, h, i: (0, 0)),
                  pl.BlockSpec((1, tq, LANES), lambda b, h, i: (b, i + j0, h)),
                  pl.BlockSpec((1, P, LANES), lambda b, h, i: (b, 0, H + h)),
                  pl.BlockSpec((1, P, LANES), lambda b, h, i: (b, 0, 2 * H + h))],
        out_specs=pl.BlockSpec((1, tq, LANES), lambda b, h, i: (b, i + j0, h)),
        out_shape=jax.ShapeDtypeStruct((B, P, H * LANES), BF16),
        compiler_params=_cparams(3),
    )(lam_p, subln_g.reshape(1, -1), qkv, qkv, qkv)


def _wattn_kernel(sink_ref, q_ref, kc_ref, k0_ref, k1_ref, k2_ref, vc_ref, v0_ref, v1_ref, v2_ref, o_ref,
                  *, n_ctx_blocks, n_lat_blocks, j0):
    qb = pl.program_id(1) + j0
    g = pl.program_id(2)
    tq = q_ref.shape[1]
    width = q_ref.shape[2]
    q = q_ref[0]
    lane = lax.broadcasted_iota(jnp.int32, q.shape, 1)
    zero = jnp.zeros_like(q)
    qm = jnp.concatenate(
        [jnp.where((lane >= r * HEAD_DIM) & (lane < (r + 1) * HEAD_DIM), q, zero) for r in range(WA_REP)], axis=0)
    sink_col = jnp.concatenate([jnp.full((tq, 1), sink_ref[g * WA_REP + r], F32) for r in range(WA_REP)], axis=0)
    dims = (((1,), (1,)), ((), ()))

    def finish(o_all, denom):
        o_all = o_all / denom
        lane_o = lax.broadcasted_iota(jnp.int32, (tq, width), 1)
        o = jnp.zeros((tq, width), F32)
        for r in range(WA_REP):
            sel = (lane_o >= r * HEAD_DIM) & (lane_o < (r + 1) * HEAD_DIM)
            o = o + jnp.where(sel, o_all[r * tq:(r + 1) * tq, :], 0.0)
        o_ref[0] = o.astype(o_ref.dtype)

    @pl.when(qb < n_ctx_blocks)
    def _():
        s_c = lax.dot_general(qm, kc_ref[0], dims, preferred_element_type=F32) * ATT_SCALE
        m = jnp.maximum(jnp.max(s_c, axis=-1, keepdims=True), sink_col)
        p_c = jnp.exp(s_c - m)
        denom = jnp.sum(p_c, axis=-1, keepdims=True) + jnp.exp(sink_col - m)
        finish(jnp.dot(p_c.astype(BF16), vc_ref[0], preferred_element_type=F32), denom)

    @pl.when(qb >= n_ctx_blocks)
    def _():
        lb = qb - n_ctx_blocks
        kw = jnp.concatenate([k0_ref[0], k1_ref[0], k2_ref[0]], axis=0)
        vw = jnp.concatenate([v0_ref[0], v1_ref[0], v2_ref[0]], axis=0)
        s_c = lax.dot_general(qm, kc_ref[0], dims, preferred_element_type=F32) * ATT_SCALE
        s_w = lax.dot_general(qm, kw, dims, preferred_element_type=F32) * ATT_SCALE
        iq = lax.broadcasted_iota(jnp.int32, s_w.shape, 0) % tq
        ik = lax.broadcasted_iota(jnp.int32, s_w.shape, 1)
        lo = jnp.where(lb == 0, tq, 0)
        hi = jnp.where(lb == n_lat_blocks - 1, 2 * tq, 3 * tq)
        band = (ik >= iq) & (ik <= iq + 2 * WINDOW) & (ik >= lo) & (ik < hi)
        s_w = jnp.where(band, s_w, NEG_INF)
        m = jnp.maximum(jnp.maximum(jnp.max(s_c, axis=-1, keepdims=True), jnp.max(s_w, axis=-1, keepdims=True)),
                        sink_col)
        p_c = jnp.exp(s_c - m)
        p_w = jnp.exp(s_w - m)
        denom = (jnp.sum(p_c, axis=-1, keepdims=True) + jnp.sum(p_w, axis=-1, keepdims=True)
                 + jnp.exp(sink_col - m))
        o_all = (jnp.dot(p_c.astype(BF16), vc_ref[0], preferred_element_type=F32)
                 + jnp.dot(p_w.astype(BF16), vw, preferred_element_type=F32))
        finish(o_all, denom)


def _window_attention(qkv, sink, ctx_len, ctx_queries):
    B, P, _ = qkv.shape
    tq = WIN_TILE
    assert tq == WINDOW
    nctx = ctx_len // tq
    nlat = P // tq - nctx
    j0 = 0 if ctx_queries else nctx
    G = WA_KV_HEADS
    width = WA_REP * HEAD_DIM

    def lat_spec(off, col0):
        return pl.BlockSpec((1, tq, width),
                            lambda b, j, g: (b, jnp.clip(j + j0 - nctx + off, 0, nlat - 1) + nctx, col0 + g))

    q_spec = pl.BlockSpec((1, tq, width), lambda b, j, g: (b, j + j0, g))
    kc_spec = pl.BlockSpec((1, ctx_len, width), lambda b, j, g: (b, 0, G + g))
    vc_spec = pl.BlockSpec((1, ctx_len, width), lambda b, j, g: (b, 0, 2 * G + g))
    kern = functools.partial(_wattn_kernel, n_ctx_blocks=nctx, n_lat_blocks=nlat, j0=j0)
    return pl.pallas_call(
        kern,
        grid=(B, P // tq - j0, G),
        in_specs=([pl.BlockSpec(memory_space=pltpu.SMEM), q_spec, kc_spec]
                  + [lat_spec(off, G) for off in (-1, 0, 1)] + [vc_spec]
                  + [lat_spec(off, 2 * G) for off in (-1, 0, 1)]),
        out_specs=pl.BlockSpec((1, tq, width), lambda b, j, g: (b, j + j0, g)),
        out_shape=jax.ShapeDtypeStruct((B, P, G * width), BF16),
        compiler_params=_cparams(3),
    )(sink, qkv, qkv, qkv, qkv, qkv, qkv, qkv, qkv, qkv)


def _resid_ln_kernel(o_ref, w_ref, s_ref, m_ref, g_ref, b_ref, out_ref):
    y = jnp.dot(o_ref[0], w_ref[...], preferred_element_type=F32)
    z = DEEPNORM_ALPHA * s_ref[0] + m_ref[0, 0][2:3] * y
    out_ref[0] = _layer_norm_rows(z, g_ref[...], b_ref[...])


def _out_proj_resid_ln(O, w, S, M, ln_g, ln_b, ctx_len, skip_ctx):
    B, P, D = S.shape
    kdim = O.shape[2]
    tm = ROW_TILE
    j0 = ctx_len // tm if skip_ctx else 0
    row = lambda b, j: (b, j + j0, 0)
    return pl.pallas_call(
        _resid_ln_kernel,
        grid=(B, P // tm - j0),
        in_specs=[pl.BlockSpec((1, tm, kdim), row),
                  pl.BlockSpec((kdim, D), lambda b, j: (0, 0)),
                  pl.BlockSpec((1, tm, D), row),
                  _mod_spec(D, ctx_len // tm, j0),
                  pl.BlockSpec((1, D), lambda b, j: (0, 0)),
                  pl.BlockSpec((1, D), lambda b, j: (0, 0))],
        out_specs=pl.BlockSpec((1, tm, D), row),
        out_shape=jax.ShapeDtypeStruct((B, P, D), F32),
        input_output_aliases={2: 0},
        compiler_params=_cparams(2),
    )(O, w, S, M, ln_g.reshape(1, D), ln_b.reshape(1, D))


def _s5_kernel(s_ref, mt_ref, bm_ref, cm_ref, lr_ref, li_ref, y_ref, xbuf, state):
    d = pl.program_id(0)
    c = pl.program_id(2)
    tc, nb, w_in = s_ref.shape
    half = lr_ref.shape[-1]

    @pl.when(c == 0)
    def _():
        state[...] = jnp.zeros_like(state)

    mt = mt_ref[0]
    u = s_ref[...] * (1.0 + mt[1][None]) + mt[0][None]
    u2 = u.reshape(tc * nb, w_in).astype(BF16)
    xbuf[...] = jnp.dot(u2, bm_ref[0, 0], preferred_element_type=F32)
    lr = jnp.broadcast_to(lr_ref[0, 0], (nb, half))
    li = jnp.broadcast_to(li_ref[0, 0], (nb, half))

    def step(t, carry):
        sr, si = carry
        tt = jnp.where(d == 0, t, tc - 1 - t)
        r0 = pl.multiple_of(tt * nb, nb)
        nr = lr * sr - li * si + xbuf[pl.ds(r0, nb), 0:half]
        ni = lr * si + li * sr + xbuf[pl.ds(r0, nb), half:2 * half]
        xbuf[pl.ds(r0, nb), 0:half] = nr
        xbuf[pl.ds(r0, nb), half:2 * half] = ni
        return nr, ni

    sr, si = lax.fori_loop(0, tc, step, (state[:, 0:half], state[:, half:2 * half]), unroll=4)
    state[:, 0:half] = sr
    state[:, half:2 * half] = si
    y = jnp.dot(xbuf[...].astype(BF16), cm_ref[0, 0], preferred_element_type=F32)
    y_ref[0] = y.reshape(tc, nb, w_in)


def _s5_scan(S_tm, Mt, bmat, cmat, lam_r, lam_i, ctx_len):
    P, B, D = S_tm.shape
    tc = SCAN_CHUNK
    w_in = SSM_SLAB_GROUPS * SSM_GROUP
    n_slab = D // w_in
    half = SSM_SLAB_GROUPS * SSM_STATE
    nch = P // tc
    nctx = ctx_len // tc
    nlat = nch - nctx

    def chunk(d, c):
        q = nch - 1 - c
        back = jnp.where(q < nlat, q + nctx, q - nlat)
        return jnp.where(d == 0, c, back)

    return pl.pallas_call(
        _s5_kernel,
        grid=(2, n_slab, nch),
        in_specs=[pl.BlockSpec((tc, B, w_in), lambda d, s, c: (chunk(d, c), 0, s)),
                  pl.BlockSpec((1, 2, B, w_in), lambda d, s, c: (jnp.minimum(chunk(d, c) // nctx, 1), 0, 0, s)),
                  pl.BlockSpec((1, 1, w_in, 2 * half), lambda d, s, c: (d, s, 0, 0)),
                  pl.BlockSpec((1, 1, 2 * half, w_in), lambda d, s, c: (d, s, 0, 0)),
                  pl.BlockSpec((1, 1, 1, half), lambda d, s, c: (d, s, 0, 0)),
                  pl.BlockSpec((1, 1, 1, half), lambda d, s, c: (d, s, 0, 0))],
        out_specs=pl.BlockSpec((1, tc, B, w_in), lambda d, s, c: (d, chunk(d, c), 0, s)),
        out_shape=jax.ShapeDtypeStruct((2, P, B, D), F32),
        scratch_shapes=[pltpu.VMEM((tc * B, 2 * half), F32), pltpu.VMEM((B, 2 * half), F32)],
        compiler_params=_cparams(3),
    )(S_tm, Mt, bmat, cmat, lam_r, lam_i)


def _s5_params(lam_re, lam_im, log_step, b_re, b_im, c_re, c_im):
    lam = lax.complex(lam_re.astype(F32), lam_im.astype(F32))
    step = jnp.exp(log_step.astype(F32))[..., None]
    lam_bar = jnp.exp(lam * step)
    b_bar = lax.complex(b_re.astype(F32), b_im.astype(F32)) * ((lam_bar - 1) / lam)[..., None]
    G, Pst, Hg = b_bar.shape[1:]
    ng = SSM_SLAB_GROUPS
    ns = G // ng
    eye = jnp.eye(ng, dtype=F32)

    def bdiag_in(x):
        x = x.reshape(2, ns, ng, Pst, Hg)
        return jnp.einsum('rsgph,gk->rsghkp', x, eye).reshape(2, ns, ng * Hg, ng * Pst)

    def bdiag_out(x):
        x = x.reshape(2, ns, ng, Hg, Pst)
        return jnp.einsum('rsghp,gk->rsgpkh', x, eye).reshape(2, ns, ng * Pst, ng * Hg)

    bmat = jnp.concatenate([bdiag_in(jnp.real(b_bar)), bdiag_in(jnp.imag(b_bar))], axis=-1).astype(BF16)
    cmat = jnp.concatenate([bdiag_out(c_re.astype(F32)), bdiag_out(-c_im.astype(F32))], axis=-2).astype(BF16)
    lam_r = jnp.real(lam_bar).reshape(2, ns, 1, ng * Pst)
    lam_i = jnp.imag(lam_bar).reshape(2, ns, 1, ng * Pst)
    return bmat, cmat, lam_r, lam_i


def _glu_ln_kernel(s_ref, y0_ref, y1_ref, d_ref, w_ref, m_ref, g_ref, b_ref, out_ref):
    m = m_ref[0, 0]
    s = s_ref[0]
    y = d_ref[...] * _modulate(s, m, 0) + y0_ref[0, 0] + y1_ref[0, 0]
    z = jnp.dot(_gelu_tanh(y).astype(BF16), w_ref[...], preferred_element_type=F32)
    D = s.shape[1]
    o = z[:, 0:D] * _sigmoid(z[:, D:2 * D])
    out_ref[0] = _layer_norm_rows(DEEPNORM_ALPHA * s + m[2:3] * o, g_ref[...], b_ref[...])


def _glu_resid_ln(S, Y, d_skip, w_glu, M, ln_g, ln_b, ctx_len):
    B, P, D = S.shape
    tm = ROW_TILE
    row = lambda b, j: (b, j, 0)
    return pl.pallas_call(
        _glu_ln_kernel,
        grid=(B, P // tm),
        in_specs=[pl.BlockSpec((1, tm, D), row),
                  pl.BlockSpec((1, 1, tm, D), lambda b, j: (0, b, j, 0)),
                  pl.BlockSpec((1, 1, tm, D), lambda b, j: (1, b, j, 0)),
                  pl.BlockSpec((1, D), lambda b, j: (0, 0)),
                  pl.BlockSpec((D, 2 * D), lambda b, j: (0, 0)),
                  _mod_spec(D, ctx_len // tm, 0),
                  pl.BlockSpec((1, D), lambda b, j: (0, 0)),
                  pl.BlockSpec((1, D), lambda b, j: (0, 0))],
        out_specs=pl.BlockSpec((1, tm, D), row),
        out_shape=jax.ShapeDtypeStruct((B, P, D), F32),
        compiler_params=_cparams(2),
    )(S, Y, Y, d_skip.reshape(1, D), w_glu, M, ln_g.reshape(1, D), ln_b.reshape(1, D))


def _peer_stair():
    blocks = []
    for a in range(PEER_TOPK):
        nb = PEER_TOPK // (a + 1)
        blocks.append((a, nb, PEER_TOPK if a == 0 else SUBLANES))
    return blocks


def _peer_topk_kernel(s_ref, m_ref, wqt_ref, keys_ref, g_out, e_out, qt):
    tn = s_ref.shape[1]
    nk = PEER_NKEYS
    k = PEER_TOPK
    xb = _modulate(s_ref[0], m_ref[0, 0], 3).astype(BF16)
    qt[...] = lax.dot_general(wqt_ref[...], xb, (((1,), (1,)), ((), ())), preferred_element_type=F32)
    stair = _peer_stair()

    def head(h, carry):
        krow = lax.broadcasted_iota(jnp.int32, (nk, tn), 0).astype(F32)
        rank = lax.broadcasted_iota(jnp.int32, (k, tn), 0)
        rank8 = lax.broadcasted_iota(jnp.int32, (SUBLANES, tn), 0)
        pos_const = jnp.concatenate(
            [(a * k + lax.broadcasted_iota(jnp.int32, (rows, tn), 0)).astype(F32) for a, _, rows in stair], axis=0)
        tops = []
        for j in range(2):
            qs = qt[pl.ds(pl.multiple_of(h * 2 * nk + j * nk, nk), nk), :].astype(BF16)
            s = jnp.dot(keys_ref[j, h], qs, preferred_element_type=F32)
            top = []
            for it in range(k):
                mx = jnp.max(s, axis=0, keepdims=True)
                ix = jnp.min(jnp.where(s == mx, krow, float(nk)), axis=0, keepdims=True)
                s = jnp.where(krow == ix, NEG_INF, s)
                top.append((mx, ix))
            tops.append(top)
        sv16 = jnp.zeros((k, tn), F32)
        si16 = jnp.zeros((k, tn), F32)
        sv8 = jnp.zeros((SUBLANES, tn), F32)
        si8 = jnp.zeros((SUBLANES, tn), F32)
        for it, (mx, ix) in enumerate(tops[1]):
            sv16 = jnp.where(rank == it, mx, sv16)
            si16 = jnp.where(rank == it, ix, si16)
            if it < SUBLANES:
                sv8 = jnp.where(rank8 == it, mx, sv8)
                si8 = jnp.where(rank8 == it, ix, si8)
        cs, ci = [], []
        for a, nb, rows in stair:
            mx0, ix0 = tops[0][a]
            cv = mx0 + (sv16 if rows == k else sv8)
            ce = ix0 * float(nk) + (si16 if rows == k else si8)
            if nb < rows:
                cv = jnp.where(rank8 < nb, cv, NEG_INF)
            cs.append(cv)
            ci.append(ce)
        cand = jnp.concatenate(cs, axis=0)
        cexp = jnp.concatenate(ci, axis=0)
        ts = jnp.zeros((k, tn), F32)
        te = jnp.zeros((k, tn), F32)
        best = None
        for it in range(k):
            mx = jnp.max(cand, axis=0, keepdims=True)
            sel = jnp.min(jnp.where(cand == mx, pos_const, float(k * k)), axis=0, keepdims=True)
            hit = pos_const == sel
            ex = jnp.max(jnp.where(hit, cexp, -1.0), axis=0, keepdims=True)
            cand = jnp.where(hit, NEG_INF, cand)
            ts = jnp.where(rank == it, mx, ts)
            te = jnp.where(rank == it, ex, te)
            best = mx if best is None else best
        p = jnp.exp(ts - best)
        gate = p / jnp.sum(p, axis=0, keepdims=True)
        r0 = pl.multiple_of(h * k, k)
        g_out[0, pl.ds(r0, k), :] = gate
        e_out[0, pl.ds(r0, k), :] = te.astype(jnp.int32)
        return carry

    lax.fori_loop(0, PEER_HEADS, head, 0)


def _peer_topk(S, M, wq_t, keys, ctx_len, skip_ctx):
    B, P, D = S.shape
    tn = ROW_TILE
    j0 = ctx_len // tn if skip_ctx else 0
    nblk = P // tn - j0
    hk = PEER_HEADS * PEER_TOPK
    out_spec = pl.BlockSpec((1, hk, tn), lambda b, j: (b, 0, j))
    return pl.pallas_call(
        _peer_topk_kernel,
        grid=(B, nblk),
        in_specs=[pl.BlockSpec((1, tn, D), lambda b, j: (b, j + j0, 0)),
                  _mod_spec(D, ctx_len // tn, j0),
                  pl.BlockSpec(wq_t.shape, lambda b, j: (0, 0)),
                  pl.BlockSpec(keys.shape, lambda b, j: (0, 0, 0, 0))],
        out_specs=[out_spec, out_spec],
        out_shape=[jax.ShapeDtypeStruct((B, hk, nblk * tn), F32),
                   jax.ShapeDtypeStruct((B, hk, nblk * tn), jnp.int32)],
        scratch_shapes=[pltpu.VMEM((wq_t.shape[0], tn), F32)],
        compiler_params=_cparams(2),
    )(S, M, wq_t, keys)


def _sublane_sums(vregs):
    sub = lax.broadcasted_iota(jnp.int32, (SUBLANES, LANES), 0)
    level = list(vregs)
    half = SUBLANES // 2
    while half >= 1:
        lower = (sub % (2 * half)) < half
        nxt = []
        for k in range(len(level) // 2):
            a, b = level[k], level[k + len(level) // 2]
            stay = jnp.where(lower, a, b)
            move = jnp.where(lower, b, a)
            if 2 * half == SUBLANES:
                moved = pltpu.roll(move, half, 0)
            else:
                moved = jnp.where(lower, pltpu.roll(move, SUBLANES - half, 0), pltpu.roll(move, half, 0))
            nxt.append(stay + moved)
        level = nxt
        half //= 2
    return level[0]


def _peer_gather_kernel(idx_ref, nxt_ref, g_ref, s_ref, m_ref, lg_ref, lb_ref, tab_ref, out_ref, buf, sem, obuf,
                        cbuf):
    i = pl.program_id(0)
    n_steps = pl.num_programs(0)
    tb = s_ref.shape[0]
    hk = g_ref.shape[1]
    gt = PEER_TOK_GROUP
    rows = gt * hk
    n_groups = tb // gt
    pgs = hk // SUBLANES

    def row_copy(e, slot, r, prio):
        pltpu.make_async_copy(tab_ref.at[e], buf.at[slot, r], sem.at[slot]).start(priority=prio)

    def issue(ref, grp, slot):
        per_it = 4 * SUBLANES

        def body(it, carry):
            for k in range(per_it):
                row_copy(ref[grp * rows + it * per_it + k], slot, it * per_it + k, k % 2)
            return carry

        lax.fori_loop(0, rows // per_it, body, 0)

    def issue_token(ref, grp, slot, t):
        for r in range(t * hk, (t + 1) * hk):
            row_copy(ref[grp * rows + r], slot, r, r % 2)

    def wait(slot):
        pltpu.make_async_copy(buf.at[(slot + 1) % n_groups], buf.at[slot], sem.at[slot]).wait()

    @pl.when(i == 0)
    def _():
        for g0 in range(PEER_LOOKAHEAD):
            issue(idx_ref, g0, g0)

    m = m_ref[0]
    s_rows = s_ref[...]
    h_rows = s_rows * (1.0 + m[4][None]) + m[3][None]
    gates = g_ref[0]
    for grp in range(n_groups):
        slot = grp
        ahead = grp + PEER_LOOKAHEAD
        ahead_ref, ahead_grp = (idx_ref, ahead) if ahead < n_groups else (nxt_ref, ahead - n_groups)
        wait(slot)

        def expert_scores(t):
            h_t = h_rows[grp * gt + t]
            scs = []
            for q in range(pgs):
                r0 = t * hk + q * SUBLANES
                part = _sublane_sums([buf[slot, r0 + k, 0] * h_t for k in range(SUBLANES)])
                scs.append(jnp.sum(part, axis=1, keepdims=True))
            return jnp.concatenate(scs, axis=0)

        def expert_mix(t, sc):
            tok = grp * gt + t
            cbuf[tok] = jnp.broadcast_to(gates[:, tok:tok + 1] * _gelu_tanh(sc), (hk, LANES))
            accs = [None] * 4
            for r in range(hk):
                term = jnp.broadcast_to(cbuf[tok, r:r + 1, :], (SUBLANES, LANES)) * buf[slot, t * hk + r, 1]
                accs[r % 4] = term if accs[r % 4] is None else accs[r % 4] + term
            obuf[tok] = (accs[0] + accs[1]) + (accs[2] + accs[3])

        sc_prev = expert_scores(0)
        for t in range(gt):
            sc_next = expert_scores(t + 1) if t + 1 < gt else None
            issue_token(ahead_ref, ahead_grp, ahead % n_groups, t)
            expert_mix(t, sc_prev)
            sc_prev = sc_next
    z = DEEPNORM_ALPHA * s_rows + m[5][None] * obuf[...]
    n_el = z.shape[1] * z.shape[2]
    mu = jnp.sum(jnp.sum(z, axis=2, keepdims=True), axis=1, keepdims=True) / n_el
    zc = z - mu
    var = jnp.sum(jnp.sum(zc * zc, axis=2, keepdims=True), axis=1, keepdims=True) / n_el
    out_ref[...] = zc * lax.rsqrt(var + LN_EPS) * lg_ref[...][None] + lb_ref[...][None]

    @pl.when(i == n_steps - 1)
    def _():
        for slot in range(PEER_LOOKAHEAD):
            wait(slot)


def _peer_gather(S, M, gates_t, eidx, table, ln_g, ln_b, ctx_len, skip_ctx):
    B, P, D = S.shape
    tb = PEER_TOK_BLOCK
    n_slots = tb // PEER_TOK_GROUP
    assert PEER_LOOKAHEAD < n_slots
    hk = PEER_HEADS * PEER_TOPK
    j0 = ctx_len // tb if skip_ctx else 0
    per_b = P // tb - j0
    n_steps = B * per_b
    nctx = ctx_len // tb
    sub = D // LANES
    assert sub == SUBLANES
    S3 = S.reshape(B * P, sub, LANES)
    M3 = M.reshape(B * 2, SUBLANES, sub, LANES)
    row_blk = lambda i: (i // per_b) * (P // tb) + i % per_b + j0
    out = pl.pallas_call(
        _peer_gather_kernel,
        grid=(n_steps,),
        in_specs=[pl.BlockSpec((tb * hk,), lambda i: (i,), memory_space=pltpu.SMEM),
                  pl.BlockSpec((tb * hk,), lambda i: (jnp.minimum(i + 1, n_steps - 1),), memory_space=pltpu.SMEM),
                  pl.BlockSpec((1, hk, tb), lambda i: (i, 0, 0)),
                  pl.BlockSpec((tb, sub, LANES), lambda i: (row_blk(i), 0, 0)),
                  pl.BlockSpec((1, SUBLANES, sub, LANES),
                               lambda i: ((i // per_b) * 2 + jnp.minimum((i % per_b + j0) // nctx, 1), 0, 0, 0)),
                  pl.BlockSpec((sub, LANES), lambda i: (0, 0)),
                  pl.BlockSpec((sub, LANES), lambda i: (0, 0)),
                  pl.BlockSpec(memory_space=pl.ANY)],
        out_specs=pl.BlockSpec((tb, sub, LANES), lambda i: (row_blk(i), 0, 0)),
        out_shape=jax.ShapeDtypeStruct((B * P, sub, LANES), F32),
        scratch_shapes=[pltpu.VMEM((n_slots, PEER_TOK_GROUP * hk, 2, sub, LANES), F32),
                        pltpu.SemaphoreType.DMA((n_slots,)),
                        pltpu.VMEM((tb, sub, LANES), F32),
                        pltpu.VMEM((tb, hk, LANES), F32)],
        input_output_aliases={3: 0},
        compiler_params=_cparams(1),
    )(eidx, eidx, gates_t, S3, M3, ln_g.reshape(sub, LANES), ln_b.reshape(sub, LANES), table)
    return out.reshape(B, P, D)


def _peer_ffn_resid_ln(S, M, wq, keys, u_tab, v_tab, ln_g, ln_b, ctx_len, skip_ctx):
    B, P, D = S.shape
    wq_t = wq.T.astype(BF16)
    gates, eidx = _peer_topk(S, M, wq_t, keys.astype(BF16), ctx_len, skip_ctx)
    hk = gates.shape[1]
    tb = PEER_TOK_BLOCK
    gates_t = jnp.transpose(gates.reshape(B, hk, -1, tb), (0, 2, 1, 3)).reshape(-1, hk, tb)
    eidx_flat = jnp.transpose(eidx, (0, 2, 1)).reshape(-1)
    table = jnp.stack([u_tab, v_tab], axis=1).reshape(-1, 2, D // LANES, LANES)
    return _peer_gather(S, M, gates_t, eidx_flat, table, ln_g, ln_b, ctx_len, skip_ctx)


def _deinterleave_heads(w):
    d_in, n = w.shape
    return w.reshape(d_in, n // HEAD_DIM, HEAD_DIM // 2, 2).transpose(0, 1, 3, 2).reshape(d_in, n)


def _rope_tables(ctx_len, n_lat):
    rows = n_lat // GRID_W
    row = jnp.repeat(jnp.arange(rows, dtype=F32), GRID_W)
    col = jnp.tile(jnp.arange(GRID_W, dtype=F32), rows)
    n_freq = HEAD_DIM // 4
    inv = ROPE_BASE ** (-jnp.arange(n_freq, dtype=F32) / n_freq)
    ang = jnp.concatenate([row[:, None] * inv, col[:, None] * inv], -1)
    cos, sin = jnp.cos(ang), jnp.sin(ang)
    reps = LANES // HEAD_DIM
    cos_t = jnp.tile(jnp.concatenate([cos, cos], -1), (1, reps))
    sin_t = jnp.tile(jnp.concatenate([-sin, sin], -1), (1, reps))
    cos_t = jnp.concatenate([jnp.ones((ctx_len, LANES), F32), cos_t], 0)
    sin_t = jnp.concatenate([jnp.zeros((ctx_len, LANES), F32), sin_t], 0)
    return cos_t, sin_t


def _mixer_layer(i, S, M, cos_t, sin_t, L, last, da_wqkv, da_wo, da_lambda, da_subln, wa_wqkv, wa_wo, wa_sink,
                 ssm_lam_re, ssm_lam_im, ssm_log_step, ssm_b_re, ssm_b_im, ssm_c_re, ssm_c_im, ssm_d, ssm_w_glu,
                 ln_g, ln_b):
    D = S.shape[2]
    kind, j = i % N_MIXERS, i // N_MIXERS
    if kind == 0:
        lam_init = 0.8 - 0.6 * math.exp(-0.3 * i)
        w = da_wqkv[j]
        n_qk = 2 * DA_HEADS * 2 * HEAD_DIM
        w = jnp.concatenate([_deinterleave_heads(w[:, :n_qk]), w[:, n_qk:]], axis=1).astype(BF16)
        qkv = _project(S, M, w, cos_t, sin_t, n_qk, L)
        O = _diff_attention(qkv, da_lambda[j], da_subln[j], lam_init, L, not last)
        return _out_proj_resid_ln(O, da_wo[j].astype(BF16), S, M, ln_g[i, 0], ln_b[i, 0], L, last)
    if kind == 1:
        w = wa_wqkv[j]
        nq = WA_Q_HEADS * HEAD_DIM
        nkv = WA_KV_HEADS * HEAD_DIM
        rep = lambda m: jnp.tile(m.reshape(D, WA_KV_HEADS, 1, HEAD_DIM), (1, 1, WA_REP, 1)).reshape(D, nq)
        w = jnp.concatenate([_deinterleave_heads(w[:, :nq]),
                             rep(_deinterleave_heads(w[:, nq:nq + nkv])),
                             rep(w[:, nq + nkv:])], axis=1).astype(BF16)
        qkv = _project(S, M, w, cos_t, sin_t, 2 * nq, L)
        O = _window_attention(qkv, wa_sink[j], L, not last)
        return _out_proj_resid_ln(O, wa_wo[j].astype(BF16), S, M, ln_g[i, 0], ln_b[i, 0], L, last)
    bmat, cmat, lam_r, lam_i = _s5_params(ssm_lam_re[j], ssm_lam_im[j], ssm_log_step[j], ssm_b_re[j],
                                          ssm_b_im[j], ssm_c_re[j], ssm_c_im[j])
    Mt = jnp.transpose(M[:, :, 0:2, :], (1, 2, 0, 3))
    Y_tm = _s5_scan(jnp.transpose(S, (1, 0, 2)), Mt, bmat, cmat, lam_r, lam_i, L)
    Y = jnp.transpose(Y_tm, (0, 2, 1, 3))
    return _glu_resid_ln(S, Y, ssm_d[j], ssm_w_glu[j].astype(BF16), M, ln_g[i, 0], ln_b[i, 0], L)


def kernel(x, c, ctx, c_ctx, mod_w, mod_b, ln_g, ln_b, peer_wq, peer_keys, peer_u, peer_v, da_wqkv, da_wo, da_lambda, da_subln, wa_wqkv, wa_wo, wa_sink, ssm_lam_re, ssm_lam_im, ssm_log_step, ssm_b_re, ssm_b_im, ssm_c_re, ssm_c_im, ssm_d, ssm_w_glu):
    B, T, D = x.shape
    L = ctx.shape[1]
    depth = mod_w.shape[0]
    cos_t, sin_t = _rope_tables(L, T)
    M_all = _mod_vectors(c, c_ctx, mod_w, mod_b)
    S = jnp.concatenate([ctx, x], axis=1)
    for i in range(depth):
        last = i == depth - 1
        S = _mixer_layer(i, S, M_all[i], cos_t, sin_t, L, last, da_wqkv, da_wo, da_lambda, da_subln, wa_wqkv, wa_wo,
                         wa_sink, ssm_lam_re, ssm_lam_im, ssm_log_step, ssm_b_re, ssm_b_im, ssm_c_re, ssm_c_im,
                         ssm_d, ssm_w_glu, ln_g, ln_b)
        S = _peer_ffn_resid_ln(S, M_all[i], peer_wq[i], peer_keys[i], peer_u[i], peer_v[i], ln_g[i, 1], ln_b[i, 1],
                               L, last)
    return S[:, L:, :]
```

```python
import functools
import math

import jax
import jax.numpy as jnp
from jax import lax
from jax.experimental import pallas as pl
from jax.experimental.pallas import tpu as pltpu

F32 = jnp.float32
BF16 = jnp.bfloat16

DEPTH = 4
N_MIXERS = 3
GRID_W = 64
HEAD_DIM = 64
ROPE_BASE = 10000.0
DA_HEADS = 8
WA_Q_HEADS = 16
WA_KV_HEADS = 4
WA_REP = WA_Q_HEADS // WA_KV_HEADS
WINDOW = 128
SSM_GROUP = 16
SSM_STATE = 64
PEER_HEADS = 8
PEER_NKEYS = 128
PEER_QDIM = 256
PEER_TOPK = 16
LN_EPS = 1e-5
DEEPNORM_ALPHA = (2 * DEPTH) ** 0.25
ATT_SCALE = HEAD_DIM ** -0.5

LANES = 128
SUBLANES = 8
ROW_TILE = 256
WIN_TILE = 128
SCAN_CHUNK = 128
SSM_SLAB_GROUPS = 8
PEER_TOK_BLOCK = 32
PEER_TOK_GROUP = 8
PEER_LOOKAHEAD = 2
VMEM_LIMIT = 48 * 1024 * 1024
NEG_INF = float("-inf")


def _cparams(n_axes):
    return pltpu.CompilerParams(dimension_semantics=("arbitrary",) * n_axes, vmem_limit_bytes=VMEM_LIMIT)


def _gelu_tanh(x):
    return 0.5 * x * (1.0 + jnp.tanh(math.sqrt(2.0 / math.pi) * (x + 0.044715 * (x * x * x))))


def _sigmoid(x):
    return 1.0 / (1.0 + jnp.exp(-x))


def _layer_norm_rows(z, g, b):
    mu = jnp.mean(z, axis=-1, keepdims=True)
    zc = z - mu
    var = jnp.mean(zc * zc, axis=-1, keepdims=True)
    return zc * lax.rsqrt(var + LN_EPS) * g + b


def _modulate(x, m, shift_idx):
    return x * (1.0 + m[shift_idx + 1:shift_idx + 2]) + m[shift_idx:shift_idx + 1]


def _mod_kernel(a_ref, w_ref, b_ref, o_ref):
    a = a_ref[...]
    a = a * _sigmoid(a)
    o_ref[0] = jnp.dot(a, w_ref[0], preferred_element_type=F32, precision=lax.Precision.HIGHEST) + b_ref[0]


def _mod_vectors(c, c_ctx, mod_w, mod_b):
    B, D = c.shape
    depth, _, n6 = mod_w.shape
    rows = -(-(B + 1) // SUBLANES) * SUBLANES
    a = jnp.zeros((rows, D), F32).at[:B].set(c).at[B].set(c_ctx)
    tn = n6 // 4
    out = pl.pallas_call(
        _mod_kernel,
        grid=(depth, n6 // tn),
        in_specs=[pl.BlockSpec((rows, D), lambda i, j: (0, 0)),
                  pl.BlockSpec((1, D, tn), lambda i, j: (i, 0, j)),
                  pl.BlockSpec((1, 1, tn), lambda i, j: (i, 0, j))],
        out_specs=pl.BlockSpec((1, rows, tn), lambda i, j: (i, 0, j)),
        out_shape=jax.ShapeDtypeStruct((depth, rows, n6), F32),
        compiler_params=_cparams(2),
    )(a, mod_w, mod_b.reshape(depth, 1, n6))
    lat = out[:, :B].reshape(depth, B, 6, D)
    ctx = jnp.broadcast_to(out[:, B].reshape(depth, 1, 6, D), (depth, B, 6, D))
    m = jnp.stack([ctx, lat], axis=2)
    return jnp.pad(m, ((0, 0), (0, 0), (0, 0), (0, 2), (0, 0)))


def _mod_spec(D, nctx_blocks, j0):
    return pl.BlockSpec((1, 1, SUBLANES, D), lambda b, j: (b, jnp.minimum((j + j0) // nctx_blocks, 1), 0, 0))


def _proj_kernel(s_ref, m_ref, w_ref, cos_ref, sin_ref, o_ref, *, n_rope, tn):
    xb = _modulate(s_ref[0], m_ref[0, 0], 0).astype(BF16)
    tm = xb.shape[0]
    n_out = w_ref.shape[1]
    cos_t = cos_ref[...]
    sin_t = sin_ref[...]
    lane = lax.broadcasted_iota(jnp.int32, (tm, LANES), 1)
    first_half = (lane % HEAD_DIM) < (HEAD_DIM // 2)
    for c0 in range(0, n_out, tn):
        y = jnp.dot(xb, w_ref[:, c0:c0 + tn], preferred_element_type=F32)
        if c0 < n_rope:
            pieces = []
            for l0 in range(0, tn, LANES):
                yc = y[:, l0:l0 + LANES]
                partner = jnp.where(first_half,
                                    pltpu.roll(yc, LANES - HEAD_DIM // 2, 1),
                                    pltpu.roll(yc, HEAD_DIM // 2, 1))
                pieces.append(yc * cos_t + partner * sin_t)
            y = jnp.concatenate(pieces, axis=1)
        o_ref[0, :, c0:c0 + tn] = y.astype(o_ref.dtype)


def _project(S, M, w, cos_t, sin_t, n_rope, ctx_len):
    B, P, D = S.shape
    n_out = w.shape[1]
    tm = ROW_TILE
    kern = functools.partial(_proj_kernel, n_rope=n_rope, tn=512)
    return pl.pallas_call(
        kern,
        grid=(B, P // tm),
        in_specs=[pl.BlockSpec((1, tm, D), lambda b, j: (b, j, 0)),
                  _mod_spec(D, ctx_len // tm, 0),
                  pl.BlockSpec((D, n_out), lambda b, j: (0, 0)),
                  pl.BlockSpec((tm, LANES), lambda b, j: (j, 0)),
                  pl.BlockSpec((tm, LANES), lambda b, j: (j, 0))],
        out_specs=pl.BlockSpec((1, tm, n_out), lambda b, j: (b, j, 0)),
        out_shape=jax.ShapeDtypeStruct((B, P, n_out), BF16),
        compiler_params=_cparams(2),
    )(S, M, w, cos_t, sin_t)


def _dattn_kernel(lam_ref, g_ref, q_ref, k_ref, v_ref, o_ref, *, lam_init, ctx_len, ctx_queries):
    lp = lam_ref[...]
    lam = (jnp.exp(jnp.sum(lp[0:1] * lp[1:2], axis=-1, keepdims=True))
           - jnp.exp(jnp.sum(lp[2:3] * lp[3:4], axis=-1, keepdims=True)) + lam_init)
    q = q_ref[0]
    lane = lax.broadcasted_iota(jnp.int32, q.shape, 1)
    zero = jnp.zeros_like(q)
    q_maps = (jnp.where(lane < HEAD_DIM, q, zero), jnp.where(lane >= HEAD_DIM, q, zero))

    def attend(nk):
        k = k_ref[0, 0:nk, :]
        v = v_ref[0, 0:nk, :]
        probs = []
        for qm in q_maps:
            s = lax.dot_general(qm, k, (((1,), (1,)), ((), ())), preferred_element_type=F32) * ATT_SCALE
            p = jnp.exp(s - jnp.max(s, axis=-1, keepdims=True))
            probs.append((p, 1.0 / jnp.sum(p, axis=-1, keepdims=True)))
        a = probs[0][0] * probs[0][1] - probs[1][0] * (lam * probs[1][1])
        o = jnp.dot(a.astype(BF16), v, preferred_element_type=F32)
        o = o * lax.rsqrt(jnp.mean(o * o, axis=-1, keepdims=True) + LN_EPS) * g_ref[...] * (1.0 - lam_init)
        o_ref[0] = o.astype(o_ref.dtype)

    if ctx_queries:
        qi = pl.program_id(2)

        @pl.when(qi == 0)
        def _():
            attend(ctx_len)

        @pl.when(qi > 0)
        def _():
            attend(k_ref.shape[1])
    else:
        attend(k_ref.shape[1])


def _diff_attention(qkv, lam_p, subln_g, lam_init, ctx_len, ctx_queries):
    B, P, _ = qkv.shape
    tq = ROW_TILE
    assert ctx_len == tq
    j0 = 0 if ctx_queries else 1
    H = DA_HEADS
    kern = functools.partial(_dattn_kernel, lam_init=lam_init, ctx_len=ctx_len, ctx_queries=ctx_queries)
    return pl.pallas_call(
        kern,
        grid=(B, H, P // tq - j0),
        in_specs=[pl.BlockSpec((4, HEAD_DIM), lambda b, h, i: (0, 0)),
                  pl.BlockSpec((1, 2 * HEAD_DIM), lambda b, h, i: (0, 0)),
                  pl.BlockSpec((1, tq, LANES), lambda b, h, i: (b, i + j0, h)),
                  pl.BlockSpec((1, P, LANES), lambda b, h, i: (b, 0, H + h)),
                  pl.BlockSpec((1, P, LANES), lambda b, h, i: (b, 0, 2 * H + h))],
        out_specs=pl.BlockSpec((1, tq, LANES), lambda b, h, i: (b, i + j0, h)),
        out_shape=jax.ShapeDtypeStruct((B, P, H * LANES), BF16),
        compiler_params=_cparams(3),
    )(lam_p, subln_g.reshape(1, -1), qkv, qkv, qkv)


def _wattn_kernel(sink_ref, q_ref, kc_ref, k0_ref, k1_ref, k2_ref, vc_ref, v0_ref, v1_ref, v2_ref, o_ref,
                  *, n_ctx_blocks, n_lat_blocks, j0):
    qb = pl.program_id(1) + j0
    g = pl.program_id(2)
    tq = q_ref.shape[1]
    width = q_ref.shape[2]
    q = q_ref[0]
    lane = lax.broadcasted_iota(jnp.int32, q.shape, 1)
    zero = jnp.zeros_like(q)
    qm = jnp.concatenate(
        [jnp.where((lane >= r * HEAD_DIM) & (lane < (r + 1) * HEAD_DIM), q, zero) for r in range(WA_REP)], axis=0)
    sink_col = jnp.concatenate([jnp.full((tq, 1), sink_ref[g * WA_REP + r], F32) for r in range(WA_REP)], axis=0)
    dims = (((1,), (1,)), ((), ()))

    def finish(o_all, denom):
        o_all = o_all / denom
        lane_o = lax.broadcasted_iota(jnp.int32, (tq, width), 1)
        o = jnp.zeros((tq, width), F32)
        for r in range(WA_REP):
            sel = (lane_o >= r * HEAD_DIM) & (lane_o < (r + 1) * HEAD_DIM)
            o = o + jnp.where(sel, o_all[r * tq:(r + 1) * tq, :], 0.0)
        o_ref[0] = o.astype(o_ref.dtype)

    @pl.when(qb < n_ctx_blocks)
    def _():
        s_c = lax.dot_general(qm, kc_ref[0], dims, preferred_element_type=F32) * ATT_SCALE
        m = jnp.maximum(jnp.max(s_c, axis=-1, keepdims=True), sink_col)
        p_c = jnp.exp(s_c - m)
        denom = jnp.sum(p_c, axis=-1, keepdims=True) + jnp.exp(sink_col - m)
        finish(jnp.dot(p_c.astype(BF16), vc_ref[0], preferred_element_type=F32), denom)

    @pl.when(qb >= n_ctx_blocks)
    def _():
        lb = qb - n_ctx_blocks
        kw = jnp.concatenate([k0_ref[0], k1_ref[0], k2_ref[0]], axis=0)
        vw = jnp.concatenate([v0_ref[0], v1_ref[0], v2_ref[0]], axis=0)
        s_c = lax.dot_general(qm, kc_ref[0], dims, preferred_element_type=F32) * ATT_SCALE
        s_w = lax.dot_general(qm, kw, dims, preferred_element_type=F32) * ATT_SCALE
        iq = lax.broadcasted_iota(jnp.int32, s_w.shape, 0) % tq
        ik = lax.broadcasted_iota(jnp.int32, s_w.shape, 1)
        lo = jnp.where(lb == 0, tq, 0)
        hi = jnp.where(lb == n_lat_blocks - 1, 2 * tq, 3 * tq)
        band = (ik >= iq) & (ik <= iq + 2 * WINDOW) & (ik >= lo) & (ik < hi)
        s_w = jnp.where(band, s_w, NEG_INF)
        m = jnp.maximum(jnp.maximum(jnp.max(s_c, axis=-1, keepdims=True), jnp.max(s_w, axis=-1, keepdims=True)),
                        sink_col)
        p_c = jnp.exp(s_c - m)
        p_w = jnp.exp(s_w - m)
        denom = (jnp.sum(p_c, axis=-1, keepdims=True) + jnp.sum(p_w, axis=-1, keepdims=True)
                 + jnp.exp(sink_col - m))
        o_all = (jnp.dot(p_c.astype(BF16), vc_ref[0], preferred_element_type=F32)
                 + jnp.dot(p_w.astype(BF16), vw, preferred_element_type=F32))
        finish(o_all, denom)


def _window_attention(qkv, sink, ctx_len, ctx_queries):
    B, P, _ = qkv.shape
    tq = WIN_TILE
    assert tq == WINDOW
    nctx = ctx_len // tq
    nlat = P // tq - nctx
    j0 = 0 if ctx_queries else nctx
    G = WA_KV_HEADS
    width = WA_REP * HEAD_DIM

    def lat_spec(off, col0):
        return pl.BlockSpec((1, tq, width),
                            lambda b, j, g: (b, jnp.clip(j + j0 - nctx + off, 0, nlat - 1) + nctx, col0 + g))

    q_spec = pl.BlockSpec((1, tq, width), lambda b, j, g: (b, j + j0, g))
    kc_spec = pl.BlockSpec((1, ctx_len, width), lambda b, j, g: (b, 0, G + g))
    vc_spec = pl.BlockSpec((1, ctx_len, width), lambda b, j, g: (b, 0, 2 * G + g))
    kern = functools.partial(_wattn_kernel, n_ctx_blocks=nctx, n_lat_blocks=nlat, j0=j0)
    return pl.pallas_call(
        kern,
        grid=(B, P // tq - j0, G),
        in_specs=([pl.BlockSpec(memory_space=pltpu.SMEM), q_spec, kc_spec]
                  + [lat_spec(off, G) for off in (-1, 0, 1)] + [vc_spec]
                  + [lat_spec(off, 2 * G) for off in (-1, 0, 1)]),
        out_specs=pl.BlockSpec((1, tq, width), lambda b, j, g: (b, j + j0, g)),
        out_shape=jax.ShapeDtypeStruct((B, P, G * width), BF16),
        compiler_params=_cparams(3),
    )(sink, qkv, qkv, qkv, qkv, qkv, qkv, qkv, qkv, qkv)


def _resid_ln_kernel(o_ref, w_ref, s_ref, m_ref, g_ref, b_ref, out_ref):
    y = jnp.dot(o_ref[0], w_ref[...], preferred_element_type=F32)
    z = DEEPNORM_ALPHA * s_ref[0] + m_ref[0, 0][2:3] * y
    out_ref[0] = _layer_norm_rows(z, g_ref[...], b_ref[...])


def _out_proj_resid_ln(O, w, S, M, ln_g, ln_b, ctx_len, skip_ctx):
    B, P, D = S.shape
    kdim = O.shape[2]
    tm = ROW_TILE
    j0 = ctx_len // tm if skip_ctx else 0
    row = lambda b, j: (b, j + j0, 0)
    return pl.pallas_call(
        _resid_ln_kernel,
        grid=(B, P // tm - j0),
        in_specs=[pl.BlockSpec((1, tm, kdim), row),
                  pl.BlockSpec((kdim, D), lambda b, j: (0, 0)),
                  pl.BlockSpec((1, tm, D), row),
                  _mod_spec(D, ctx_len // tm, j0),
                  pl.BlockSpec((1, D), lambda b, j: (0, 0)),
                  pl.BlockSpec((1, D), lambda b, j: (0, 0))],
        out_specs=pl.BlockSpec((1, tm, D), row),
        out_shape=jax.ShapeDtypeStruct((B, P, D), F32),
        input_output_aliases={2: 0},
        compiler_params=_cparams(2),
    )(O, w, S, M, ln_g.reshape(1, D), ln_b.reshape(1, D))


def _s5_kernel(s_ref, mt_ref, bm_ref, cm_ref, lr_ref, li_ref, y_ref, xbuf, state):
    d = pl.program_id(0)
    c = pl.program_id(2)
    tc, nb, w_in = s_ref.shape
    half = lr_ref.shape[-1]

    @pl.when(c == 0)
    def _():
        state[...] = jnp.zeros_like(state)

    mt = mt_ref[0]
    u = s_ref[...] * (1.0 + mt[1][None]) + mt[0][None]
    u2 = u.reshape(tc * nb, w_in).astype(BF16)
    xbuf[...] = jnp.dot(u2, bm_ref[0, 0], preferred_element_type=F32)
    lr = jnp.broadcast_to(lr_ref[0, 0], (nb, half))
    li = jnp.broadcast_to(li_ref[0, 0], (nb, half))

    def step(t, carry):
        sr, si = carry
        tt = jnp.where(d == 0, t, tc - 1 - t)
        r0 = pl.multiple_of(tt * nb, nb)
        nr = lr * sr - li * si + xbuf[pl.ds(r0, nb), 0:half]
        ni = lr * si + li * sr + xbuf[pl.ds(r0, nb), half:2 * half]
        xbuf[pl.ds(r0, nb), 0:half] = nr
        xbuf[pl.ds(r0, nb), half:2 * half] = ni
        return nr, ni

    sr, si = lax.fori_loop(0, tc, step, (state[:, 0:half], state[:, half:2 * half]), unroll=4)
    state[:, 0:half] = sr
    state[:, half:2 * half] = si
    y = jnp.dot(xbuf[...].astype(BF16), cm_ref[0, 0], preferred_element_type=F32)
    y_ref[0] = y.reshape(tc, nb, w_in)


def _s5_scan(S_tm, Mt, bmat, cmat, lam_r, lam_i, ctx_len):
    P, B, D = S_tm.shape
    tc = SCAN_CHUNK
    w_in = SSM_SLAB_GROUPS * SSM_GROUP
    n_slab = D // w_in
    half = SSM_SLAB_GROUPS * SSM_STATE
    nch = P // tc
    nctx = ctx_len // tc
    nlat = nch - nctx

    def chunk(d, c):
        q = nch - 1 - c
        back = jnp.where(q < nlat, q + nctx, q - nlat)
        return jnp.where(d == 0, c, back)

    return pl.pallas_call(
        _s5_kernel,
        grid=(2, n_slab, nch),
        in_specs=[pl.BlockSpec((tc, B, w_in), lambda d, s, c: (chunk(d, c), 0, s)),
                  pl.BlockSpec((1, 2, B, w_in), lambda d, s, c: (jnp.minimum(chunk(d, c) // nctx, 1), 0, 0, s)),
                  pl.BlockSpec((1, 1, w_in, 2 * half), lambda d, s, c: (d, s, 0, 0)),
                  pl.BlockSpec((1, 1, 2 * half, w_in), lambda d, s, c: (d, s, 0, 0)),
                  pl.BlockSpec((1, 1, 1, half), lambda d, s, c: (d, s, 0, 0)),
                  pl.BlockSpec((1, 1, 1, half), lambda d, s, c: (d, s, 0, 0))],
        out_specs=pl.BlockSpec((1, tc, B, w_in), lambda d, s, c: (d, chunk(d, c), 0, s)),
        out_shape=jax.ShapeDtypeStruct((2, P, B, D), F32),
        scratch_shapes=[pltpu.VMEM((tc * B, 2 * half), F32), pltpu.VMEM((B, 2 * half), F32)],
        compiler_params=_cparams(3),
    )(S_tm, Mt, bmat, cmat, lam_r, lam_i)


def _s5_params(lam_re, lam_im, log_step, b_re, b_im, c_re, c_im):
    lam = lax.complex(lam_re.astype(F32), lam_im.astype(F32))
    step = jnp.exp(log_step.astype(F32))[..., None]
    lam_bar = jnp.exp(lam * step)
    b_bar = lax.complex(b_re.astype(F32), b_im.astype(F32)) * ((lam_bar - 1) / lam)[..., None]
    G, Pst, Hg = b_bar.shape[1:]
    ng = SSM_SLAB_GROUPS
    ns = G // ng
    eye = jnp.eye(ng, dtype=F32)

    def bdiag_in(x):
        x = x.reshape(2, ns, ng, Pst, Hg)
        return jnp.einsum('rsgph,gk->rsghkp', x, eye).reshape(2, ns, ng * Hg, ng * Pst)

    def bdiag_out(x):
        x = x.reshape(2, ns, ng, Hg, Pst)
        return jnp.einsum('rsghp,gk->rsgpkh', x, eye).reshape(2, ns, ng * Pst, ng * Hg)

    bmat = jnp.concatenate([bdiag_in(jnp.real(b_bar)), bdiag_in(jnp.imag(b_bar))], axis=-1).astype(BF16)
    cmat = jnp.concatenate([bdiag_out(c_re.astype(F32)), bdiag_out(-c_im.astype(F32))], axis=-2).astype(BF16)
    lam_r = jnp.real(lam_bar).reshape(2, ns, 1, ng * Pst)
    lam_i = jnp.imag(lam_bar).reshape(2, ns, 1, ng * Pst)
    return bmat, cmat, lam_r, lam_i


def _glu_ln_kernel(s_ref, y0_ref, y1_ref, d_ref, w_ref, m_ref, g_ref, b_ref, out_ref):
    m = m_ref[0, 0]
    s = s_ref[0]
    y = d_ref[...] * _modulate(s, m, 0) + y0_ref[0, 0] + y1_ref[0, 0]
    z = jnp.dot(_gelu_tanh(y).astype(BF16), w_ref[...], preferred_element_type=F32)
    D = s.shape[1]
    o = z[:, 0:D] * _sigmoid(z[:, D:2 * D])
    out_ref[0] = _layer_norm_rows(DEEPNORM_ALPHA * s + m[2:3] * o, g_ref[...], b_ref[...])


def _glu_resid_ln(S, Y, d_skip, w_glu, M, ln_g, ln_b, ctx_len):
    B, P, D = S.shape
    tm = ROW_TILE
    row = lambda b, j: (b, j, 0)
    return pl.pallas_call(
        _glu_ln_kernel,
        grid=(B, P // tm),
        in_specs=[pl.BlockSpec((1, tm, D), row),
                  pl.BlockSpec((1, 1, tm, D), lambda b, j: (0, b, j, 0)),
                  pl.BlockSpec((1, 1, tm, D), lambda b, j: (1, b, j, 0)),
                  pl.BlockSpec((1, D), lambda b, j: (0, 0)),
                  pl.BlockSpec((D, 2 * D), lambda b, j: (0, 0)),
                  _mod_spec(D, ctx_len // tm, 0),
                  pl.BlockSpec((1, D), lambda b, j: (0, 0)),
                  pl.BlockSpec((1, D), lambda b, j: (0, 0))],
        out_specs=pl.BlockSpec((1, tm, D), row),
        out_shape=jax.ShapeDtypeStruct((B, P, D), F32),
        compiler_params=_cparams(2),
    )(S, Y, Y, d_skip.reshape(1, D), w_glu, M, ln_g.reshape(1, D), ln_b.reshape(1, D))


def _peer_topk_kernel(s_ref, m_ref, wqt_ref, keys_ref, g_out, e_out, qt):
    tn = s_ref.shape[1]
    nk = PEER_NKEYS
    k = PEER_TOPK
    xb = _modulate(s_ref[0], m_ref[0, 0], 3).astype(BF16)
    qt[...] = lax.dot_general(wqt_ref[...], xb, (((1,), (1,)), ((), ())), preferred_element_type=F32)

    def head(h, carry):
        krow = lax.broadcasted_iota(jnp.int32, (nk, tn), 0).astype(F32)
        rank = lax.broadcasted_iota(jnp.int32, (k, tn), 0)
        sub = lax.broadcasted_iota(jnp.int32, (SUBLANES, tn), 0)
        subf = sub.astype(F32)
        tops = []
        for j in range(2):
            qs = qt[pl.ds(pl.multiple_of(h * 2 * nk + j * nk, nk), nk), :].astype(BF16)
            s = jnp.dot(keys_ref[j, h], qs, preferred_element_type=F32)
            top = []
            for it in range(k):
                mx = jnp.max(s, axis=0, keepdims=True)
                ix = jnp.min(jnp.where(s == mx, krow, float(nk)), axis=0, keepdims=True)
                s = jnp.where(krow == ix, NEG_INF, s)
                top.append((mx, ix))
            tops.append(top)
        def stack8(items):
            out = jnp.zeros((SUBLANES, tn), F32)
            for s_i, item in enumerate(items):
                out = jnp.where(sub == s_i, item, out)
            return out

        def pack(first, second, comb):
            a_lo = first[:SUBLANES]
            a_hi = pltpu.roll(stack8(first[SUBLANES:]), 2, 0)
            b_lo = stack8(second[:SUBLANES])
            b_hi = stack8(second[SUBLANES:])
            b_0 = second[0]
            return [comb(a_lo[0], b_lo), comb(a_lo[0], b_hi), comb(a_lo[1], b_lo),
                    jnp.where(sub < 5, comb(a_lo[2], b_lo), comb(a_lo[4], pltpu.roll(b_lo, 5, 0))),
                    jnp.where(sub < 4, comb(a_lo[3], b_lo),
                              jnp.where(sub < 6, comb(a_lo[5], pltpu.roll(b_lo, 4, 0)),
                                        comb(a_lo[6], pltpu.roll(b_lo, 6, 0)))),
                    jnp.where(sub < 2, comb(a_lo[7], b_lo), comb(a_hi, b_0)),
                    comb(a_hi, b_0)]

        vals = pack([t[0] for t in tops[0]], [t[0] for t in tops[1]], lambda a, b: a + b)
        vals[6] = jnp.where(sub < 2, vals[6], NEG_INF)
        cand = jnp.concatenate(vals, axis=0)
        cexp = jnp.concatenate(pack([t[1] for t in tops[0]], [t[1] for t in tops[1]],
                                    lambda a, b: a * float(nk) + b), axis=0)
        pos_const = jnp.concatenate(
            [subf, 8.0 + subf, 16.0 + subf,
             jnp.where(sub < 5, 32.0 + subf, 59.0 + subf),
             jnp.where(sub < 4, 48.0 + subf, jnp.where(sub < 6, 76.0 + subf, 90.0 + subf)),
             jnp.where(sub < 2, 112.0 + subf, 16.0 * (subf + 6.0)),
             16.0 * (subf + 14.0)], axis=0)
        ts = jnp.zeros((k, tn), F32)
        te = jnp.zeros((k, tn), F32)
        best = None
        for it in range(k):
            mx = jnp.max(cand, axis=0, keepdims=True)
            sel = jnp.min(jnp.where(cand == mx, pos_const, float(k * k)), axis=0, keepdims=True)
            hit = pos_const == sel
            ex = jnp.max(jnp.where(hit, cexp, -1.0), axis=0, keepdims=True)
            cand = jnp.where(hit, NEG_INF, cand)
            ts = jnp.where(rank == it, mx, ts)
            te = jnp.where(rank == it, ex, te)
            best = mx if best is None else best
        p = jnp.exp(ts - best)
        gate = p / jnp.sum(p, axis=0, keepdims=True)
        r0 = pl.multiple_of(h * k, k)
        g_out[0, pl.ds(r0, k), :] = gate
        e_out[0, pl.ds(r0, k), :] = te.astype(jnp.int32)
        return carry

    lax.fori_loop(0, PEER_HEADS, head, 0)


def _peer_topk(S, M, wq_t, keys, ctx_len, skip_ctx):
    B, P, D = S.shape
    tn = ROW_TILE
    j0 = ctx_len // tn if skip_ctx else 0
    nblk = P // tn - j0
    hk = PEER_HEADS * PEER_TOPK
    out_spec = pl.BlockSpec((1, hk, tn), lambda b, j: (b, 0, j))
    return pl.pallas_call(
        _peer_topk_kernel,
        grid=(B, nblk),
        in_specs=[pl.BlockSpec((1, tn, D), lambda b, j: (b, j + j0, 0)),
                  _mod_spec(D, ctx_len // tn, j0),
                  pl.BlockSpec(wq_t.shape, lambda b, j: (0, 0)),
                  pl.BlockSpec(keys.shape, lambda b, j: (0, 0, 0, 0))],
        out_specs=[out_spec, out_spec],
        out_shape=[jax.ShapeDtypeStruct((B, hk, nblk * tn), F32),
                   jax.ShapeDtypeStruct((B, hk, nblk * tn), jnp.int32)],
        scratch_shapes=[pltpu.VMEM((wq_t.shape[0], tn), F32)],
        compiler_params=_cparams(2),
    )(S, M, wq_t, keys)


def _sublane_sums(vregs):
    sub = lax.broadcasted_iota(jnp.int32, (SUBLANES, LANES), 0)
    level = list(vregs)
    half = SUBLANES // 2
    while half >= 1:
        lower = (sub % (2 * half)) < half
        nxt = []
        for k in range(len(level) // 2):
            a, b = level[k], level[k + len(level) // 2]
            stay = jnp.where(lower, a, b)
            move = jnp.where(lower, b, a)
            if 2 * half == SUBLANES:
                moved = pltpu.roll(move, half, 0)
            else:
                moved = jnp.where(lower, pltpu.roll(move, SUBLANES - half, 0), pltpu.roll(move, half, 0))
            nxt.append(stay + moved)
        level = nxt
        half //= 2
    return level[0]


def _sublane_transpose(vregs):
    sub = lax.broadcasted_iota(jnp.int32, (SUBLANES, LANES), 0)
    v = list(vregs)
    half = SUBLANES // 2
    while half >= 1:
        clear = (sub & half) == 0
        for i in range(SUBLANES):
            if i & half == 0:
                a, b = v[i], v[i + half]
                v[i] = jnp.where(clear, a, pltpu.roll(b, half, 0))
                v[i + half] = jnp.where(clear, pltpu.roll(a, SUBLANES - half, 0), b)
        half //= 2
    return v


def _peer_gather_kernel(idx_ref, nxt_ref, g_ref, s_ref, m_ref, lg_ref, lb_ref, tab_ref, out_ref, buf, sem, obuf,
                        cbuf):
    i = pl.program_id(0)
    n_steps = pl.num_programs(0)
    tb = s_ref.shape[0]
    hk = g_ref.shape[1]
    gt = PEER_TOK_GROUP
    rows = gt * hk
    n_groups = tb // gt
    pgs = hk // SUBLANES

    def row_copy(e, slot, r, prio):
        pltpu.make_async_copy(tab_ref.at[e], buf.at[slot, r], sem.at[slot]).start(priority=prio)

    def issue(ref, grp, slot):
        per_it = 4 * SUBLANES

        def body(it, carry):
            for k in range(per_it):
                row_copy(ref[grp * rows + it * per_it + k], slot, it * per_it + k, k % 2)
            return carry

        lax.fori_loop(0, rows // per_it, body, 0)

    def issue_token(ref, grp, slot, t):
        for r in range(t * hk, (t + 1) * hk):
            row_copy(ref[grp * rows + r], slot, r, r % 2)

    def wait(slot):
        pltpu.make_async_copy(buf.at[(slot + 1) % n_groups], buf.at[slot], sem.at[slot]).wait()

    @pl.when(i == 0)
    def _():
        for g0 in range(PEER_LOOKAHEAD):
            issue(idx_ref, g0, g0)

    m = m_ref[0]
    s_tok = []
    for t0 in range(0, tb, SUBLANES):
        s_tok += _sublane_transpose([s_ref[t0:t0 + SUBLANES, j * LANES:(j + 1) * LANES] for j in range(SUBLANES)])
    s_rows = jnp.stack(s_tok, axis=0)
    h_rows = s_rows * (1.0 + m[4][None]) + m[3][None]
    gates = g_ref[0]
    for grp in range(n_groups):
        slot = grp
        ahead = grp + PEER_LOOKAHEAD
        ahead_ref, ahead_grp = (idx_ref, ahead) if ahead < n_groups else (nxt_ref, ahead - n_groups)
        wait(slot)

        def expert_scores(t):
            h_t = h_rows[grp * gt + t]
            scs = []
            for q in range(pgs):
                r0 = t * hk + q * SUBLANES
                part = _sublane_sums([buf[slot, r0 + k, 0] * h_t for k in range(SUBLANES)])
                scs.append(jnp.sum(part, axis=1, keepdims=True))
            return jnp.concatenate(scs, axis=0)

        def expert_mix(t, sc):
            tok = grp * gt + t
            cbuf[tok] = jnp.broadcast_to(gates[:, tok:tok + 1] * _gelu_tanh(sc), (hk, LANES))
            accs = [None] * 4
            for r in range(hk):
                term = jnp.broadcast_to(cbuf[tok, r:r + 1, :], (SUBLANES, LANES)) * buf[slot, t * hk + r, 1]
                accs[r % 4] = term if accs[r % 4] is None else accs[r % 4] + term
            obuf[tok] = (accs[0] + accs[1]) + (accs[2] + accs[3])

        sc_prev = expert_scores(0)
        for t in range(gt):
            sc_next = expert_scores(t + 1) if t + 1 < gt else None
            issue_token(ahead_ref, ahead_grp, ahead % n_groups, t)
            expert_mix(t, sc_prev)
            sc_prev = sc_next
    z = DEEPNORM_ALPHA * s_rows + m[5][None] * obuf[...]
    n_el = z.shape[1] * z.shape[2]
    mu = jnp.sum(jnp.sum(z, axis=2, keepdims=True), axis=1, keepdims=True) / n_el
    zc = z - mu
    var = jnp.sum(jnp.sum(zc * zc, axis=2, keepdims=True), axis=1, keepdims=True) / n_el
    y = zc * lax.rsqrt(var + LN_EPS) * lg_ref[...][None] + lb_ref[...][None]
    for t0 in range(0, tb, SUBLANES):
        tiles = _sublane_transpose([y[t0 + t] for t in range(SUBLANES)])
        for j in range(SUBLANES):
            out_ref[t0:t0 + SUBLANES, j * LANES:(j + 1) * LANES] = tiles[j]

    @pl.when(i == n_steps - 1)
    def _():
        for slot in range(PEER_LOOKAHEAD):
            wait(slot)


def _peer_gather(S, M, gates_t, eidx, table, ln_g, ln_b, ctx_len, skip_ctx):
    B, P, D = S.shape
    tb = PEER_TOK_BLOCK
    n_slots = tb // PEER_TOK_GROUP
    assert PEER_LOOKAHEAD < n_slots
    hk = PEER_HEADS * PEER_TOPK
    j0 = ctx_len // tb if skip_ctx else 0
    per_b = P // tb - j0
    n_steps = B * per_b
    nctx = ctx_len // tb
    sub = D // LANES
    assert sub == SUBLANES
    S2 = S.reshape(B * P, D)
    M3 = M.reshape(B * 2, SUBLANES, sub, LANES)
    row_blk = lambda i: (i // per_b) * (P // tb) + i % per_b + j0
    out = pl.pallas_call(
        _peer_gather_kernel,
        grid=(n_steps,),
        in_specs=[pl.BlockSpec((tb * hk,), lambda i: (i,), memory_space=pltpu.SMEM),
                  pl.BlockSpec((tb * hk,), lambda i: (jnp.minimum(i + 1, n_steps - 1),), memory_space=pltpu.SMEM),
                  pl.BlockSpec((1, hk, tb), lambda i: (i, 0, 0)),
                  pl.BlockSpec((tb, D), lambda i: (row_blk(i), 0)),
                  pl.BlockSpec((1, SUBLANES, sub, LANES),
                               lambda i: ((i // per_b) * 2 + jnp.minimum((i % per_b + j0) // nctx, 1), 0, 0, 0)),
                  pl.BlockSpec((sub, LANES), lambda i: (0, 0)),
                  pl.BlockSpec((sub, LANES), lambda i: (0, 0)),
                  pl.BlockSpec(memory_space=pl.ANY)],
        out_specs=pl.BlockSpec((tb, D), lambda i: (row_blk(i), 0)),
        out_shape=jax.ShapeDtypeStruct((B * P, D), F32),
        scratch_shapes=[pltpu.VMEM((n_slots, PEER_TOK_GROUP * hk, 2, sub, LANES), F32),
                        pltpu.SemaphoreType.DMA((n_slots,)),
                        pltpu.VMEM((tb, sub, LANES), F32),
                        pltpu.VMEM((tb, hk, LANES), F32)],
        input_output_aliases={3: 0},
        compiler_params=_cparams(1),
    )(eidx, eidx, gates_t, S2, M3, ln_g.reshape(sub, LANES), ln_b.reshape(sub, LANES), table)
    return out.reshape(B, P, D)


def _peer_ffn_resid_ln(S, M, wq, keys, u_tab, v_tab, ln_g, ln_b, ctx_len, skip_ctx):
    B, P, D = S.shape
    wq_t = wq.T.astype(BF16)
    gates, eidx = _peer_topk(S, M, wq_t, keys.astype(BF16), ctx_len, skip_ctx)
    hk = gates.shape[1]
    tb = PEER_TOK_BLOCK
    gates_t = jnp.transpose(gates.reshape(B, hk, -1, tb), (0, 2, 1, 3)).reshape(-1, hk, tb)
    eidx_flat = jnp.transpose(eidx, (0, 2, 1)).reshape(-1)
    table = jnp.stack([u_tab, v_tab], axis=1).reshape(-1, 2, D // LANES, LANES)
    return _peer_gather(S, M, gates_t, eidx_flat, table, ln_g, ln_b, ctx_len, skip_ctx)


def _deinterleave_heads(w):
    d_in, n = w.shape
    return w.reshape(d_in, n // HEAD_DIM, HEAD_DIM // 2, 2).transpose(0, 1, 3, 2).reshape(d_in, n)


def _rope_tables(ctx_len, n_lat):
    rows = n_lat // GRID_W
    row = jnp.repeat(jnp.arange(rows, dtype=F32), GRID_W)
    col = jnp.tile(jnp.arange(GRID_W, dtype=F32), rows)
    n_freq = HEAD_DIM // 4
    inv = ROPE_BASE ** (-jnp.arange(n_freq, dtype=F32) / n_freq)
    ang = jnp.concatenate([row[:, None] * inv, col[:, None] * inv], -1)
    cos, sin = jnp.cos(ang), jnp.sin(ang)
    reps = LANES // HEAD_DIM
    cos_t = jnp.tile(jnp.concatenate([cos, cos], -1), (1, reps))
    sin_t = jnp.tile(jnp.concatenate([-sin, sin], -1), (1, reps))
    cos_t = jnp.concatenate([jnp.ones((ctx_len, LANES), F32), cos_t], 0)
    sin_t = jnp.concatenate([jnp.zeros((ctx_len, LANES), F32), sin_t], 0)
    return cos_t, sin_t


def _mixer_layer(i, S, M, cos_t, sin_t, L, last, da_wqkv, da_wo, da_lambda, da_subln, wa_wqkv, wa_wo, wa_sink,
                 ssm_lam_re, ssm_lam_im, ssm_log_step, ssm_b_re, ssm_b_im, ssm_c_re, ssm_c_im, ssm_d, ssm_w_glu,
                 ln_g, ln_b):
    D = S.shape[2]
    kind, j = i % N_MIXERS, i // N_MIXERS
    if kind == 0:
        lam_init = 0.8 - 0.6 * math.exp(-0.3 * i)
        w = da_wqkv[j]
        n_qk = 2 * DA_HEADS * 2 * HEAD_DIM
        w = jnp.concatenate([_deinterleave_heads(w[:, :n_qk]), w[:, n_qk:]], axis=1).astype(BF16)
        qkv = _project(S, M, w, cos_t, sin_t, n_qk, L)
        O = _diff_attention(qkv, da_lambda[j], da_subln[j], lam_init, L, not last)
        return _out_proj_resid_ln(O, da_wo[j].astype(BF16), S, M, ln_g[i, 0], ln_b[i, 0], L, last)
    if kind == 1:
        w = wa_wqkv[j]
        nq = WA_Q_HEADS * HEAD_DIM
        nkv = WA_KV_HEADS * HEAD_DIM
        rep = lambda m: jnp.tile(m.reshape(D, WA_KV_HEADS, 1, HEAD_DIM), (1, 1, WA_REP, 1)).reshape(D, nq)
        w = jnp.concatenate([_deinterleave_heads(w[:, :nq]),
                             rep(_deinterleave_heads(w[:, nq:nq + nkv])),
                             rep(w[:, nq + nkv:])], axis=1).astype(BF16)
        qkv = _project(S, M, w, cos_t, sin_t, 2 * nq, L)
        O = _window_attention(qkv, wa_sink[j], L, not last)
        return _out_proj_resid_ln(O, wa_wo[j].astype(BF16), S, M, ln_g[i, 0], ln_b[i, 0], L, last)
    bmat, cmat, lam_r, lam_i = _s5_params(ssm_lam_re[j], ssm_lam_im[j], ssm_log_step[j], ssm_b_re[j],
                                          ssm_b_im[j], ssm_c_re[j], ssm_c_im[j])
    Mt = jnp.transpose(M[:, :, 0:2, :], (1, 2, 0, 3))
    Y_tm = _s5_scan(jnp.transpose(S, (1, 0, 2)), Mt, bmat, cmat, lam_r, lam_i, L)
    Y = jnp.transpose(Y_tm, (0, 2, 1, 3))
    return _glu_resid_ln(S, Y, ssm_d[j], ssm_w_glu[j].astype(BF16), M, ln_g[i, 0], ln_b[i, 0], L)


def kernel(x, c, ctx, c_ctx, mod_w, mod_b, ln_g, ln_b, peer_wq, peer_keys, peer_u, peer_v, da_wqkv, da_wo, da_lambda, da_subln, wa_wqkv, wa_wo, wa_sink, ssm_lam_re, ssm_lam_im, ssm_log_step, ssm_b_re, ssm_b_im, ssm_c_re, ssm_c_im, ssm_d, ssm_w_glu):
    B, T, D = x.shape
    L = ctx.shape[1]
    depth = mod_w.shape[0]
    cos_t, sin_t = _rope_tables(L, T)
    M_all = _mod_vectors(c, c_ctx, mod_w, mod_b)
    S = jnp.concatenate([ctx, x], axis=1)
    for i in range(depth):
        last = i == depth - 1
        S = _mixer_layer(i, S, M_all[i], cos_t, sin_t, L, last, da_wqkv, da_wo, da_lambda, da_subln, wa_wqkv, wa_wo,
                         wa_sink, ssm_lam_re, ssm_lam_im, ssm_log_step, ssm_b_re, ssm_b_im, ssm_c_re, ssm_c_im,
                         ssm_d, ssm_w_glu, ln_g, ln_b)
        S = _peer_ffn_resid_ln(S, M_all[i], peer_wq[i], peer_keys[i], peer_u[i], peer_v[i], ln_g[i, 1], ln_b[i, 1],
                               L, last)
    return S[:, L:, :]
```

```python
import functools
import math

import jax
import jax.numpy as jnp
from jax import lax
from jax.experimental import pallas as pl
from jax.experimental.pallas import tpu as pltpu

F32 = jnp.float32
BF16 = jnp.bfloat16

DEPTH = 4
N_MIXERS = 3
GRID_W = 64
HEAD_DIM = 64
ROPE_BASE = 10000.0
DA_HEADS = 8
WA_Q_HEADS = 16
WA_KV_HEADS = 4
WA_REP = WA_Q_HEADS // WA_KV_HEADS
WINDOW = 128
SSM_GROUP = 16
SSM_STATE = 64
PEER_HEADS = 8
PEER_NKEYS = 128
PEER_QDIM = 256
PEER_TOPK = 16
LN_EPS = 1e-5
DEEPNORM_ALPHA = (2 * DEPTH) ** 0.25
ATT_SCALE = HEAD_DIM ** -0.5

LANES = 128
SUBLANES = 8
ROW_TILE = 256
WIN_TILE = 128
SCAN_CHUNK = 128
SSM_SLAB_GROUPS = 8
PEER_TOK_BLOCK = 32
PEER_TOK_GROUP = 8
PEER_LOOKAHEAD = 2
VMEM_LIMIT = 48 * 1024 * 1024
NEG_INF = float("-inf")


def _cparams(n_axes):
    return pltpu.CompilerParams(dimension_semantics=("arbitrary",) * n_axes, vmem_limit_bytes=VMEM_LIMIT)


def _gelu_tanh(x):
    return 0.5 * x * (1.0 + jnp.tanh(math.sqrt(2.0 / math.pi) * (x + 0.044715 * (x * x * x))))


def _sigmoid(x):
    return 1.0 / (1.0 + jnp.exp(-x))


def _layer_norm_rows(z, g, b):
    mu = jnp.mean(z, axis=-1, keepdims=True)
    zc = z - mu
    var = jnp.mean(zc * zc, axis=-1, keepdims=True)
    return zc * lax.rsqrt(var + LN_EPS) * g + b


def _modulate(x, m, shift_idx):
    return x * (1.0 + m[shift_idx + 1:shift_idx + 2]) + m[shift_idx:shift_idx + 1]


def _mod_kernel(a_ref, w_ref, b_ref, o_ref):
    a = a_ref[...]
    a = a * _sigmoid(a)
    o_ref[0] = jnp.dot(a, w_ref[0], preferred_element_type=F32, precision=lax.Precision.HIGHEST) + b_ref[0]


def _mod_vectors(c, c_ctx, mod_w, mod_b):
    B, D = c.shape
    depth, _, n6 = mod_w.shape
    rows = -(-(B + 1) // SUBLANES) * SUBLANES
    a = jnp.zeros((rows, D), F32).at[:B].set(c).at[B].set(c_ctx)
    tn = n6 // 4
    out = pl.pallas_call(
        _mod_kernel,
        grid=(depth, n6 // tn),
        in_specs=[pl.BlockSpec((rows, D), lambda i, j: (0, 0)),
                  pl.BlockSpec((1, D, tn), lambda i, j: (i, 0, j)),
                  pl.BlockSpec((1, 1, tn), lambda i, j: (i, 0, j))],
        out_specs=pl.BlockSpec((1, rows, tn), lambda i, j: (i, 0, j)),
        out_shape=jax.ShapeDtypeStruct((depth, rows, n6), F32),
        compiler_params=_cparams(2),
    )(a, mod_w, mod_b.reshape(depth, 1, n6))
    lat = out[:, :B].reshape(depth, B, 6, D)
    ctx = jnp.broadcast_to(out[:, B].reshape(depth, 1, 6, D), (depth, B, 6, D))
    m = jnp.stack([ctx, lat], axis=2)
    return jnp.pad(m, ((0, 0), (0, 0), (0, 0), (0, 2), (0, 0)))


def _mod_spec(D, nctx_blocks, j0):
    return pl.BlockSpec((1, 1, SUBLANES, D), lambda b, j: (b, jnp.minimum((j + j0) // nctx_blocks, 1), 0, 0))


def _proj_kernel(s_ref, m_ref, w_ref, cos_ref, sin_ref, o_ref, *, n_rope, tn):
    xb = _modulate(s_ref[0], m_ref[0, 0], 0).astype(BF16)
    tm = xb.shape[0]
    n_out = w_ref.shape[1]
    cos_t = cos_ref[...]
    sin_t = sin_ref[...]
    lane = lax.broadcasted_iota(jnp.int32, (tm, LANES), 1)
    first_half = (lane % HEAD_DIM) < (HEAD_DIM // 2)
    for c0 in range(0, n_out, tn):
        y = jnp.dot(xb, w_ref[:, c0:c0 + tn], preferred_element_type=F32)
        if c0 < n_rope:
            pieces = []
            for l0 in range(0, tn, LANES):
                yc = y[:, l0:l0 + LANES]
                partner = jnp.where(first_half,
                                    pltpu.roll(yc, LANES - HEAD_DIM // 2, 1),
                                    pltpu.roll(yc, HEAD_DIM // 2, 1))
                pieces.append(yc * cos_t + partner * sin_t)
            y = jnp.concatenate(pieces, axis=1)
        o_ref[0, :, c0:c0 + tn] = y.astype(o_ref.dtype)


def _project(S, M, w, cos_t, sin_t, n_rope, ctx_len):
    B, P, D = S.shape
    n_out = w.shape[1]
    tm = ROW_TILE
    kern = functools.partial(_proj_kernel, n_rope=n_rope, tn=512)
    return pl.pallas_call(
        kern,
        grid=(B, P // tm),
        in_specs=[pl.BlockSpec((1, tm, D), lambda b, j: (b, j, 0)),
                  _mod_spec(D, ctx_len // tm, 0),
                  pl.BlockSpec((D, n_out), lambda b, j: (0, 0)),
                  pl.BlockSpec((tm, LANES), lambda b, j: (j, 0)),
                  pl.BlockSpec((tm, LANES), lambda b, j: (j, 0))],
        out_specs=pl.BlockSpec((1, tm, n_out), lambda b, j: (b, j, 0)),
        out_shape=jax.ShapeDtypeStruct((B, P, n_out), BF16),
        compiler_params=_cparams(2),
    )(S, M, w, cos_t, sin_t)


def _dattn_kernel(lam_ref, g_ref, q_ref, k_ref, v_ref, o_ref, *, lam_init, ctx_len, ctx_queries):
    lp = lam_ref[...]
    lam = (jnp.exp(jnp.sum(lp[0:1] * lp[1:2], axis=-1, keepdims=True))
           - jnp.exp(jnp.sum(lp[2:3] * lp[3:4], axis=-1, keepdims=True)) + lam_init)
    q = q_ref[0]
    lane = lax.broadcasted_iota(jnp.int32, q.shape, 1)
    zero = jnp.zeros_like(q)
    q_maps = (jnp.where(lane < HEAD_DIM, q, zero), jnp.where(lane >= HEAD_DIM, q, zero))

    def attend(nk):
        k = k_ref[0, 0:nk, :]
        v = v_ref[0, 0:nk, :]
        probs = []
        for qm in q_maps:
            s = lax.dot_general(qm, k, (((1,), (1,)), ((), ())), preferred_element_type=F32) * ATT_SCALE
            p = jnp.exp(s - jnp.max(s, axis=-1, keepdims=True))
            probs.append((p, 1.0 / jnp.sum(p, axis=-1, keepdims=True)))
        a = probs[0][0] * probs[0][1] - probs[1][0] * (lam * probs[1][1])
        o = jnp.dot(a.astype(BF16), v, preferred_element_type=F32)
        o = o * lax.rsqrt(jnp.mean(o * o, axis=-1, keepdims=True) + LN_EPS) * g_ref[...] * (1.0 - lam_init)
        o_ref[0] = o.astype(o_ref.dtype)

    if ctx_queries:
        qi = pl.program_id(2)

        @pl.when(qi == 0)
        def _():
            attend(ctx_len)

        @pl.when(qi > 0)
        def _():
            attend(k_ref.shape[1])
    else:
        attend(k_ref.shape[1])


def _diff_attention(qkv, lam_p, subln_g, lam_init, ctx_len, ctx_queries):
    B, P, _ = qkv.shape
    tq = ROW_TILE
    assert ctx_len == tq
    j0 = 0 if ctx_queries else 1
    H = DA_HEADS
    kern = functools.partial(_dattn_kernel, lam_init=lam_init, ctx_len=ctx_len, ctx_queries=ctx_queries)
    return pl.pallas_call(
        kern,
        grid=(B, H, P // tq - j0),
        in_specs=[pl.BlockSpec((4, HEAD_DIM), lambda b, h, i: (0, 0)),
                  pl.BlockSpec((1, 2 * HEAD_DIM), lambda b, h, i: (0, 0)),
                  pl.BlockSpec((1, tq, LANES), lambda b, h, i: (b, i + j0, h)),
                  pl.BlockSpec((1, P, LANES), lambda b, h, i: (b, 0, H + h)),
                  pl.BlockSpec((1, P, LANES), lambda b, h, i: (b, 0, 2 * H + h))],
        out_specs=pl.BlockSpec((1, tq, LANES), lambda b, h, i: (b, i + j0, h)),
        out_shape=jax.ShapeDtypeStruct((B, P, H * LANES), BF16),
        compiler_params=_cparams(3),
    )(lam_p, subln_g.reshape(1, -1), qkv, qkv, qkv)


def _wattn_kernel(sink_ref, q_ref, kc_ref, k0_ref, k1_ref, k2_ref, vc_ref, v0_ref, v1_ref, v2_ref, o_ref,
                  *, n_ctx_blocks, n_lat_blocks, j0):
    qb = pl.program_id(1) + j0
    g = pl.program_id(2)
    tq = q_ref.shape[1]
    width = q_ref.shape[2]
    q = q_ref[0]
    lane = lax.broadcasted_iota(jnp.int32, q.shape, 1)
    zero = jnp.zeros_like(q)
    qm = jnp.concatenate(
        [jnp.where((lane >= r * HEAD_DIM) & (lane < (r + 1) * HEAD_DIM), q, zero) for r in range(WA_REP)], axis=0)
    sink_col = jnp.concatenate([jnp.full((tq, 1), sink_ref[g * WA_REP + r], F32) for r in range(WA_REP)], axis=0)
    dims = (((1,), (1,)), ((), ()))

    def finish(o_all, denom):
        o_all = o_all / denom
        lane_o = lax.broadcasted_iota(jnp.int32, (tq, width), 1)
        o = jnp.zeros((tq, width), F32)
        for r in range(WA_REP):
            sel = (lane_o >= r * HEAD_DIM) & (lane_o < (r + 1) * HEAD_DIM)
            o = o + jnp.where(sel, o_all[r * tq:(r + 1) * tq, :], 0.0)
        o_ref[0] = o.astype(o_ref.dtype)

    @pl.when(qb < n_ctx_blocks)
    def _():
        s_c = lax.dot_general(qm, kc_ref[0], dims, preferred_element_type=F32) * ATT_SCALE
        m = jnp.maximum(jnp.max(s_c, axis=-1, keepdims=True), sink_col)
        p_c = jnp.exp(s_c - m)
        denom = jnp.sum(p_c, axis=-1, keepdims=True) + jnp.exp(sink_col - m)
        finish(jnp.dot(p_c.astype(BF16), vc_ref[0], preferred_element_type=F32), denom)

    @pl.when(qb >= n_ctx_blocks)
    def _():
        lb = qb - n_ctx_blocks
        kw = jnp.concatenate([k0_ref[0], k1_ref[0], k2_ref[0]], axis=0)
        vw = jnp.concatenate([v0_ref[0], v1_ref[0], v2_ref[0]], axis=0)
        s_c = lax.dot_general(qm, kc_ref[0], dims, preferred_element_type=F32) * ATT_SCALE
        s_w = lax.dot_general(qm, kw, dims, preferred_element_type=F32) * ATT_SCALE
        iq = lax.broadcasted_iota(jnp.int32, s_w.shape, 0) % tq
        ik = lax.broadcasted_iota(jnp.int32, s_w.shape, 1)
        lo = jnp.where(lb == 0, tq, 0)
        hi = jnp.where(lb == n_lat_blocks - 1, 2 * tq, 3 * tq)
        band = (ik >= iq) & (ik <= iq + 2 * WINDOW) & (ik >= lo) & (ik < hi)
        s_w = jnp.where(band, s_w, NEG_INF)
        m = jnp.maximum(jnp.maximum(jnp.max(s_c, axis=-1, keepdims=True), jnp.max(s_w, axis=-1, keepdims=True)),
                        sink_col)
        p_c = jnp.exp(s_c - m)
        p_w = jnp.exp(s_w - m)
        denom = (jnp.sum(p_c, axis=-1, keepdims=True) + jnp.sum(p_w, axis=-1, keepdims=True)
                 + jnp.exp(sink_col - m))
        o_all = (jnp.dot(p_c.astype(BF16), vc_ref[0], preferred_element_type=F32)
                 + jnp.dot(p_w.astype(BF16), vw, preferred_element_type=F32))
        finish(o_all, denom)


def _window_attention(qkv, sink, ctx_len, ctx_queries):
    B, P, _ = qkv.shape
    tq = WIN_TILE
    assert tq == WINDOW
    nctx = ctx_len // tq
    nlat = P // tq - nctx
    j0 = 0 if ctx_queries else nctx
    G = WA_KV_HEADS
    width = WA_REP * HEAD_DIM

    def lat_spec(off, col0):
        return pl.BlockSpec((1, tq, width),
                            lambda b, j, g: (b, jnp.clip(j + j0 - nctx + off, 0, nlat - 1) + nctx, col0 + g))

    q_spec = pl.BlockSpec((1, tq, width), lambda b, j, g: (b, j + j0, g))
    kc_spec = pl.BlockSpec((1, ctx_len, width), lambda b, j, g: (b, 0, G + g))
    vc_spec = pl.BlockSpec((1, ctx_len, width), lambda b, j, g: (b, 0, 2 * G + g))
    kern = functools.partial(_wattn_kernel, n_ctx_blocks=nctx, n_lat_blocks=nlat, j0=j0)
    return pl.pallas_call(
        kern,
        grid=(B, P // tq - j0, G),
        in_specs=([pl.BlockSpec(memory_space=pltpu.SMEM), q_spec, kc_spec]
                  + [lat_spec(off, G) for off in (-1, 0, 1)] + [vc_spec]
                  + [lat_spec(off, 2 * G) for off in (-1, 0, 1)]),
        out_specs=pl.BlockSpec((1, tq, width), lambda b, j, g: (b, j + j0, g)),
        out_shape=jax.ShapeDtypeStruct((B, P, G * width), BF16),
        compiler_params=_cparams(3),
    )(sink, qkv, qkv, qkv, qkv, qkv, qkv, qkv, qkv, qkv)


def _resid_ln_kernel(o_ref, w_ref, s_ref, m_ref, g_ref, b_ref, out_ref):
    y = jnp.dot(o_ref[0], w_ref[...], preferred_element_type=F32)
    z = DEEPNORM_ALPHA * s_ref[0] + m_ref[0, 0][2:3] * y
    out_ref[0] = _layer_norm_rows(z, g_ref[...], b_ref[...])


def _out_proj_resid_ln(O, w, S, M, ln_g, ln_b, ctx_len, skip_ctx):
    B, P, D = S.shape
    kdim = O.shape[2]
    tm = ROW_TILE
    j0 = ctx_len // tm if skip_ctx else 0
    row = lambda b, j: (b, j + j0, 0)
    return pl.pallas_call(
        _resid_ln_kernel,
        grid=(B, P // tm - j0),
        in_specs=[pl.BlockSpec((1, tm, kdim), row),
                  pl.BlockSpec((kdim, D), lambda b, j: (0, 0)),
                  pl.BlockSpec((1, tm, D), row),
                  _mod_spec(D, ctx_len // tm, j0),
                  pl.BlockSpec((1, D), lambda b, j: (0, 0)),
                  pl.BlockSpec((1, D), lambda b, j: (0, 0))],
        out_specs=pl.BlockSpec((1, tm, D), row),
        out_shape=jax.ShapeDtypeStruct((B, P, D), F32),
        input_output_aliases={2: 0},
        compiler_params=_cparams(2),
    )(O, w, S, M, ln_g.reshape(1, D), ln_b.reshape(1, D))


def _s5_kernel(s_ref, mt_ref, bm_ref, cm_ref, lr_ref, li_ref, y_ref, xbuf, state):
    d = pl.program_id(0)
    c = pl.program_id(2)
    tc, nb, w_in = s_ref.shape
    half = lr_ref.shape[-1]

    @pl.when(c == 0)
    def _():
        state[...] = jnp.zeros_like(state)

    mt = mt_ref[0]
    u = s_ref[...] * (1.0 + mt[1][None]) + mt[0][None]
    u2 = u.reshape(tc * nb, w_in).astype(BF16)
    xbuf[...] = jnp.dot(u2, bm_ref[0, 0], preferred_element_type=F32)
    lr = jnp.broadcast_to(lr_ref[0, 0], (nb, half))
    li = jnp.broadcast_to(li_ref[0, 0], (nb, half))

    def step(t, carry):
        sr, si = carry
        tt = jnp.where(d == 0, t, tc - 1 - t)
        r0 = pl.multiple_of(tt * nb, nb)
        nr = lr * sr - li * si + xbuf[pl.ds(r0, nb), 0:half]
        ni = lr * si + li * sr + xbuf[pl.ds(r0, nb), half:2 * half]
        xbuf[pl.ds(r0, nb), 0:half] = nr
        xbuf[pl.ds(r0, nb), half:2 * half] = ni
        return nr, ni

    sr, si = lax.fori_loop(0, tc, step, (state[:, 0:half], state[:, half:2 * half]), unroll=4)
    state[:, 0:half] = sr
    state[:, half:2 * half] = si
    y = jnp.dot(xbuf[...].astype(BF16), cm_ref[0, 0], preferred_element_type=F32)
    y_ref[0] = y.reshape(tc, nb, w_in)


def _s5_scan(S_tm, Mt, bmat, cmat, lam_r, lam_i, ctx_len):
    P, B, D = S_tm.shape
    tc = SCAN_CHUNK
    w_in = SSM_SLAB_GROUPS * SSM_GROUP
    n_slab = D // w_in
    half = SSM_SLAB_GROUPS * SSM_STATE
    nch = P // tc
    nctx = ctx_len // tc
    nlat = nch - nctx

    def chunk(d, c):
        q = nch - 1 - c
        back = jnp.where(q < nlat, q + nctx, q - nlat)
        return jnp.where(d == 0, c, back)

    return pl.pallas_call(
        _s5_kernel,
        grid=(2, n_slab, nch),
        in_specs=[pl.BlockSpec((tc, B, w_in), lambda d, s, c: (chunk(d, c), 0, s)),
                  pl.BlockSpec((1, 2, B, w_in), lambda d, s, c: (jnp.minimum(chunk(d, c) // nctx, 1), 0, 0, s)),
                  pl.BlockSpec((1, 1, w_in, 2 * half), lambda d, s, c: (d, s, 0, 0)),
                  pl.BlockSpec((1, 1, 2 * half, w_in), lambda d, s, c: (d, s, 0, 0)),
                  pl.BlockSpec((1, 1, 1, half), lambda d, s, c: (d, s, 0, 0)),
                  pl.BlockSpec((1, 1, 1, half), lambda d, s, c: (d, s, 0, 0))],
        out_specs=pl.BlockSpec((1, tc, B, w_in), lambda d, s, c: (d, chunk(d, c), 0, s)),
        out_shape=jax.ShapeDtypeStruct((2, P, B, D), F32),
        scratch_shapes=[pltpu.VMEM((tc * B, 2 * half), F32), pltpu.VMEM((B, 2 * half), F32)],
        compiler_params=_cparams(3),
    )(S_tm, Mt, bmat, cmat, lam_r, lam_i)


def _s5_params(lam_re, lam_im, log_step, b_re, b_im, c_re, c_im):
    lam = lax.complex(lam_re.astype(F32), lam_im.astype(F32))
    step = jnp.exp(log_step.astype(F32))[..., None]
    lam_bar = jnp.exp(lam * step)
    b_bar = lax.complex(b_re.astype(F32), b_im.astype(F32)) * ((lam_bar - 1) / lam)[..., None]
    G, Pst, Hg = b_bar.shape[1:]
    ng = SSM_SLAB_GROUPS
    ns = G // ng
    eye = jnp.eye(ng, dtype=F32)

    def bdiag_in(x):
        x = x.reshape(2, ns, ng, Pst, Hg)
        return jnp.einsum('rsgph,gk->rsghkp', x, eye).reshape(2, ns, ng * Hg, ng * Pst)

    def bdiag_out(x):
        x = x.reshape(2, ns, ng, Hg, Pst)
        return jnp.einsum('rsghp,gk->rsgpkh', x, eye).reshape(2, ns, ng * Pst, ng * Hg)

    bmat = jnp.concatenate([bdiag_in(jnp.real(b_bar)), bdiag_in(jnp.imag(b_bar))], axis=-1).astype(BF16)
    cmat = jnp.concatenate([bdiag_out(c_re.astype(F32)), bdiag_out(-c_im.astype(F32))], axis=-2).astype(BF16)
    lam_r = jnp.real(lam_bar).reshape(2, ns, 1, ng * Pst)
    lam_i = jnp.imag(lam_bar).reshape(2, ns, 1, ng * Pst)
    return bmat, cmat, lam_r, lam_i


def _glu_ln_kernel(s_ref, y0_ref, y1_ref, d_ref, w_ref, m_ref, g_ref, b_ref, out_ref):
    m = m_ref[0, 0]
    s = s_ref[0]
    y = d_ref[...] * _modulate(s, m, 0) + y0_ref[0, 0] + y1_ref[0, 0]
    z = jnp.dot(_gelu_tanh(y).astype(BF16), w_ref[...], preferred_element_type=F32)
    D = s.shape[1]
    o = z[:, 0:D] * _sigmoid(z[:, D:2 * D])
    out_ref[0] = _layer_norm_rows(DEEPNORM_ALPHA * s + m[2:3] * o, g_ref[...], b_ref[...])


def _glu_resid_ln(S, Y, d_skip, w_glu, M, ln_g, ln_b, ctx_len):
    B, P, D = S.shape
    tm = ROW_TILE
    row = lambda b, j: (b, j, 0)
    return pl.pallas_call(
        _glu_ln_kernel,
        grid=(B, P // tm),
        in_specs=[pl.BlockSpec((1, tm, D), row),
                  pl.BlockSpec((1, 1, tm, D), lambda b, j: (0, b, j, 0)),
                  pl.BlockSpec((1, 1, tm, D), lambda b, j: (1, b, j, 0)),
                  pl.BlockSpec((1, D), lambda b, j: (0, 0)),
                  pl.BlockSpec((D, 2 * D), lambda b, j: (0, 0)),
                  _mod_spec(D, ctx_len // tm, 0),
                  pl.BlockSpec((1, D), lambda b, j: (0, 0)),
                  pl.BlockSpec((1, D), lambda b, j: (0, 0))],
        out_specs=pl.BlockSpec((1, tm, D), row),
        out_shape=jax.ShapeDtypeStruct((B, P, D), F32),
        compiler_params=_cparams(2),
    )(S, Y, Y, d_skip.reshape(1, D), w_glu, M, ln_g.reshape(1, D), ln_b.reshape(1, D))


def _peer_topk_kernel(s_ref, m_ref, wqt_ref, keys_ref, g_out, e_out, qt):
    tn = s_ref.shape[1]
    nk = PEER_NKEYS
    k = PEER_TOPK
    xb = _modulate(s_ref[0], m_ref[0, 0], 3).astype(BF16)
    qt[...] = lax.dot_general(wqt_ref[...], xb, (((1,), (1,)), ((), ())), preferred_element_type=F32)

    def head(h, carry):
        krow = lax.broadcasted_iota(jnp.int32, (nk, tn), 0).astype(F32)
        rank = lax.broadcasted_iota(jnp.int32, (k, tn), 0)
        sub = lax.broadcasted_iota(jnp.int32, (SUBLANES, tn), 0)
        subf = sub.astype(F32)
        tops = []
        for j in range(2):
            qs = qt[pl.ds(pl.multiple_of(h * 2 * nk + j * nk, nk), nk), :].astype(BF16)
            s = jnp.dot(keys_ref[j, h], qs, preferred_element_type=F32)
            top = []
            for it in range(k):
                mx = jnp.max(s, axis=0, keepdims=True)
                ix = jnp.min(jnp.where(s == mx, krow, float(nk)), axis=0, keepdims=True)
                s = jnp.where(krow == ix, NEG_INF, s)
                top.append((mx, ix))
            tops.append(top)
        def stack8(items):
            out = jnp.zeros((SUBLANES, tn), F32)
            for s_i, item in enumerate(items):
                out = jnp.where(sub == s_i, item, out)
            return out

        def pack(first, second, comb):
            a_lo = first[:SUBLANES]
            a_hi = pltpu.roll(stack8(first[SUBLANES:]), 2, 0)
            b_lo = stack8(second[:SUBLANES])
            b_hi = stack8(second[SUBLANES:])
            b_0 = second[0]
            return [comb(a_lo[0], b_lo), comb(a_lo[0], b_hi), comb(a_lo[1], b_lo),
                    jnp.where(sub < 5, comb(a_lo[2], b_lo), comb(a_lo[4], pltpu.roll(b_lo, 5, 0))),
                    jnp.where(sub < 4, comb(a_lo[3], b_lo),
                              jnp.where(sub < 6, comb(a_lo[5], pltpu.roll(b_lo, 4, 0)),
                                        comb(a_lo[6], pltpu.roll(b_lo, 6, 0)))),
                    jnp.where(sub < 2, comb(a_lo[7], b_lo), comb(a_hi, b_0)),
                    comb(a_hi, b_0)]

        vals = pack([t[0] for t in tops[0]], [t[0] for t in tops[1]], lambda a, b: a + b)
        vals[6] = jnp.where(sub < 2, vals[6], NEG_INF)
        cand = jnp.concatenate(vals, axis=0)
        cexp = jnp.concatenate(pack([t[1] for t in tops[0]], [t[1] for t in tops[1]],
                                    lambda a, b: a * float(nk) + b), axis=0)
        pos_const = jnp.concatenate(
            [subf, 8.0 + subf, 16.0 + subf,
             jnp.where(sub < 5, 32.0 + subf, 59.0 + subf),
             jnp.where(sub < 4, 48.0 + subf, jnp.where(sub < 6, 76.0 + subf, 90.0 + subf)),
             jnp.where(sub < 2, 112.0 + subf, 16.0 * (subf + 6.0)),
             16.0 * (subf + 14.0)], axis=0)
        ts = jnp.zeros((k, tn), F32)
        te = jnp.zeros((k, tn), F32)
        best = None
        for it in range(k):
            mx = jnp.max(cand, axis=0, keepdims=True)
            sel = jnp.min(jnp.where(cand == mx, pos_const, float(k * k)), axis=0, keepdims=True)
            hit = pos_const == sel
            ex = jnp.max(jnp.where(hit, cexp, -1.0), axis=0, keepdims=True)
            cand = jnp.where(hit, NEG_INF, cand)
            ts = jnp.where(rank == it, mx, ts)
            te = jnp.where(rank == it, ex, te)
            best = mx if best is None else best
        p = jnp.exp(ts - best)
        gate = p / jnp.sum(p, axis=0, keepdims=True)
        r0 = pl.multiple_of(h * k, k)
        g_out[0, pl.ds(r0, k), :] = gate
        e_out[0, pl.ds(r0, k), :] = te.astype(jnp.int32)
        return carry

    lax.fori_loop(0, PEER_HEADS, head, 0)


def _peer_topk(S, M, wq_t, keys, ctx_len, skip_ctx):
    B, P, D = S.shape
    tn = ROW_TILE
    j0 = ctx_len // tn if skip_ctx else 0
    nblk = P // tn - j0
    hk = PEER_HEADS * PEER_TOPK
    out_spec = pl.BlockSpec((1, hk, tn), lambda b, j: (b, 0, j))
    return pl.pallas_call(
        _peer_topk_kernel,
        grid=(B, nblk),
        in_specs=[pl.BlockSpec((1, tn, D), lambda b, j: (b, j + j0, 0)),
                  _mod_spec(D, ctx_len // tn, j0),
                  pl.BlockSpec(wq_t.shape, lambda b, j: (0, 0)),
                  pl.BlockSpec(keys.shape, lambda b, j: (0, 0, 0, 0))],
        out_specs=[out_spec, out_spec],
        out_shape=[jax.ShapeDtypeStruct((B, hk, nblk * tn), F32),
                   jax.ShapeDtypeStruct((B, hk, nblk * tn), jnp.int32)],
        scratch_shapes=[pltpu.VMEM((wq_t.shape[0], tn), F32)],
        compiler_params=_cparams(2),
    )(S, M, wq_t, keys)


def _sublane_sums(vregs):
    sub = lax.broadcasted_iota(jnp.int32, (SUBLANES, LANES), 0)
    level = list(vregs)
    half = SUBLANES // 2
    while half >= 1:
        lower = (sub % (2 * half)) < half
        nxt = []
        for k in range(len(level) // 2):
            a, b = level[k], level[k + len(level) // 2]
            stay = jnp.where(lower, a, b)
            move = jnp.where(lower, b, a)
            if 2 * half == SUBLANES:
                moved = pltpu.roll(move, half, 0)
            else:
                moved = jnp.where(lower, pltpu.roll(move, SUBLANES - half, 0), pltpu.roll(move, half, 0))
            nxt.append(stay + moved)
        level = nxt
        half //= 2
    return level[0]


def _sublane_transpose(vregs):
    sub = lax.broadcasted_iota(jnp.int32, (SUBLANES, LANES), 0)
    v = list(vregs)
    half = SUBLANES // 2
    while half >= 1:
        clear = (sub & half) == 0
        for i in range(SUBLANES):
            if i & half == 0:
                a, b = v[i], v[i + half]
                v[i] = jnp.where(clear, a, pltpu.roll(b, half, 0))
                v[i + half] = jnp.where(clear, pltpu.roll(a, SUBLANES - half, 0), b)
        half //= 2
    return v


def _peer_gather_kernel(idx_ref, nxt_ref, g_ref, s_ref, m_ref, lg_ref, lb_ref, tab_ref, out_ref, buf, sem, obuf,
                        cbuf):
    i = pl.program_id(0)
    n_steps = pl.num_programs(0)
    tb = s_ref.shape[0]
    hk = g_ref.shape[1]
    gt = PEER_TOK_GROUP
    rows = gt * hk
    n_groups = tb // gt
    pgs = hk // SUBLANES

    def row_copy(e, slot, r, prio):
        pltpu.make_async_copy(tab_ref.at[e], buf.at[slot, r], sem.at[slot]).start(priority=prio)

    def issue(ref, grp, slot):
        per_it = 4 * SUBLANES

        def body(it, carry):
            for k in range(per_it):
                row_copy(ref[grp * rows + it * per_it + k], slot, it * per_it + k, k % 2)
            return carry

        lax.fori_loop(0, rows // per_it, body, 0)

    def issue_token(ref, grp, slot, t, part, n_parts):
        for r in range(t * hk + part * hk // n_parts, t * hk + (part + 1) * hk // n_parts):
            row_copy(ref[grp * rows + r], slot, r, r % 2)

    def wait(slot):
        pltpu.make_async_copy(buf.at[(slot + 1) % n_groups], buf.at[slot], sem.at[slot]).wait()

    @pl.when(i == 0)
    def _():
        for g0 in range(PEER_LOOKAHEAD):
            issue(idx_ref, g0, g0)

    m = m_ref[0]
    s_tok = []
    for t0 in range(0, tb, SUBLANES):
        s_tok += _sublane_transpose([s_ref[t0:t0 + SUBLANES, j * LANES:(j + 1) * LANES] for j in range(SUBLANES)])
    s_rows = jnp.stack(s_tok, axis=0)
    h_rows = s_rows * (1.0 + m[4][None]) + m[3][None]
    gates = g_ref[0]
    for grp in range(n_groups):
        slot = grp
        ahead = grp + PEER_LOOKAHEAD
        ahead_ref, ahead_grp = (idx_ref, ahead) if ahead < n_groups else (nxt_ref, ahead - n_groups)
        wait(slot)

        n_parts = 2 * pgs

        def issue_part(t_issue, part):
            if t_issue is not None:
                issue_token(ahead_ref, ahead_grp, ahead % n_groups, t_issue, part, n_parts)

        def expert_scores(t, t_issue):
            h_t = h_rows[grp * gt + t]
            scs = []
            for q in range(pgs):
                issue_part(t_issue, q)
                r0 = t * hk + q * SUBLANES
                part = _sublane_sums([buf[slot, r0 + k, 0] * h_t for k in range(SUBLANES)])
                scs.append(jnp.sum(part, axis=1, keepdims=True))
            return jnp.concatenate(scs, axis=0)

        def expert_mix(t, sc, t_issue):
            tok = grp * gt + t
            cbuf[tok] = jnp.broadcast_to(gates[:, tok:tok + 1] * _gelu_tanh(sc), (hk, LANES))
            accs = [None] * 4
            for r in range(hk):
                if r % SUBLANES == 0:
                    issue_part(t_issue, pgs + r // SUBLANES)
                term = jnp.broadcast_to(cbuf[tok, r:r + 1, :], (SUBLANES, LANES)) * buf[slot, t * hk + r, 1]
                accs[r % 4] = term if accs[r % 4] is None else accs[r % 4] + term
            obuf[tok] = (accs[0] + accs[1]) + (accs[2] + accs[3])

        sc_prev = expert_scores(0, None)
        for t in range(gt):
            if t + 1 < gt:
                sc_next = expert_scores(t + 1, t)
            else:
                sc_next = None
                for part in range(pgs):
                    issue_part(t, part)
            expert_mix(t, sc_prev, t)
            sc_prev = sc_next
    z = DEEPNORM_ALPHA * s_rows + m[5][None] * obuf[...]
    n_el = z.shape[1] * z.shape[2]
    mu = jnp.sum(jnp.sum(z, axis=2, keepdims=True), axis=1, keepdims=True) / n_el
    zc = z - mu
    var = jnp.sum(jnp.sum(zc * zc, axis=2, keepdims=True), axis=1, keepdims=True) / n_el
    y = zc * lax.rsqrt(var + LN_EPS) * lg_ref[...][None] + lb_ref[...][None]
    for t0 in range(0, tb, SUBLANES):
        tiles = _sublane_transpose([y[t0 + t] for t in range(SUBLANES)])
        for j in range(SUBLANES):
            out_ref[t0:t0 + SUBLANES, j * LANES:(j + 1) * LANES] = tiles[j]

    @pl.when(i == n_steps - 1)
    def _():
        for slot in range(PEER_LOOKAHEAD):
            wait(slot)


def _peer_gather(S, M, gates_t, eidx, table, ln_g, ln_b, ctx_len, skip_ctx):
    B, P, D = S.shape
    tb = PEER_TOK_BLOCK
    n_slots = tb // PEER_TOK_GROUP
    assert PEER_LOOKAHEAD < n_slots
    hk = PEER_HEADS * PEER_TOPK
    j0 = ctx_len // tb if skip_ctx else 0
    per_b = P // tb - j0
    n_steps = B * per_b
    nctx = ctx_len // tb
    sub = D // LANES
    assert sub == SUBLANES
    S2 = S.reshape(B * P, D)
    M3 = M.reshape(B * 2, SUBLANES, sub, LANES)
    row_blk = lambda i: (i // per_b) * (P // tb) + i % per_b + j0
    out = pl.pallas_call(
        _peer_gather_kernel,
        grid=(n_steps,),
        in_specs=[pl.BlockSpec((tb * hk,), lambda i: (i,), memory_space=pltpu.SMEM),
                  pl.BlockSpec((tb * hk,), lambda i: (jnp.minimum(i + 1, n_steps - 1),), memory_space=pltpu.SMEM),
                  pl.BlockSpec((1, hk, tb), lambda i: (i, 0, 0)),
                  pl.BlockSpec((tb, D), lambda i: (row_blk(i), 0)),
                  pl.BlockSpec((1, SUBLANES, sub, LANES),
                               lambda i: ((i // per_b) * 2 + jnp.minimum((i % per_b + j0) // nctx, 1), 0, 0, 0)),
                  pl.BlockSpec((sub, LANES), lambda i: (0, 0)),
                  pl.BlockSpec((sub, LANES), lambda i: (0, 0)),
                  pl.BlockSpec(memory_space=pl.ANY)],
        out_specs=pl.BlockSpec((tb, D), lambda i: (row_blk(i), 0)),
        out_shape=jax.ShapeDtypeStruct((B * P, D), F32),
        scratch_shapes=[pltpu.VMEM((n_slots, PEER_TOK_GROUP * hk, 2, sub, LANES), F32),
                        pltpu.SemaphoreType.DMA((n_slots,)),
                        pltpu.VMEM((tb, sub, LANES), F32),
                        pltpu.VMEM((tb, hk, LANES), F32)],
        input_output_aliases={3: 0},
        compiler_params=_cparams(1),
    )(eidx, eidx, gates_t, S2, M3, ln_g.reshape(sub, LANES), ln_b.reshape(sub, LANES), table)
    return out.reshape(B, P, D)


def _peer_ffn_resid_ln(S, M, wq, keys, u_tab, v_tab, ln_g, ln_b, ctx_len, skip_ctx):
    B, P, D = S.shape
    wq_t = wq.T.astype(BF16)
    gates, eidx = _peer_topk(S, M, wq_t, keys.astype(BF16), ctx_len, skip_ctx)
    hk = gates.shape[1]
    tb = PEER_TOK_BLOCK
    gates_t = jnp.transpose(gates.reshape(B, hk, -1, tb), (0, 2, 1, 3)).reshape(-1, hk, tb)
    eidx_flat = jnp.transpose(eidx, (0, 2, 1)).reshape(-1)
    table = jnp.stack([u_tab, v_tab], axis=1).reshape(-1, 2, D // LANES, LANES)
    return _peer_gather(S, M, gates_t, eidx_flat, table, ln_g, ln_b, ctx_len, skip_ctx)


def _deinterleave_heads(w):
    d_in, n = w.shape
    return w.reshape(d_in, n // HEAD_DIM, HEAD_DIM // 2, 2).transpose(0, 1, 3, 2).reshape(d_in, n)


def _rope_tables(ctx_len, n_lat):
    rows = n_lat // GRID_W
    row = jnp.repeat(jnp.arange(rows, dtype=F32), GRID_W)
    col = jnp.tile(jnp.arange(GRID_W, dtype=F32), rows)
    n_freq = HEAD_DIM // 4
    inv = ROPE_BASE ** (-jnp.arange(n_freq, dtype=F32) / n_freq)
    ang = jnp.concatenate([row[:, None] * inv, col[:, None] * inv], -1)
    cos, sin = jnp.cos(ang), jnp.sin(ang)
    reps = LANES // HEAD_DIM
    cos_t = jnp.tile(jnp.concatenate([cos, cos], -1), (1, reps))
    sin_t = jnp.tile(jnp.concatenate([-sin, sin], -1), (1, reps))
    cos_t = jnp.concatenate([jnp.ones((ctx_len, LANES), F32), cos_t], 0)
    sin_t = jnp.concatenate([jnp.zeros((ctx_len, LANES), F32), sin_t], 0)
    return cos_t, sin_t


def _mixer_layer(i, S, M, cos_t, sin_t, L, last, da_wqkv, da_wo, da_lambda, da_subln, wa_wqkv, wa_wo, wa_sink,
                 ssm_lam_re, ssm_lam_im, ssm_log_step, ssm_b_re, ssm_b_im, ssm_c_re, ssm_c_im, ssm_d, ssm_w_glu,
                 ln_g, ln_b):
    D = S.shape[2]
    kind, j = i % N_MIXERS, i // N_MIXERS
    if kind == 0:
        lam_init = 0.8 - 0.6 * math.exp(-0.3 * i)
        w = da_wqkv[j]
        n_qk = 2 * DA_HEADS * 2 * HEAD_DIM
        w = jnp.concatenate([_deinterleave_heads(w[:, :n_qk]), w[:, n_qk:]], axis=1).astype(BF16)
        qkv = _project(S, M, w, cos_t, sin_t, n_qk, L)
        O = _diff_attention(qkv, da_lambda[j], da_subln[j], lam_init, L, not last)
        return _out_proj_resid_ln(O, da_wo[j].astype(BF16), S, M, ln_g[i, 0], ln_b[i, 0], L, last)
    if kind == 1:
        w = wa_wqkv[j]
        nq = WA_Q_HEADS * HEAD_DIM
        nkv = WA_KV_HEADS * HEAD_DIM
        rep = lambda m: jnp.tile(m.reshape(D, WA_KV_HEADS, 1, HEAD_DIM), (1, 1, WA_REP, 1)).reshape(D, nq)
        w = jnp.concatenate([_deinterleave_heads(w[:, :nq]),
                             rep(_deinterleave_heads(w[:, nq:nq + nkv])),
                             rep(w[:, nq + nkv:])], axis=1).astype(BF16)
        qkv = _project(S, M, w, cos_t, sin_t, 2 * nq, L)
        O = _window_attention(qkv, wa_sink[j], L, not last)
        return _out_proj_resid_ln(O, wa_wo[j].astype(BF16), S, M, ln_g[i, 0], ln_b[i, 0], L, last)
    bmat, cmat, lam_r, lam_i = _s5_params(ssm_lam_re[j], ssm_lam_im[j], ssm_log_step[j], ssm_b_re[j],
                                          ssm_b_im[j], ssm_c_re[j], ssm_c_im[j])
    Mt = jnp.transpose(M[:, :, 0:2, :], (1, 2, 0, 3))
    Y_tm = _s5_scan(jnp.transpose(S, (1, 0, 2)), Mt, bmat, cmat, lam_r, lam_i, L)
    Y = jnp.transpose(Y_tm, (0, 2, 1, 3))
    return _glu_resid_ln(S, Y, ssm_d[j], ssm_w_glu[j].astype(BF16), M, ln_g[i, 0], ln_b[i, 0], L)


def kernel(x, c, ctx, c_ctx, mod_w, mod_b, ln_g, ln_b, peer_wq, peer_keys, peer_u, peer_v, da_wqkv, da_wo, da_lambda, da_subln, wa_wqkv, wa_wo, wa_sink, ssm_lam_re, ssm_lam_im, ssm_log_step, ssm_b_re, ssm_b_im, ssm_c_re, ssm_c_im, ssm_d, ssm_w_glu):
    B, T, D = x.shape
    L = ctx.shape[1]
    depth = mod_w.shape[0]
    cos_t, sin_t = _rope_tables(L, T)
    M_all = _mod_vectors(c, c_ctx, mod_w, mod_b)
    S = jnp.concatenate([ctx, x], axis=1)
    for i in range(depth):
        last = i == depth - 1
        S = _mixer_layer(i, S, M_all[i], cos_t, sin_t, L, last, da_wqkv, da_wo, da_lambda, da_subln, wa_wqkv, wa_wo,
                         wa_sink, ssm_lam_re, ssm_lam_im, ssm_log_step, ssm_b_re, ssm_b_im, ssm_c_re, ssm_c_im,
                         ssm_d, ssm_w_glu, ln_g, ln_b)
        S = _peer_ffn_resid_ln(S, M_all[i], peer_wq[i], peer_keys[i], peer_u[i], peer_v[i], ln_g[i, 1], ln_b[i, 1],
                               L, last)
    return S[:, L:, :]
```

```python
import functools
import math

import jax
import jax.numpy as jnp
from jax import lax
from jax.experimental import pallas as pl
from jax.experimental.pallas import tpu as pltpu

F32 = jnp.float32
BF16 = jnp.bfloat16

DEPTH = 4
N_MIXERS = 3
GRID_W = 64
HEAD_DIM = 64
ROPE_BASE = 10000.0
DA_HEADS = 8
WA_Q_HEADS = 16
WA_KV_HEADS = 4
WA_REP = WA_Q_HEADS // WA_KV_HEADS
WINDOW = 128
SSM_GROUP = 16
SSM_STATE = 64
PEER_HEADS = 8
PEER_NKEYS = 128
PEER_QDIM = 256
PEER_TOPK = 16
LN_EPS = 1e-5
DEEPNORM_ALPHA = (2 * DEPTH) ** 0.25
ATT_SCALE = HEAD_DIM ** -0.5

LANES = 128
SUBLANES = 8
ROW_TILE = 256
WIN_TILE = 128
SCAN_CHUNK = 128
SSM_SLAB_GROUPS = 8
DA_HEADS_PER_STEP = 2
PEER_TOK_BLOCK = 32
PEER_TOK_GROUP = 8
PEER_LOOKAHEAD = 2
VMEM_LIMIT = 48 * 1024 * 1024
NEG_INF = float("-inf")


def _cparams(n_axes):
    return pltpu.CompilerParams(dimension_semantics=("arbitrary",) * n_axes, vmem_limit_bytes=VMEM_LIMIT)


def _gelu_tanh(x):
    return 0.5 * x * (1.0 + jnp.tanh(math.sqrt(2.0 / math.pi) * (x + 0.044715 * (x * x * x))))


def _sigmoid(x):
    return 1.0 / (1.0 + jnp.exp(-x))


def _layer_norm_rows(z, g, b):
    mu = jnp.mean(z, axis=-1, keepdims=True)
    zc = z - mu
    var = jnp.mean(zc * zc, axis=-1, keepdims=True)
    return zc * lax.rsqrt(var + LN_EPS) * g + b


def _modulate(x, m, shift_idx):
    return x * (1.0 + m[shift_idx + 1:shift_idx + 2]) + m[shift_idx:shift_idx + 1]


def _mod_kernel(a_ref, w_ref, b_ref, o_ref):
    a = a_ref[...]
    a = a * _sigmoid(a)
    o_ref[0] = jnp.dot(a, w_ref[0], preferred_element_type=F32, precision=lax.Precision.HIGHEST) + b_ref[0]


def _mod_vectors(c, c_ctx, mod_w, mod_b):
    B, D = c.shape
    depth, _, n6 = mod_w.shape
    rows = -(-(B + 1) // SUBLANES) * SUBLANES
    a = jnp.zeros((rows, D), F32).at[:B].set(c).at[B].set(c_ctx)
    tn = n6 // 4
    out = pl.pallas_call(
        _mod_kernel,
        grid=(depth, n6 // tn),
        in_specs=[pl.BlockSpec((rows, D), lambda i, j: (0, 0)),
                  pl.BlockSpec((1, D, tn), lambda i, j: (i, 0, j)),
                  pl.BlockSpec((1, 1, tn), lambda i, j: (i, 0, j))],
        out_specs=pl.BlockSpec((1, rows, tn), lambda i, j: (i, 0, j)),
        out_shape=jax.ShapeDtypeStruct((depth, rows, n6), F32),
        compiler_params=_cparams(2),
    )(a, mod_w, mod_b.reshape(depth, 1, n6))
    lat = out[:, :B].reshape(depth, B, 6, D)
    ctx = jnp.broadcast_to(out[:, B].reshape(depth, 1, 6, D), (depth, B, 6, D))
    m = jnp.stack([ctx, lat], axis=2)
    return jnp.pad(m, ((0, 0), (0, 0), (0, 0), (0, 2), (0, 0)))


def _mod_spec(D, nctx_blocks, j0):
    return pl.BlockSpec((1, 1, SUBLANES, D), lambda b, j: (b, jnp.minimum((j + j0) // nctx_blocks, 1), 0, 0))


def _proj_kernel(s_ref, m_ref, w_ref, cos_ref, sin_ref, o_ref, *, n_rope, tn):
    xb = _modulate(s_ref[0], m_ref[0, 0], 0).astype(BF16)
    tm = xb.shape[0]
    n_out = w_ref.shape[1]
    cos_t = cos_ref[...]
    sin_t = sin_ref[...]
    lane = lax.broadcasted_iota(jnp.int32, (tm, LANES), 1)
    first_half = (lane % HEAD_DIM) < (HEAD_DIM // 2)
    for c0 in range(0, n_out, tn):
        y = jnp.dot(xb, w_ref[:, c0:c0 + tn], preferred_element_type=F32)
        if c0 < n_rope:
            pieces = []
            for l0 in range(0, tn, LANES):
                yc = y[:, l0:l0 + LANES]
                partner = jnp.where(first_half,
                                    pltpu.roll(yc, LANES - HEAD_DIM // 2, 1),
                                    pltpu.roll(yc, HEAD_DIM // 2, 1))
                pieces.append(yc * cos_t + partner * sin_t)
            y = jnp.concatenate(pieces, axis=1)
        o_ref[0, :, c0:c0 + tn] = y.astype(o_ref.dtype)


def _project(S, M, w, cos_t, sin_t, n_rope, ctx_len):
    B, P, D = S.shape
    n_out = w.shape[1]
    tm = ROW_TILE
    kern = functools.partial(_proj_kernel, n_rope=n_rope, tn=512)
    return pl.pallas_call(
        kern,
        grid=(B, P // tm),
        in_specs=[pl.BlockSpec((1, tm, D), lambda b, j: (b, j, 0)),
                  _mod_spec(D, ctx_len // tm, 0),
                  pl.BlockSpec((D, n_out), lambda b, j: (0, 0)),
                  pl.BlockSpec((tm, LANES), lambda b, j: (j, 0)),
                  pl.BlockSpec((tm, LANES), lambda b, j: (j, 0))],
        out_specs=pl.BlockSpec((1, tm, n_out), lambda b, j: (b, j, 0)),
        out_shape=jax.ShapeDtypeStruct((B, P, n_out), BF16),
        compiler_params=_cparams(2),
    )(S, M, w, cos_t, sin_t)


def _dattn_kernel(lam_ref, g_ref, q_ref, k_ref, v_ref, o_ref, *, lam_init, ctx_len, ctx_queries):
    lp = lam_ref[...]
    lam = (jnp.exp(jnp.sum(lp[0:1] * lp[1:2], axis=-1, keepdims=True))
           - jnp.exp(jnp.sum(lp[2:3] * lp[3:4], axis=-1, keepdims=True)) + lam_init)
    n_heads = q_ref.shape[2] // LANES
    lane = lax.broadcasted_iota(jnp.int32, (q_ref.shape[1], LANES), 1)

    def attend_head(hh, nk):
        cols = slice(hh * LANES, (hh + 1) * LANES)
        q = q_ref[0, :, cols] * ATT_SCALE
        zero = jnp.zeros_like(q)
        k = k_ref[0, 0:nk, cols]
        v = v_ref[0, 0:nk, cols]
        probs = []
        for qm in (jnp.where(lane < HEAD_DIM, q, zero), jnp.where(lane >= HEAD_DIM, q, zero)):
            s = lax.dot_general(qm, k, (((1,), (1,)), ((), ())), preferred_element_type=F32)
            p = jnp.exp(s - jnp.max(s, axis=-1, keepdims=True))
            probs.append((p, 1.0 / jnp.sum(p, axis=-1, keepdims=True)))
        a = probs[0][0] * probs[0][1] - probs[1][0] * (lam * probs[1][1])
        o = jnp.dot(a.astype(BF16), v, preferred_element_type=F32)
        o = o * lax.rsqrt(jnp.mean(o * o, axis=-1, keepdims=True) + LN_EPS) * g_ref[...] * (1.0 - lam_init)
        o_ref[0, :, cols] = o.astype(o_ref.dtype)

    def attend(nk):
        for hh in range(n_heads):
            attend_head(hh, nk)

    if ctx_queries:
        qi = pl.program_id(2)

        @pl.when(qi == 0)
        def _():
            attend(ctx_len)

        @pl.when(qi > 0)
        def _():
            attend(k_ref.shape[1])
    else:
        attend(k_ref.shape[1])


def _diff_attention(qkv, lam_p, subln_g, lam_init, ctx_len, ctx_queries):
    B, P, _ = qkv.shape
    tq = ROW_TILE
    assert ctx_len == tq
    j0 = 0 if ctx_queries else 1
    H = DA_HEADS
    hg = H // DA_HEADS_PER_STEP
    width = DA_HEADS_PER_STEP * LANES
    kern = functools.partial(_dattn_kernel, lam_init=lam_init, ctx_len=ctx_len, ctx_queries=ctx_queries)
    return pl.pallas_call(
        kern,
        grid=(B, hg, P // tq - j0),
        in_specs=[pl.BlockSpec((4, HEAD_DIM), lambda b, h, i: (0, 0)),
                  pl.BlockSpec((1, 2 * HEAD_DIM), lambda b, h, i: (0, 0)),
                  pl.BlockSpec((1, tq, width), lambda b, h, i: (b, i + j0, h)),
                  pl.BlockSpec((1, P, width), lambda b, h, i: (b, 0, hg + h)),
                  pl.BlockSpec((1, P, width), lambda b, h, i: (b, 0, 2 * hg + h))],
        out_specs=pl.BlockSpec((1, tq, width), lambda b, h, i: (b, i + j0, h)),
        out_shape=jax.ShapeDtypeStruct((B, P, H * LANES), BF16),
        compiler_params=_cparams(3),
    )(lam_p, subln_g.reshape(1, -1), qkv, qkv, qkv)


def _wattn_kernel(sink_ref, q_ref, kc_ref, k0_ref, k1_ref, k2_ref, vc_ref, v0_ref, v1_ref, v2_ref, o_ref,
                  *, n_ctx_blocks, n_lat_blocks, j0):
    qb = pl.program_id(1) + j0
    tq = q_ref.shape[1]
    width = WA_REP * HEAD_DIM
    n_kv = q_ref.shape[2] // width
    lane = lax.broadcasted_iota(jnp.int32, (tq, width), 1)
    head_sel = [(lane >= r * HEAD_DIM) & (lane < (r + 1) * HEAD_DIM) for r in range(WA_REP)]
    dims = (((1,), (1,)), ((), ()))

    def queries(g):
        q = q_ref[0, :, g * width:(g + 1) * width] * ATT_SCALE
        zero = jnp.zeros_like(q)
        qm = jnp.concatenate([jnp.where(head_sel[r], q, zero) for r in range(WA_REP)], axis=0)
        sink_col = jnp.concatenate([jnp.full((tq, 1), sink_ref[g * WA_REP + r], F32) for r in range(WA_REP)],
                                   axis=0)
        return qm, sink_col

    def finish(g, o_all, denom):
        o_all = o_all / denom
        o = jnp.zeros((tq, width), F32)
        for r in range(WA_REP):
            o = o + jnp.where(head_sel[r], o_all[r * tq:(r + 1) * tq, :], 0.0)
        o_ref[0, :, g * width:(g + 1) * width] = o.astype(o_ref.dtype)

    @pl.when(qb < n_ctx_blocks)
    def _():
        for g in range(n_kv):
            cols = slice(g * width, (g + 1) * width)
            qm, sink_col = queries(g)
            s_c = lax.dot_general(qm, kc_ref[0, :, cols], dims, preferred_element_type=F32)
            m = jnp.maximum(jnp.max(s_c, axis=-1, keepdims=True), sink_col)
            p_c = jnp.exp(s_c - m)
            denom = jnp.sum(p_c, axis=-1, keepdims=True) + jnp.exp(sink_col - m)
            finish(g, jnp.dot(p_c.astype(BF16), vc_ref[0, :, cols], preferred_element_type=F32), denom)

    @pl.when(qb >= n_ctx_blocks)
    def _():
        lb = qb - n_ctx_blocks
        iq = lax.broadcasted_iota(jnp.int32, (WA_REP * tq, 3 * tq), 0) % tq
        ik = lax.broadcasted_iota(jnp.int32, (WA_REP * tq, 3 * tq), 1)
        lo = jnp.where(lb == 0, tq, 0)
        hi = jnp.where(lb == n_lat_blocks - 1, 2 * tq, 3 * tq)
        band = (ik >= iq) & (ik <= iq + 2 * WINDOW) & (ik >= lo) & (ik < hi)
        for g in range(n_kv):
            cols = slice(g * width, (g + 1) * width)
            qm, sink_col = queries(g)
            kw = jnp.concatenate([k0_ref[0, :, cols], k1_ref[0, :, cols], k2_ref[0, :, cols]], axis=0)
            vw = jnp.concatenate([v0_ref[0, :, cols], v1_ref[0, :, cols], v2_ref[0, :, cols]], axis=0)
            s_c = lax.dot_general(qm, kc_ref[0, :, cols], dims, preferred_element_type=F32)
            s_w = jnp.where(band, lax.dot_general(qm, kw, dims, preferred_element_type=F32), NEG_INF)
            m = jnp.maximum(jnp.maximum(jnp.max(s_c, axis=-1, keepdims=True),
                                        jnp.max(s_w, axis=-1, keepdims=True)), sink_col)
            p_c = jnp.exp(s_c - m)
            p_w = jnp.exp(s_w - m)
            denom = (jnp.sum(p_c, axis=-1, keepdims=True) + jnp.sum(p_w, axis=-1, keepdims=True)
                     + jnp.exp(sink_col - m))
            o_all = (jnp.dot(p_c.astype(BF16), vc_ref[0, :, cols], preferred_element_type=F32)
                     + jnp.dot(p_w.astype(BF16), vw, preferred_element_type=F32))
            finish(g, o_all, denom)


def _window_attention(qkv, sink, ctx_len, ctx_queries):
    B, P, _ = qkv.shape
    tq = WIN_TILE
    assert tq == WINDOW
    nctx = ctx_len // tq
    nlat = P // tq - nctx
    j0 = 0 if ctx_queries else nctx
    full = WA_KV_HEADS * WA_REP * HEAD_DIM

    def lat_spec(off, region):
        return pl.BlockSpec((1, tq, full),
                            lambda b, j: (b, jnp.clip(j + j0 - nctx + off, 0, nlat - 1) + nctx, region))

    row = lambda b, j: (b, j + j0, 0)
    kern = functools.partial(_wattn_kernel, n_ctx_blocks=nctx, n_lat_blocks=nlat, j0=j0)
    return pl.pallas_call(
        kern,
        grid=(B, P // tq - j0),
        in_specs=([pl.BlockSpec(memory_space=pltpu.SMEM), pl.BlockSpec((1, tq, full), row),
                   pl.BlockSpec((1, ctx_len, full), lambda b, j: (b, 0, 1))]
                  + [lat_spec(off, 1) for off in (-1, 0, 1)]
                  + [pl.BlockSpec((1, ctx_len, full), lambda b, j: (b, 0, 2))]
                  + [lat_spec(off, 2) for off in (-1, 0, 1)]),
        out_specs=pl.BlockSpec((1, tq, full), row),
        out_shape=jax.ShapeDtypeStruct((B, P, full), BF16),
        compiler_params=_cparams(2),
    )(sink, qkv, qkv, qkv, qkv, qkv, qkv, qkv, qkv, qkv)


def _resid_ln_kernel(o_ref, w_ref, s_ref, m_ref, g_ref, b_ref, out_ref):
    y = jnp.dot(o_ref[0], w_ref[...], preferred_element_type=F32)
    z = DEEPNORM_ALPHA * s_ref[0] + m_ref[0, 0][2:3] * y
    out_ref[0] = _layer_norm_rows(z, g_ref[...], b_ref[...])


def _out_proj_resid_ln(O, w, S, M, ln_g, ln_b, ctx_len, skip_ctx):
    B, P, D = S.shape
    kdim = O.shape[2]
    tm = ROW_TILE
    j0 = ctx_len // tm if skip_ctx else 0
    row = lambda b, j: (b, j + j0, 0)
    return pl.pallas_call(
        _resid_ln_kernel,
        grid=(B, P // tm - j0),
        in_specs=[pl.BlockSpec((1, tm, kdim), row),
                  pl.BlockSpec((kdim, D), lambda b, j: (0, 0)),
                  pl.BlockSpec((1, tm, D), row),
                  _mod_spec(D, ctx_len // tm, j0),
                  pl.BlockSpec((1, D), lambda b, j: (0, 0)),
                  pl.BlockSpec((1, D), lambda b, j: (0, 0))],
        out_specs=pl.BlockSpec((1, tm, D), row),
        out_shape=jax.ShapeDtypeStruct((B, P, D), F32),
        input_output_aliases={2: 0},
        compiler_params=_cparams(2),
    )(O, w, S, M, ln_g.reshape(1, D), ln_b.reshape(1, D))


def _s5_kernel(s_ref, mt_ref, bm_ref, cm_ref, lr_ref, li_ref, y_ref, xbuf, state):
    d = pl.program_id(0)
    c = pl.program_id(2)
    tc, nb, w_in = s_ref.shape
    half = lr_ref.shape[-1]

    @pl.when(c == 0)
    def _():
        state[...] = jnp.zeros_like(state)

    mt = mt_ref[0]
    u = s_ref[...] * (1.0 + mt[1][None]) + mt[0][None]
    u2 = u.reshape(tc * nb, w_in).astype(BF16)
    xbuf[...] = jnp.dot(u2, bm_ref[0, 0], preferred_element_type=F32)
    lr = jnp.broadcast_to(lr_ref[0, 0], (nb, half))
    li = jnp.broadcast_to(li_ref[0, 0], (nb, half))

    def step(t, carry):
        sr, si = carry
        tt = jnp.where(d == 0, t, tc - 1 - t)
        r0 = pl.multiple_of(tt * nb, nb)
        nr = lr * sr - li * si + xbuf[pl.ds(r0, nb), 0:half]
        ni = lr * si + li * sr + xbuf[pl.ds(r0, nb), half:2 * half]
        xbuf[pl.ds(r0, nb), 0:half] = nr
        xbuf[pl.ds(r0, nb), half:2 * half] = ni
        return nr, ni

    sr, si = lax.fori_loop(0, tc, step, (state[:, 0:half], state[:, half:2 * half]), unroll=4)
    state[:, 0:half] = sr
    state[:, half:2 * half] = si
    y = jnp.dot(xbuf[...].astype(BF16), cm_ref[0, 0], preferred_element_type=F32)
    y_ref[0] = y.reshape(tc, nb, w_in)


def _s5_scan(S_tm, Mt, bmat, cmat, lam_r, lam_i, ctx_len):
    P, B, D = S_tm.shape
    tc = SCAN_CHUNK
    w_in = SSM_SLAB_GROUPS * SSM_GROUP
    n_slab = D // w_in
    half = SSM_SLAB_GROUPS * SSM_STATE
    nch = P // tc
    nctx = ctx_len // tc
    nlat = nch - nctx

    def chunk(d, c):
        q = nch - 1 - c
        back = jnp.where(q < nlat, q + nctx, q - nlat)
        return jnp.where(d == 0, c, back)

    return pl.pallas_call(
        _s5_kernel,
        grid=(2, n_slab, nch),
        in_specs=[pl.BlockSpec((tc, B, w_in), lambda d, s, c: (chunk(d, c), 0, s)),
                  pl.BlockSpec((1, 2, B, w_in), lambda d, s, c: (jnp.minimum(chunk(d, c) // nctx, 1), 0, 0, s)),
                  pl.BlockSpec((1, 1, w_in, 2 * half), lambda d, s, c: (d, s, 0, 0)),
                  pl.BlockSpec((1, 1, 2 * half, w_in), lambda d, s, c: (d, s, 0, 0)),
                  pl.BlockSpec((1, 1, 1, half), lambda d, s, c: (d, s, 0, 0)),
                  pl.BlockSpec((1, 1, 1, half), lambda d, s, c: (d, s, 0, 0))],
        out_specs=pl.BlockSpec((1, tc, B, w_in), lambda d, s, c: (d, chunk(d, c), 0, s)),
        out_shape=jax.ShapeDtypeStruct((2, P, B, D), F32),
        scratch_shapes=[pltpu.VMEM((tc * B, 2 * half), F32), pltpu.VMEM((B, 2 * half), F32)],
        compiler_params=_cparams(3),
    )(S_tm, Mt, bmat, cmat, lam_r, lam_i)


def _s5_params(lam_re, lam_im, log_step, b_re, b_im, c_re, c_im):
    lam = lax.complex(lam_re.astype(F32), lam_im.astype(F32))
    step = jnp.exp(log_step.astype(F32))[..., None]
    lam_bar = jnp.exp(lam * step)
    b_bar = lax.complex(b_re.astype(F32), b_im.astype(F32)) * ((lam_bar - 1) / lam)[..., None]
    G, Pst, Hg = b_bar.shape[1:]
    ng = SSM_SLAB_GROUPS
    ns = G // ng
    eye = jnp.eye(ng, dtype=F32)

    def bdiag_in(x):
        x = x.reshape(2, ns, ng, Pst, Hg)
        return jnp.einsum('rsgph,gk->rsghkp', x, eye).reshape(2, ns, ng * Hg, ng * Pst)

    def bdiag_out(x):
        x = x.reshape(2, ns, ng, Hg, Pst)
        return jnp.einsum('rsghp,gk->rsgpkh', x, eye).reshape(2, ns, ng * Pst, ng * Hg)

    bmat = jnp.concatenate([bdiag_in(jnp.real(b_bar)), bdiag_in(jnp.imag(b_bar))], axis=-1).astype(BF16)
    cmat = jnp.concatenate([bdiag_out(c_re.astype(F32)), bdiag_out(-c_im.astype(F32))], axis=-2).astype(BF16)
    lam_r = jnp.real(lam_bar).reshape(2, ns, 1, ng * Pst)
    lam_i = jnp.imag(lam_bar).reshape(2, ns, 1, ng * Pst)
    return bmat, cmat, lam_r, lam_i


def _glu_ln_kernel(s_ref, y0_ref, y1_ref, d_ref, w_ref, m_ref, g_ref, b_ref, out_ref):
    m = m_ref[0, 0]
    s = s_ref[0]
    y = d_ref[...] * _modulate(s, m, 0) + y0_ref[0, 0] + y1_ref[0, 0]
    z = jnp.dot(_gelu_tanh(y).astype(BF16), w_ref[...], preferred_element_type=F32)
    D = s.shape[1]
    o = z[:, 0:D] * _sigmoid(z[:, D:2 * D])
    out_ref[0] = _layer_norm_rows(DEEPNORM_ALPHA * s + m[2:3] * o, g_ref[...], b_ref[...])


def _glu_resid_ln(S, Y, d_skip, w_glu, M, ln_g, ln_b, ctx_len):
    B, P, D = S.shape
    tm = ROW_TILE
    row = lambda b, j: (b, j, 0)
    return pl.pallas_call(
        _glu_ln_kernel,
        grid=(B, P // tm),
        in_specs=[pl.BlockSpec((1, tm, D), row),
                  pl.BlockSpec((1, 1, tm, D), lambda b, j: (0, b, j, 0)),
                  pl.BlockSpec((1, 1, tm, D), lambda b, j: (1, b, j, 0)),
                  pl.BlockSpec((1, D), lambda b, j: (0, 0)),
                  pl.BlockSpec((D, 2 * D), lambda b, j: (0, 0)),
                  _mod_spec(D, ctx_len // tm, 0),
                  pl.BlockSpec((1, D), lambda b, j: (0, 0)),
                  pl.BlockSpec((1, D), lambda b, j: (0, 0))],
        out_specs=pl.BlockSpec((1, tm, D), row),
        out_shape=jax.ShapeDtypeStruct((B, P, D), F32),
        compiler_params=_cparams(2),
    )(S, Y, Y, d_skip.reshape(1, D), w_glu, M, ln_g.reshape(1, D), ln_b.reshape(1, D))


def _peer_topk_kernel(s_ref, m_ref, wqt_ref, keys_ref, g_out, e_out, qt):
    tn = s_ref.shape[1]
    nk = PEER_NKEYS
    k = PEER_TOPK
    xb = _modulate(s_ref[0], m_ref[0, 0], 3).astype(BF16)
    qt[...] = lax.dot_general(wqt_ref[...], xb, (((1,), (1,)), ((), ())), preferred_element_type=F32)

    def head(h, carry):
        krow = lax.broadcasted_iota(jnp.int32, (nk, tn), 0).astype(F32)
        rank = lax.broadcasted_iota(jnp.int32, (k, tn), 0)
        sub = lax.broadcasted_iota(jnp.int32, (SUBLANES, tn), 0)
        subf = sub.astype(F32)
        tops = []
        for j in range(2):
            qs = qt[pl.ds(pl.multiple_of(h * 2 * nk + j * nk, nk), nk), :].astype(BF16)
            s = jnp.dot(keys_ref[j, h], qs, preferred_element_type=F32)
            top = []
            for it in range(k):
                mx = jnp.max(s, axis=0, keepdims=True)
                ix = jnp.min(jnp.where(s == mx, krow, float(nk)), axis=0, keepdims=True)
                s = jnp.where(krow == ix, NEG_INF, s)
                top.append((mx, ix))
            tops.append(top)
        def stack8(items):
            out = jnp.zeros((SUBLANES, tn), F32)
            for s_i, item in enumerate(items):
                out = jnp.where(sub == s_i, item, out)
            return out

        def pack(first, second, comb):
            a_lo = first[:SUBLANES]
            a_hi = pltpu.roll(stack8(first[SUBLANES:]), 2, 0)
            b_lo = stack8(second[:SUBLANES])
            b_hi = stack8(second[SUBLANES:])
            b_0 = second[0]
            return [comb(a_lo[0], b_lo), comb(a_lo[0], b_hi), comb(a_lo[1], b_lo),
                    jnp.where(sub < 5, comb(a_lo[2], b_lo), comb(a_lo[4], pltpu.roll(b_lo, 5, 0))),
                    jnp.where(sub < 4, comb(a_lo[3], b_lo),
                              jnp.where(sub < 6, comb(a_lo[5], pltpu.roll(b_lo, 4, 0)),
                                        comb(a_lo[6], pltpu.roll(b_lo, 6, 0)))),
                    jnp.where(sub < 2, comb(a_lo[7], b_lo), comb(a_hi, b_0)),
                    comb(a_hi, b_0)]

        vals = pack([t[0] for t in tops[0]], [t[0] for t in tops[1]], lambda a, b: a + b)
        vals[6] = jnp.where(sub < 2, vals[6], NEG_INF)
        cand = jnp.concatenate(vals, axis=0)
        cexp = jnp.concatenate(pack([t[1] for t in tops[0]], [t[1] for t in tops[1]],
                                    lambda a, b: a * float(nk) + b), axis=0)
        pos_const = jnp.concatenate(
            [subf, 8.0 + subf, 16.0 + subf,
             jnp.where(sub < 5, 32.0 + subf, 59.0 + subf),
             jnp.where(sub < 4, 48.0 + subf, jnp.where(sub < 6, 76.0 + subf, 90.0 + subf)),
             jnp.where(sub < 2, 112.0 + subf, 16.0 * (subf + 6.0)),
             16.0 * (subf + 14.0)], axis=0)
        ts = jnp.zeros((k, tn), F32)
        te = jnp.zeros((k, tn), F32)
        best = None
        for it in range(k):
            mx = jnp.max(cand, axis=0, keepdims=True)
            sel = jnp.min(jnp.where(cand == mx, pos_const, float(k * k)), axis=0, keepdims=True)
            hit = pos_const == sel
            ex = jnp.max(jnp.where(hit, cexp, -1.0), axis=0, keepdims=True)
            cand = jnp.where(hit, NEG_INF, cand)
            ts = jnp.where(rank == it, mx, ts)
            te = jnp.where(rank == it, ex, te)
            best = mx if best is None else best
        p = jnp.exp(ts - best)
        gate = p / jnp.sum(p, axis=0, keepdims=True)
        r0 = pl.multiple_of(h * k, k)
        g_out[0, pl.ds(r0, k), :] = gate
        e_out[0, pl.ds(r0, k), :] = te.astype(jnp.int32)
        return carry

    lax.fori_loop(0, PEER_HEADS, head, 0)


def _peer_topk(S, M, wq_t, keys, ctx_len, skip_ctx):
    B, P, D = S.shape
    tn = ROW_TILE
    j0 = ctx_len // tn if skip_ctx else 0
    nblk = P // tn - j0
    hk = PEER_HEADS * PEER_TOPK
    out_spec = pl.BlockSpec((1, hk, tn), lambda b, j: (b, 0, j))
    return pl.pallas_call(
        _peer_topk_kernel,
        grid=(B, nblk),
        in_specs=[pl.BlockSpec((1, tn, D), lambda b, j: (b, j + j0, 0)),
                  _mod_spec(D, ctx_len // tn, j0),
                  pl.BlockSpec(wq_t.shape, lambda b, j: (0, 0)),
                  pl.BlockSpec(keys.shape, lambda b, j: (0, 0, 0, 0))],
        out_specs=[out_spec, out_spec],
        out_shape=[jax.ShapeDtypeStruct((B, hk, nblk * tn), F32),
                   jax.ShapeDtypeStruct((B, hk, nblk * tn), jnp.int32)],
        scratch_shapes=[pltpu.VMEM((wq_t.shape[0], tn), F32)],
        compiler_params=_cparams(2),
    )(S, M, wq_t, keys)


def _sublane_sums(vregs):
    sub = lax.broadcasted_iota(jnp.int32, (SUBLANES, LANES), 0)
    level = list(vregs)
    half = SUBLANES // 2
    while half >= 1:
        lower = (sub % (2 * half)) < half
        nxt = []
        for k in range(len(level) // 2):
            a, b = level[k], level[k + len(level) // 2]
            stay = jnp.where(lower, a, b)
            move = jnp.where(lower, b, a)
            if 2 * half == SUBLANES:
                moved = pltpu.roll(move, half, 0)
            else:
                moved = jnp.where(lower, pltpu.roll(move, SUBLANES - half, 0), pltpu.roll(move, half, 0))
            nxt.append(stay + moved)
        level = nxt
        half //= 2
    return level[0]


def _sublane_transpose(vregs):
    sub = lax.broadcasted_iota(jnp.int32, (SUBLANES, LANES), 0)
    v = list(vregs)
    half = SUBLANES // 2
    while half >= 1:
        clear = (sub & half) == 0
        for i in range(SUBLANES):
            if i & half == 0:
                a, b = v[i], v[i + half]
                v[i] = jnp.where(clear, a, pltpu.roll(b, half, 0))
                v[i + half] = jnp.where(clear, pltpu.roll(a, SUBLANES - half, 0), b)
        half //= 2
    return v


def _peer_gather_kernel(idx_ref, nxt_ref, g_ref, s_ref, m_ref, lg_ref, lb_ref, tab_ref, out_ref, buf, sem, obuf,
                        cbuf):
    i = pl.program_id(0)
    n_steps = pl.num_programs(0)
    tb = s_ref.shape[0]
    hk = g_ref.shape[1]
    gt = PEER_TOK_GROUP
    rows = gt * hk
    n_groups = tb // gt
    pgs = hk // SUBLANES

    def row_copy(e, slot, r, prio):
        pltpu.make_async_copy(tab_ref.at[e], buf.at[slot, r], sem.at[slot]).start(priority=prio)

    def issue(ref, grp, slot):
        per_it = 4 * SUBLANES

        def body(it, carry):
            for k in range(per_it):
                row_copy(ref[grp * rows + it * per_it + k], slot, it * per_it + k, k % 2)
            return carry

        lax.fori_loop(0, rows // per_it, body, 0)

    def issue_token(ref, grp, slot, t, part, n_parts):
        for r in range(t * hk + part * hk // n_parts, t * hk + (part + 1) * hk // n_parts):
            row_copy(ref[grp * rows + r], slot, r, r % 2)

    def wait(slot):
        pltpu.make_async_copy(buf.at[(slot + 1) % n_groups], buf.at[slot], sem.at[slot]).wait()

    @pl.when(i == 0)
    def _():
        for g0 in range(PEER_LOOKAHEAD):
            issue(idx_ref, g0, g0)

    m = m_ref[0]
    s_tok = []
    for t0 in range(0, tb, SUBLANES):
        s_tok += _sublane_transpose([s_ref[t0:t0 + SUBLANES, j * LANES:(j + 1) * LANES] for j in range(SUBLANES)])
    s_rows = jnp.stack(s_tok, axis=0)
    h_rows = s_rows * (1.0 + m[4][None]) + m[3][None]
    gates = g_ref[0]
    for grp in range(n_groups):
        slot = grp
        ahead = grp + PEER_LOOKAHEAD
        ahead_ref, ahead_grp = (idx_ref, ahead) if ahead < n_groups else (nxt_ref, ahead - n_groups)
        wait(slot)

        n_parts = 2 * pgs

        def issue_part(t_issue, part):
            if t_issue is not None:
                issue_token(ahead_ref, ahead_grp, ahead % n_groups, t_issue, part, n_parts)

        def expert_scores(t, t_issue):
            h_t = h_rows[grp * gt + t]
            scs = []
            for q in range(pgs):
                issue_part(t_issue, q)
                r0 = t * hk + q * SUBLANES
                part = _sublane_sums([buf[slot, r0 + k, 0] * h_t for k in range(SUBLANES)])
                scs.append(jnp.sum(part, axis=1, keepdims=True))
            return jnp.concatenate(scs, axis=0)

        def expert_mix(t, sc, t_issue):
            tok = grp * gt + t
            cbuf[tok] = jnp.broadcast_to(gates[:, tok:tok + 1] * _gelu_tanh(sc), (hk, LANES))
            accs = [None] * 4
            for r in range(hk):
                if r % SUBLANES == 0:
                    issue_part(t_issue, pgs + r // SUBLANES)
                term = jnp.broadcast_to(cbuf[tok, r:r + 1, :], (SUBLANES, LANES)) * buf[slot, t * hk + r, 1]
                accs[r % 4] = term if accs[r % 4] is None else accs[r % 4] + term
            obuf[tok] = (accs[0] + accs[1]) + (accs[2] + accs[3])

        sc_prev = expert_scores(0, None)
        for t in range(gt):
            if t + 1 < gt:
                sc_next = expert_scores(t + 1, t)
            else:
                sc_next = None
                for part in range(pgs):
                    issue_part(t, part)
            expert_mix(t, sc_prev, t)
            sc_prev = sc_next
    z = DEEPNORM_ALPHA * s_rows + m[5][None] * obuf[...]
    n_el = z.shape[1] * z.shape[2]
    mu = jnp.sum(jnp.sum(z, axis=2, keepdims=True), axis=1, keepdims=True) / n_el
    zc = z - mu
    var = jnp.sum(jnp.sum(zc * zc, axis=2, keepdims=True), axis=1, keepdims=True) / n_el
    y = zc * lax.rsqrt(var + LN_EPS) * lg_ref[...][None] + lb_ref[...][None]
    for t0 in range(0, tb, SUBLANES):
        tiles = _sublane_transpose([y[t0 + t] for t in range(SUBLANES)])
        for j in range(SUBLANES):
            out_ref[t0:t0 + SUBLANES, j * LANES:(j + 1) * LANES] = tiles[j]

    @pl.when(i == n_steps - 1)
    def _():
        for slot in range(PEER_LOOKAHEAD):
            wait(slot)


def _peer_gather(S, M, gates_t, eidx, table, ln_g, ln_b, ctx_len, skip_ctx):
    B, P, D = S.shape
    tb = PEER_TOK_BLOCK
    n_slots = tb // PEER_TOK_GROUP
    assert PEER_LOOKAHEAD < n_slots
    hk = PEER_HEADS * PEER_TOPK
    j0 = ctx_len // tb if skip_ctx else 0
    per_b = P // tb - j0
    n_steps = B * per_b
    nctx = ctx_len // tb
    sub = D // LANES
    assert sub == SUBLANES
    S2 = S.reshape(B * P, D)
    M3 = M.reshape(B * 2, SUBLANES, sub, LANES)
    row_blk = lambda i: (i // per_b) * (P // tb) + i % per_b + j0
    out = pl.pallas_call(
        _peer_gather_kernel,
        grid=(n_steps,),
        in_specs=[pl.BlockSpec((tb * hk,), lambda i: (i,), memory_space=pltpu.SMEM),
                  pl.BlockSpec((tb * hk,), lambda i: (jnp.minimum(i + 1, n_steps - 1),), memory_space=pltpu.SMEM),
                  pl.BlockSpec((1, hk, tb), lambda i: (i, 0, 0)),
                  pl.BlockSpec((tb, D), lambda i: (row_blk(i), 0)),
                  pl.BlockSpec((1, SUBLANES, sub, LANES),
                               lambda i: ((i // per_b) * 2 + jnp.minimum((i % per_b + j0) // nctx, 1), 0, 0, 0)),
                  pl.BlockSpec((sub, LANES), lambda i: (0, 0)),
                  pl.BlockSpec((sub, LANES), lambda i: (0, 0)),
                  pl.BlockSpec(memory_space=pl.ANY)],
        out_specs=pl.BlockSpec((tb, D), (lambda i: (i, 0)) if skip_ctx else (lambda i: (row_blk(i), 0))),
        out_shape=jax.ShapeDtypeStruct((n_steps * tb if skip_ctx else B * P, D), F32),
        scratch_shapes=[pltpu.VMEM((n_slots, PEER_TOK_GROUP * hk, 2, sub, LANES), F32),
                        pltpu.SemaphoreType.DMA((n_slots,)),
                        pltpu.VMEM((tb, sub, LANES), F32),
                        pltpu.VMEM((tb, hk, LANES), F32)],
        input_output_aliases={} if skip_ctx else {3: 0},
        compiler_params=_cparams(1),
    )(eidx, eidx, gates_t, S2, M3, ln_g.reshape(sub, LANES), ln_b.reshape(sub, LANES), table)
    return out.reshape(B, -1, D)


def _peer_ffn_resid_ln(S, M, wq, keys, u_tab, v_tab, ln_g, ln_b, ctx_len, skip_ctx):
    B, P, D = S.shape
    wq_t = wq.T.astype(BF16)
    gates, eidx = _peer_topk(S, M, wq_t, keys.astype(BF16), ctx_len, skip_ctx)
    hk = gates.shape[1]
    tb = PEER_TOK_BLOCK
    gates_t = jnp.transpose(gates.reshape(B, hk, -1, tb), (0, 2, 1, 3)).reshape(-1, hk, tb)
    eidx_flat = jnp.transpose(eidx, (0, 2, 1)).reshape(-1)
    table = jnp.stack([u_tab, v_tab], axis=1).reshape(-1, 2, D // LANES, LANES)
    return _peer_gather(S, M, gates_t, eidx_flat, table, ln_g, ln_b, ctx_len, skip_ctx)


def _deinterleave_heads(w):
    d_in, n = w.shape
    return w.reshape(d_in, n // HEAD_DIM, HEAD_DIM // 2, 2).transpose(0, 1, 3, 2).reshape(d_in, n)


def _rope_tables(ctx_len, n_lat):
    rows = n_lat // GRID_W
    row = jnp.repeat(jnp.arange(rows, dtype=F32), GRID_W)
    col = jnp.tile(jnp.arange(GRID_W, dtype=F32), rows)
    n_freq = HEAD_DIM // 4
    inv = ROPE_BASE ** (-jnp.arange(n_freq, dtype=F32) / n_freq)
    ang = jnp.concatenate([row[:, None] * inv, col[:, None] * inv], -1)
    cos, sin = jnp.cos(ang), jnp.sin(ang)
    reps = LANES // HEAD_DIM
    cos_t = jnp.tile(jnp.concatenate([cos, cos], -1), (1, reps))
    sin_t = jnp.tile(jnp.concatenate([-sin, sin], -1), (1, reps))
    cos_t = jnp.concatenate([jnp.ones((ctx_len, LANES), F32), cos_t], 0)
    sin_t = jnp.concatenate([jnp.zeros((ctx_len, LANES), F32), sin_t], 0)
    return cos_t, sin_t


def _mixer_layer(i, S, M, cos_t, sin_t, L, last, da_wqkv, da_wo, da_lambda, da_subln, wa_wqkv, wa_wo, wa_sink,
                 ssm_lam_re, ssm_lam_im, ssm_log_step, ssm_b_re, ssm_b_im, ssm_c_re, ssm_c_im, ssm_d, ssm_w_glu,
                 ln_g, ln_b):
    D = S.shape[2]
    kind, j = i % N_MIXERS, i // N_MIXERS
    if kind == 0:
        lam_init = 0.8 - 0.6 * math.exp(-0.3 * i)
        w = da_wqkv[j]
        n_qk = 2 * DA_HEADS * 2 * HEAD_DIM
        w = jnp.concatenate([_deinterleave_heads(w[:, :n_qk]), w[:, n_qk:]], axis=1).astype(BF16)
        qkv = _project(S, M, w, cos_t, sin_t, n_qk, L)
        O = _diff_attention(qkv, da_lambda[j], da_subln[j], lam_init, L, not last)
        return _out_proj_resid_ln(O, da_wo[j].astype(BF16), S, M, ln_g[i, 0], ln_b[i, 0], L, last)
    if kind == 1:
        w = wa_wqkv[j]
        nq = WA_Q_HEADS * HEAD_DIM
        nkv = WA_KV_HEADS * HEAD_DIM
        rep = lambda m: jnp.tile(m.reshape(D, WA_KV_HEADS, 1, HEAD_DIM), (1, 1, WA_REP, 1)).reshape(D, nq)
        w = jnp.concatenate([_deinterleave_heads(w[:, :nq]),
                             rep(_deinterleave_heads(w[:, nq:nq + nkv])),
                             rep(w[:, nq + nkv:])], axis=1).astype(BF16)
        qkv = _project(S, M, w, cos_t, sin_t, 2 * nq, L)
        O = _window_attention(qkv, wa_sink[j], L, not last)
        return _out_proj_resid_ln(O, wa_wo[j].astype(BF16), S, M, ln_g[i, 0], ln_b[i, 0], L, last)
    bmat, cmat, lam_r, lam_i = _s5_params(ssm_lam_re[j], ssm_lam_im[j], ssm_log_step[j], ssm_b_re[j],
                                          ssm_b_im[j], ssm_c_re[j], ssm_c_im[j])
    Mt = jnp.transpose(M[:, :, 0:2, :], (1, 2, 0, 3))
    Y_tm = _s5_scan(jnp.transpose(S, (1, 0, 2)), Mt, bmat, cmat, lam_r, lam_i, L)
    Y = jnp.transpose(Y_tm, (0, 2, 1, 3))
    return _glu_resid_ln(S, Y, ssm_d[j], ssm_w_glu[j].astype(BF16), M, ln_g[i, 0], ln_b[i, 0], L)


def kernel(x, c, ctx, c_ctx, mod_w, mod_b, ln_g, ln_b, peer_wq, peer_keys, peer_u, peer_v, da_wqkv, da_wo, da_lambda, da_subln, wa_wqkv, wa_wo, wa_sink, ssm_lam_re, ssm_lam_im, ssm_log_step, ssm_b_re, ssm_b_im, ssm_c_re, ssm_c_im, ssm_d, ssm_w_glu):
    B, T, D = x.shape
    L = ctx.shape[1]
    depth = mod_w.shape[0]
    cos_t, sin_t = _rope_tables(L, T)
    M_all = _mod_vectors(c, c_ctx, mod_w, mod_b)
    S = jnp.concatenate([ctx, x], axis=1)
    for i in range(depth):
        last = i == depth - 1
        S = _mixer_layer(i, S, M_all[i], cos_t, sin_t, L, last, da_wqkv, da_wo, da_lambda, da_subln, wa_wqkv, wa_wo,
                         wa_sink, ssm_lam_re, ssm_lam_im, ssm_log_step, ssm_b_re, ssm_b_im, ssm_c_re, ssm_c_im,
                         ssm_d, ssm_w_glu, ln_g, ln_b)
        S = _peer_ffn_resid_ln(S, M_all[i], peer_wq[i], peer_keys[i], peer_u[i], peer_v[i], ln_g[i, 1], ln_b[i, 1],
                               L, last)
    return S
```

```python
import functools
import math

import jax
import jax.numpy as jnp
from jax import lax
from jax.experimental import pallas as pl
from jax.experimental.pallas import tpu as pltpu

F32 = jnp.float32
BF16 = jnp.bfloat16

DEPTH = 4
N_MIXERS = 3
GRID_W = 64
HEAD_DIM = 64
ROPE_BASE = 10000.0
DA_HEADS = 8
WA_Q_HEADS = 16
WA_KV_HEADS = 4
WA_REP = WA_Q_HEADS // WA_KV_HEADS
WINDOW = 128
SSM_GROUP = 16
SSM_STATE = 64
PEER_HEADS = 8
PEER_NKEYS = 128
PEER_QDIM = 256
PEER_TOPK = 16
LN_EPS = 1e-5
DEEPNORM_ALPHA = (2 * DEPTH) ** 0.25
ATT_SCALE = HEAD_DIM ** -0.5

LANES = 128
SUBLANES = 8
ROW_TILE = 256
WIN_TILE = 128
SCAN_CHUNK = 128
SSM_SLAB_GROUPS = 8
DA_HEADS_PER_STEP = 2
PEER_TOK_BLOCK = 32
PEER_TOK_GROUP = 8
PEER_LOOKAHEAD = 2
VMEM_LIMIT = 48 * 1024 * 1024
NEG_INF = float("-inf")


def _cparams(n_axes):
    return pltpu.CompilerParams(dimension_semantics=("arbitrary",) * n_axes, vmem_limit_bytes=VMEM_LIMIT)


def _gelu_tanh(x):
    return 0.5 * x * (1.0 + jnp.tanh(math.sqrt(2.0 / math.pi) * (x + 0.044715 * (x * x * x))))


def _sigmoid(x):
    return 1.0 / (1.0 + jnp.exp(-x))


def _layer_norm_rows(z, g, b):
    mu = jnp.mean(z, axis=-1, keepdims=True)
    zc = z - mu
    var = jnp.mean(zc * zc, axis=-1, keepdims=True)
    return zc * lax.rsqrt(var + LN_EPS) * g + b


def _modulate(x, m, shift_idx):
    return x * (1.0 + m[shift_idx + 1:shift_idx + 2]) + m[shift_idx:shift_idx + 1]


def _mod_kernel(a_ref, w_ref, b_ref, o_ref):
    a = a_ref[...]
    a = a * _sigmoid(a)
    o_ref[0] = jnp.dot(a, w_ref[0], preferred_element_type=F32, precision=lax.Precision.HIGHEST) + b_ref[0]


def _mod_vectors(c, c_ctx, mod_w, mod_b):
    B, D = c.shape
    depth, _, n6 = mod_w.shape
    rows = -(-(B + 1) // SUBLANES) * SUBLANES
    a = jnp.zeros((rows, D), F32).at[:B].set(c).at[B].set(c_ctx)
    tn = n6 // 4
    out = pl.pallas_call(
        _mod_kernel,
        grid=(depth, n6 // tn),
        in_specs=[pl.BlockSpec((rows, D), lambda i, j: (0, 0)),
                  pl.BlockSpec((1, D, tn), lambda i, j: (i, 0, j)),
                  pl.BlockSpec((1, 1, tn), lambda i, j: (i, 0, j))],
        out_specs=pl.BlockSpec((1, rows, tn), lambda i, j: (i, 0, j)),
        out_shape=jax.ShapeDtypeStruct((depth, rows, n6), F32),
        compiler_params=_cparams(2),
    )(a, mod_w, mod_b.reshape(depth, 1, n6))
    lat = out[:, :B].reshape(depth, B, 6, D)
    ctx = jnp.broadcast_to(out[:, B].reshape(depth, 1, 6, D), (depth, B, 6, D))
    m = jnp.stack([ctx, lat], axis=2)
    return jnp.pad(m, ((0, 0), (0, 0), (0, 0), (0, 2), (0, 0)))


def _mod_spec(D, nctx_blocks, j0):
    return pl.BlockSpec((1, 1, SUBLANES, D), lambda b, j: (b, jnp.minimum((j + j0) // nctx_blocks, 1), 0, 0))


def _proj_kernel(s_ref, m_ref, w_ref, cos_ref, sin_ref, o_ref, *, n_rope, tn):
    xb = _modulate(s_ref[0], m_ref[0, 0], 0).astype(BF16)
    tm = xb.shape[0]
    n_out = w_ref.shape[1]
    cos_t = cos_ref[...]
    sin_t = sin_ref[...]
    lane = lax.broadcasted_iota(jnp.int32, (tm, LANES), 1)
    first_half = (lane % HEAD_DIM) < (HEAD_DIM // 2)
    for c0 in range(0, n_out, tn):
        y = jnp.dot(xb, w_ref[:, c0:c0 + tn], preferred_element_type=F32)
        if c0 < n_rope:
            pieces = []
            for l0 in range(0, tn, LANES):
                yc = y[:, l0:l0 + LANES]
                partner = jnp.where(first_half,
                                    pltpu.roll(yc, LANES - HEAD_DIM // 2, 1),
                                    pltpu.roll(yc, HEAD_DIM // 2, 1))
                pieces.append(yc * cos_t + partner * sin_t)
            y = jnp.concatenate(pieces, axis=1)
        o_ref[0, :, c0:c0 + tn] = y.astype(o_ref.dtype)


def _project(S, M, w, cos_t, sin_t, n_rope, ctx_len):
    B, P, D = S.shape
    n_out = w.shape[1]
    tm = ROW_TILE
    kern = functools.partial(_proj_kernel, n_rope=n_rope, tn=512)
    return pl.pallas_call(
        kern,
        grid=(B, P // tm),
        in_specs=[pl.BlockSpec((1, tm, D), lambda b, j: (b, j, 0)),
                  _mod_spec(D, ctx_len // tm, 0),
                  pl.BlockSpec((D, n_out), lambda b, j: (0, 0)),
                  pl.BlockSpec((tm, LANES), lambda b, j: (j, 0)),
                  pl.BlockSpec((tm, LANES), lambda b, j: (j, 0))],
        out_specs=pl.BlockSpec((1, tm, n_out), lambda b, j: (b, j, 0)),
        out_shape=jax.ShapeDtypeStruct((B, P, n_out), BF16),
        compiler_params=_cparams(2),
    )(S, M, w, cos_t, sin_t)


def _dattn_kernel(lam_ref, g_ref, q_ref, k_ref, v_ref, o_ref, *, lam_init, ctx_len, ctx_queries):
    lp = lam_ref[...]
    lam = (jnp.exp(jnp.sum(lp[0:1] * lp[1:2], axis=-1, keepdims=True))
           - jnp.exp(jnp.sum(lp[2:3] * lp[3:4], axis=-1, keepdims=True)) + lam_init)
    n_heads = q_ref.shape[2] // LANES
    lane = lax.broadcasted_iota(jnp.int32, (q_ref.shape[1], LANES), 1)

    def attend_head(hh, nk):
        cols = slice(hh * LANES, (hh + 1) * LANES)
        q = q_ref[0, :, cols] * ATT_SCALE
        zero = jnp.zeros_like(q)
        k = k_ref[0, 0:nk, cols]
        v = v_ref[0, 0:nk, cols]
        probs = []
        for qm in (jnp.where(lane < HEAD_DIM, q, zero), jnp.where(lane >= HEAD_DIM, q, zero)):
            s = lax.dot_general(qm, k, (((1,), (1,)), ((), ())), preferred_element_type=F32)
            p = jnp.exp(s - jnp.max(s, axis=-1, keepdims=True))
            probs.append((p, 1.0 / jnp.sum(p, axis=-1, keepdims=True)))
        a = probs[0][0] * probs[0][1] - probs[1][0] * (lam * probs[1][1])
        o = jnp.dot(a.astype(BF16), v, preferred_element_type=F32)
        o = o * lax.rsqrt(jnp.mean(o * o, axis=-1, keepdims=True) + LN_EPS) * g_ref[...] * (1.0 - lam_init)
        o_ref[0, :, cols] = o.astype(o_ref.dtype)

    def attend(nk):
        for hh in range(n_heads):
            attend_head(hh, nk)

    if ctx_queries:
        qi = pl.program_id(2)

        @pl.when(qi == 0)
        def _():
            attend(ctx_len)

        @pl.when(qi > 0)
        def _():
            attend(k_ref.shape[1])
    else:
        attend(k_ref.shape[1])


def _diff_attention(qkv, lam_p, subln_g, lam_init, ctx_len, ctx_queries):
    B, P, _ = qkv.shape
    tq = ROW_TILE
    assert ctx_len == tq
    j0 = 0 if ctx_queries else 1
    H = DA_HEADS
    hg = H // DA_HEADS_PER_STEP
    width = DA_HEADS_PER_STEP * LANES
    kern = functools.partial(_dattn_kernel, lam_init=lam_init, ctx_len=ctx_len, ctx_queries=ctx_queries)
    return pl.pallas_call(
        kern,
        grid=(B, hg, P // tq - j0),
        in_specs=[pl.BlockSpec((4, HEAD_DIM), lambda b, h, i: (0, 0)),
                  pl.BlockSpec((1, 2 * HEAD_DIM), lambda b, h, i: (0, 0)),
                  pl.BlockSpec((1, tq, width), lambda b, h, i: (b, i + j0, h)),
                  pl.BlockSpec((1, P, width), lambda b, h, i: (b, 0, hg + h)),
                  pl.BlockSpec((1, P, width), lambda b, h, i: (b, 0, 2 * hg + h))],
        out_specs=pl.BlockSpec((1, tq, width), lambda b, h, i: (b, i + j0, h)),
        out_shape=jax.ShapeDtypeStruct((B, P, H * LANES), BF16),
        compiler_params=_cparams(3),
    )(lam_p, subln_g.reshape(1, -1), qkv, qkv, qkv)


def _wattn_kernel(sink_ref, q_ref, kc_ref, k0_ref, k1_ref, k2_ref, vc_ref, v0_ref, v1_ref, v2_ref, o_ref,
                  *, n_ctx_blocks, n_lat_blocks, j0):
    qb = pl.program_id(1) + j0
    tq = q_ref.shape[1]
    width = WA_REP * HEAD_DIM
    n_kv = q_ref.shape[2] // width
    lane = lax.broadcasted_iota(jnp.int32, (tq, width), 1)
    head_sel = [(lane >= r * HEAD_DIM) & (lane < (r + 1) * HEAD_DIM) for r in range(WA_REP)]
    dims = (((1,), (1,)), ((), ()))

    def queries(g):
        q = q_ref[0, :, g * width:(g + 1) * width] * ATT_SCALE
        zero = jnp.zeros_like(q)
        qm = jnp.concatenate([jnp.where(head_sel[r], q, zero) for r in range(WA_REP)], axis=0)
        sink_col = jnp.concatenate([jnp.full((tq, 1), sink_ref[g * WA_REP + r], F32) for r in range(WA_REP)],
                                   axis=0)
        return qm, sink_col

    def finish(g, o_all, denom):
        o_all = o_all / denom
        o = jnp.zeros((tq, width), F32)
        for r in range(WA_REP):
            o = o + jnp.where(head_sel[r], o_all[r * tq:(r + 1) * tq, :], 0.0)
        o_ref[0, :, g * width:(g + 1) * width] = o.astype(o_ref.dtype)

    @pl.when(qb < n_ctx_blocks)
    def _():
        for g in range(n_kv):
            cols = slice(g * width, (g + 1) * width)
            qm, sink_col = queries(g)
            s_c = lax.dot_general(qm, kc_ref[0, :, cols], dims, preferred_element_type=F32)
            m = jnp.maximum(jnp.max(s_c, axis=-1, keepdims=True), sink_col)
            p_c = jnp.exp(s_c - m)
            denom = jnp.sum(p_c, axis=-1, keepdims=True) + jnp.exp(sink_col - m)
            finish(g, jnp.dot(p_c.astype(BF16), vc_ref[0, :, cols], preferred_element_type=F32), denom)

    @pl.when(qb >= n_ctx_blocks)
    def _():
        lb = qb - n_ctx_blocks
        iq = lax.broadcasted_iota(jnp.int32, (WA_REP * tq, 3 * tq), 0) % tq
        ik = lax.broadcasted_iota(jnp.int32, (WA_REP * tq, 3 * tq), 1)
        lo = jnp.where(lb == 0, tq, 0)
        hi = jnp.where(lb == n_lat_blocks - 1, 2 * tq, 3 * tq)
        band = (ik >= iq) & (ik <= iq + 2 * WINDOW) & (ik >= lo) & (ik < hi)
        for g in range(n_kv):
            cols = slice(g * width, (g + 1) * width)
            qm, sink_col = queries(g)
            kw = jnp.concatenate([k0_ref[0, :, cols], k1_ref[0, :, cols], k2_ref[0, :, cols]], axis=0)
            vw = jnp.concatenate([v0_ref[0, :, cols], v1_ref[0, :, cols], v2_ref[0, :, cols]], axis=0)
            s_c = lax.dot_general(qm, kc_ref[0, :, cols], dims, preferred_element_type=F32)
            s_w = jnp.where(band, lax.dot_general(qm, kw, dims, preferred_element_type=F32), NEG_INF)
            m = jnp.maximum(jnp.maximum(jnp.max(s_c, axis=-1, keepdims=True),
                                        jnp.max(s_w, axis=-1, keepdims=True)), sink_col)
            p_c = jnp.exp(s_c - m)
            p_w = jnp.exp(s_w - m)
            denom = (jnp.sum(p_c, axis=-1, keepdims=True) + jnp.sum(p_w, axis=-1, keepdims=True)
                     + jnp.exp(sink_col - m))
            o_all = (jnp.dot(p_c.astype(BF16), vc_ref[0, :, cols], preferred_element_type=F32)
                     + jnp.dot(p_w.astype(BF16), vw, preferred_element_type=F32))
            finish(g, o_all, denom)


def _window_attention(qkv, sink, ctx_len, ctx_queries):
    B, P, _ = qkv.shape
    tq = WIN_TILE
    assert tq == WINDOW
    nctx = ctx_len // tq
    nlat = P // tq - nctx
    j0 = 0 if ctx_queries else nctx
    full = WA_KV_HEADS * WA_REP * HEAD_DIM

    def lat_spec(off, region):
        return pl.BlockSpec((1, tq, full),
                            lambda b, j: (b, jnp.clip(j + j0 - nctx + off, 0, nlat - 1) + nctx, region))

    row = lambda b, j: (b, j + j0, 0)
    kern = functools.partial(_wattn_kernel, n_ctx_blocks=nctx, n_lat_blocks=nlat, j0=j0)
    return pl.pallas_call(
        kern,
        grid=(B, P // tq - j0),
        in_specs=([pl.BlockSpec(memory_space=pltpu.SMEM), pl.BlockSpec((1, tq, full), row),
                   pl.BlockSpec((1, ctx_len, full), lambda b, j: (b, 0, 1))]
                  + [lat_spec(off, 1) for off in (-1, 0, 1)]
                  + [pl.BlockSpec((1, ctx_len, full), lambda b, j: (b, 0, 2))]
                  + [lat_spec(off, 2) for off in (-1, 0, 1)]),
        out_specs=pl.BlockSpec((1, tq, full), row),
        out_shape=jax.ShapeDtypeStruct((B, P, full), BF16),
        compiler_params=_cparams(2),
    )(sink, qkv, qkv, qkv, qkv, qkv, qkv, qkv, qkv, qkv)


def _resid_ln_kernel(o_ref, w_ref, s_ref, m_ref, g_ref, b_ref, out_ref):
    y = jnp.dot(o_ref[0], w_ref[...], preferred_element_type=F32)
    z = DEEPNORM_ALPHA * s_ref[0] + m_ref[0, 0][2:3] * y
    out_ref[0] = _layer_norm_rows(z, g_ref[...], b_ref[...])


def _out_proj_resid_ln(O, w, S, M, ln_g, ln_b, ctx_len, skip_ctx):
    B, P, D = S.shape
    kdim = O.shape[2]
    tm = ROW_TILE
    j0 = ctx_len // tm if skip_ctx else 0
    row = lambda b, j: (b, j + j0, 0)
    return pl.pallas_call(
        _resid_ln_kernel,
        grid=(B, P // tm - j0),
        in_specs=[pl.BlockSpec((1, tm, kdim), row),
                  pl.BlockSpec((kdim, D), lambda b, j: (0, 0)),
                  pl.BlockSpec((1, tm, D), row),
                  _mod_spec(D, ctx_len // tm, j0),
                  pl.BlockSpec((1, D), lambda b, j: (0, 0)),
                  pl.BlockSpec((1, D), lambda b, j: (0, 0))],
        out_specs=pl.BlockSpec((1, tm, D), row),
        out_shape=jax.ShapeDtypeStruct((B, P, D), F32),
        input_output_aliases={2: 0},
        compiler_params=_cparams(2),
    )(O, w, S, M, ln_g.reshape(1, D), ln_b.reshape(1, D))


def _s5_kernel(s_ref, mt_ref, bm_ref, cm_ref, lr_ref, li_ref, y_ref, xbuf, state):
    d = pl.program_id(0)
    c = pl.program_id(2)
    tc, nb, w_in = s_ref.shape
    half = lr_ref.shape[-1]

    @pl.when(c == 0)
    def _():
        state[...] = jnp.zeros_like(state)

    mt = mt_ref[0]
    u = s_ref[...] * (1.0 + mt[1][None]) + mt[0][None]
    u2 = u.reshape(tc * nb, w_in).astype(BF16)
    xbuf[...] = jnp.dot(u2, bm_ref[0, 0], preferred_element_type=F32)
    lr = jnp.broadcast_to(lr_ref[0, 0], (nb, half))
    li = jnp.broadcast_to(li_ref[0, 0], (nb, half))

    def step(t, carry):
        sr, si = carry
        tt = jnp.where(d == 0, t, tc - 1 - t)
        r0 = pl.multiple_of(tt * nb, nb)
        nr = lr * sr - li * si + xbuf[pl.ds(r0, nb), 0:half]
        ni = lr * si + li * sr + xbuf[pl.ds(r0, nb), half:2 * half]
        xbuf[pl.ds(r0, nb), 0:half] = nr
        xbuf[pl.ds(r0, nb), half:2 * half] = ni
        return nr, ni

    sr, si = lax.fori_loop(0, tc, step, (state[:, 0:half], state[:, half:2 * half]), unroll=4)
    state[:, 0:half] = sr
    state[:, half:2 * half] = si
    y = jnp.dot(xbuf[...].astype(BF16), cm_ref[0, 0], preferred_element_type=F32)
    y_ref[0] = y.reshape(tc, nb, w_in)


def _s5_scan(S_tm, Mt, bmat, cmat, lam_r, lam_i, ctx_len):
    P, B, D = S_tm.shape
    tc = SCAN_CHUNK
    w_in = SSM_SLAB_GROUPS * SSM_GROUP
    n_slab = D // w_in
    half = SSM_SLAB_GROUPS * SSM_STATE
    nch = P // tc
    nctx = ctx_len // tc
    nlat = nch - nctx

    def chunk(d, c):
        q = nch - 1 - c
        back = jnp.where(q < nlat, q + nctx, q - nlat)
        return jnp.where(d == 0, c, back)

    return pl.pallas_call(
        _s5_kernel,
        grid=(2, n_slab, nch),
        in_specs=[pl.BlockSpec((tc, B, w_in), lambda d, s, c: (chunk(d, c), 0, s)),
                  pl.BlockSpec((1, 2, B, w_in), lambda d, s, c: (jnp.minimum(chunk(d, c) // nctx, 1), 0, 0, s)),
                  pl.BlockSpec((1, 1, w_in, 2 * half), lambda d, s, c: (d, s, 0, 0)),
                  pl.BlockSpec((1, 1, 2 * half, w_in), lambda d, s, c: (d, s, 0, 0)),
                  pl.BlockSpec((1, 1, 1, half), lambda d, s, c: (d, s, 0, 0)),
                  pl.BlockSpec((1, 1, 1, half), lambda d, s, c: (d, s, 0, 0))],
        out_specs=pl.BlockSpec((1, tc, B, w_in), lambda d, s, c: (d, chunk(d, c), 0, s)),
        out_shape=jax.ShapeDtypeStruct((2, P, B, D), F32),
        scratch_shapes=[pltpu.VMEM((tc * B, 2 * half), F32), pltpu.VMEM((B, 2 * half), F32)],
        compiler_params=_cparams(3),
    )(S_tm, Mt, bmat, cmat, lam_r, lam_i)


def _s5_params(lam_re, lam_im, log_step, b_re, b_im, c_re, c_im):
    lam = lax.complex(lam_re.astype(F32), lam_im.astype(F32))
    step = jnp.exp(log_step.astype(F32))[..., None]
    lam_bar = jnp.exp(lam * step)
    b_bar = lax.complex(b_re.astype(F32), b_im.astype(F32)) * ((lam_bar - 1) / lam)[..., None]
    G, Pst, Hg = b_bar.shape[1:]
    ng = SSM_SLAB_GROUPS
    ns = G // ng
    eye = jnp.eye(ng, dtype=F32)

    def bdiag_in(x):
        x = x.reshape(2, ns, ng, Pst, Hg)
        return jnp.einsum('rsgph,gk->rsghkp', x, eye).reshape(2, ns, ng * Hg, ng * Pst)

    def bdiag_out(x):
        x = x.reshape(2, ns, ng, Hg, Pst)
        return jnp.einsum('rsghp,gk->rsgpkh', x, eye).reshape(2, ns, ng * Pst, ng * Hg)

    bmat = jnp.concatenate([bdiag_in(jnp.real(b_bar)), bdiag_in(jnp.imag(b_bar))], axis=-1).astype(BF16)
    cmat = jnp.concatenate([bdiag_out(c_re.astype(F32)), bdiag_out(-c_im.astype(F32))], axis=-2).astype(BF16)
    lam_r = jnp.real(lam_bar).reshape(2, ns, 1, ng * Pst)
    lam_i = jnp.imag(lam_bar).reshape(2, ns, 1, ng * Pst)
    return bmat, cmat, lam_r, lam_i


def _glu_ln_kernel(s_ref, y0_ref, y1_ref, d_ref, w_ref, m_ref, g_ref, b_ref, out_ref):
    m = m_ref[0, 0]
    s = s_ref[0]
    y = d_ref[...] * _modulate(s, m, 0) + y0_ref[0, 0] + y1_ref[0, 0]
    z = jnp.dot(_gelu_tanh(y).astype(BF16), w_ref[...], preferred_element_type=F32)
    D = s.shape[1]
    o = z[:, 0:D] * _sigmoid(z[:, D:2 * D])
    out_ref[0] = _layer_norm_rows(DEEPNORM_ALPHA * s + m[2:3] * o, g_ref[...], b_ref[...])


def _glu_resid_ln(S, Y, d_skip, w_glu, M, ln_g, ln_b, ctx_len):
    B, P, D = S.shape
    tm = ROW_TILE
    row = lambda b, j: (b, j, 0)
    return pl.pallas_call(
        _glu_ln_kernel,
        grid=(B, P // tm),
        in_specs=[pl.BlockSpec((1, tm, D), row),
                  pl.BlockSpec((1, 1, tm, D), lambda b, j: (0, b, j, 0)),
                  pl.BlockSpec((1, 1, tm, D), lambda b, j: (1, b, j, 0)),
                  pl.BlockSpec((1, D), lambda b, j: (0, 0)),
                  pl.BlockSpec((D, 2 * D), lambda b, j: (0, 0)),
                  _mod_spec(D, ctx_len // tm, 0),
                  pl.BlockSpec((1, D), lambda b, j: (0, 0)),
                  pl.BlockSpec((1, D), lambda b, j: (0, 0))],
        out_specs=pl.BlockSpec((1, tm, D), row),
        out_shape=jax.ShapeDtypeStruct((B, P, D), F32),
        compiler_params=_cparams(2),
    )(S, Y, Y, d_skip.reshape(1, D), w_glu, M, ln_g.reshape(1, D), ln_b.reshape(1, D))


def _peer_topk_kernel(s_ref, m_ref, wqt_ref, keys_ref, g_out, e_out, qt):
    tn = s_ref.shape[1]
    nk = PEER_NKEYS
    k = PEER_TOPK
    xb = _modulate(s_ref[0], m_ref[0, 0], 3).astype(BF16)
    qt[...] = lax.dot_general(wqt_ref[...], xb, (((1,), (1,)), ((), ())), preferred_element_type=F32)

    def head(h, carry):
        krow = lax.broadcasted_iota(jnp.int32, (nk, tn), 0).astype(F32)
        rank = lax.broadcasted_iota(jnp.int32, (k, tn), 0)
        sub = lax.broadcasted_iota(jnp.int32, (SUBLANES, tn), 0)
        subf = sub.astype(F32)
        tops = []
        for j in range(2):
            qs = qt[pl.ds(pl.multiple_of(h * 2 * nk + j * nk, nk), nk), :].astype(BF16)
            s = jnp.dot(keys_ref[j, h], qs, preferred_element_type=F32)
            top = []
            for it in range(k):
                mx = jnp.max(s, axis=0, keepdims=True)
                ix = jnp.min(jnp.where(s == mx, krow, float(nk)), axis=0, keepdims=True)
                s = jnp.where(krow == ix, NEG_INF, s)
                top.append((mx, ix))
            tops.append(top)
        def stack8(items):
            out = jnp.zeros((SUBLANES, tn), F32)
            for s_i, item in enumerate(items):
                out = jnp.where(sub == s_i, item, out)
            return out

        def pack(first, second, comb):
            a_lo = first[:SUBLANES]
            a_hi = pltpu.roll(stack8(first[SUBLANES:]), 2, 0)
            b_lo = stack8(second[:SUBLANES])
            b_hi = stack8(second[SUBLANES:])
            b_0 = second[0]
            return [comb(a_lo[0], b_lo), comb(a_lo[0], b_hi), comb(a_lo[1], b_lo),
                    jnp.where(sub < 5, comb(a_lo[2], b_lo), comb(a_lo[4], pltpu.roll(b_lo, 5, 0))),
                    jnp.where(sub < 4, comb(a_lo[3], b_lo),
                              jnp.where(sub < 6, comb(a_lo[5], pltpu.roll(b_lo, 4, 0)),
                                        comb(a_lo[6], pltpu.roll(b_lo, 6, 0)))),
                    jnp.where(sub < 2, comb(a_lo[7], b_lo), comb(a_hi, b_0)),
                    comb(a_hi, b_0)]

        vals = pack([t[0] for t in tops[0]], [t[0] for t in tops[1]], lambda a, b: a + b)
        vals[6] = jnp.where(sub < 2, vals[6], NEG_INF)
        cand = jnp.concatenate(vals, axis=0)
        cexp = jnp.concatenate(pack([t[1] for t in tops[0]], [t[1] for t in tops[1]],
                                    lambda a, b: a * float(nk) + b), axis=0)
        pos_const = jnp.concatenate(
            [subf, 8.0 + subf, 16.0 + subf,
             jnp.where(sub < 5, 32.0 + subf, 59.0 + subf),
             jnp.where(sub < 4, 48.0 + subf, jnp.where(sub < 6, 76.0 + subf, 90.0 + subf)),
             jnp.where(sub < 2, 112.0 + subf, 16.0 * (subf + 6.0)),
             16.0 * (subf + 14.0)], axis=0)
        ts = jnp.zeros((k, tn), F32)
        te = jnp.zeros((k, tn), F32)
        best = None
        for it in range(k):
            mx = jnp.max(cand, axis=0, keepdims=True)
            sel = jnp.min(jnp.where(cand == mx, pos_const, float(k * k)), axis=0, keepdims=True)
            hit = pos_const == sel
            ex = jnp.max(jnp.where(hit, cexp, -1.0), axis=0, keepdims=True)
            cand = jnp.where(hit, NEG_INF, cand)
            ts = jnp.where(rank == it, mx, ts)
            te = jnp.where(rank == it, ex, te)
            best = mx if best is None else best
        p = jnp.exp(ts - best)
        gate = p / jnp.sum(p, axis=0, keepdims=True)
        r0 = pl.multiple_of(h * k, k)
        g_out[0, pl.ds(r0, k), :] = gate
        e_out[0, pl.ds(r0, k), :] = te.astype(jnp.int32)
        return carry

    lax.fori_loop(0, PEER_HEADS, head, 0, unroll=4)


def _peer_topk(S, M, wq_t, keys, ctx_len, skip_ctx):
    B, P, D = S.shape
    tn = ROW_TILE
    j0 = ctx_len // tn if skip_ctx else 0
    nblk = P // tn - j0
    hk = PEER_HEADS * PEER_TOPK
    out_spec = pl.BlockSpec((1, hk, tn), lambda b, j: (b, 0, j))
    return pl.pallas_call(
        _peer_topk_kernel,
        grid=(B, nblk),
        in_specs=[pl.BlockSpec((1, tn, D), lambda b, j: (b, j + j0, 0)),
                  _mod_spec(D, ctx_len // tn, j0),
                  pl.BlockSpec(wq_t.shape, lambda b, j: (0, 0)),
                  pl.BlockSpec(keys.shape, lambda b, j: (0, 0, 0, 0))],
        out_specs=[out_spec, out_spec],
        out_shape=[jax.ShapeDtypeStruct((B, hk, nblk * tn), F32),
                   jax.ShapeDtypeStruct((B, hk, nblk * tn), jnp.int32)],
        scratch_shapes=[pltpu.VMEM((wq_t.shape[0], tn), F32)],
        compiler_params=_cparams(2),
    )(S, M, wq_t, keys)


def _sublane_sums(vregs):
    sub = lax.broadcasted_iota(jnp.int32, (SUBLANES, LANES), 0)
    level = list(vregs)
    half = SUBLANES // 2
    while half >= 1:
        lower = (sub % (2 * half)) < half
        nxt = []
        for k in range(len(level) // 2):
            a, b = level[k], level[k + len(level) // 2]
            stay = jnp.where(lower, a, b)
            move = jnp.where(lower, b, a)
            if 2 * half == SUBLANES:
                moved = pltpu.roll(move, half, 0)
            else:
                moved = jnp.where(lower, pltpu.roll(move, SUBLANES - half, 0), pltpu.roll(move, half, 0))
            nxt.append(stay + moved)
        level = nxt
        half //= 2
    return level[0]


def _sublane_transpose(vregs):
    sub = lax.broadcasted_iota(jnp.int32, (SUBLANES, LANES), 0)
    v = list(vregs)
    half = SUBLANES // 2
    while half >= 1:
        clear = (sub & half) == 0
        for i in range(SUBLANES):
            if i & half == 0:
                a, b = v[i], v[i + half]
                v[i] = jnp.where(clear, a, pltpu.roll(b, half, 0))
                v[i + half] = jnp.where(clear, pltpu.roll(a, SUBLANES - half, 0), b)
        half //= 2
    return v


def _peer_gather_kernel(idx_ref, nxt_ref, g_ref, s_ref, m_ref, lg_ref, lb_ref, tab_ref, out_ref, buf, sem, obuf,
                        cbuf):
    i = pl.program_id(0)
    n_steps = pl.num_programs(0)
    tb = s_ref.shape[0]
    hk = g_ref.shape[1]
    gt = PEER_TOK_GROUP
    rows = gt * hk
    n_groups = tb // gt
    pgs = hk // SUBLANES

    def row_copy(e, slot, r, prio):
        pltpu.make_async_copy(tab_ref.at[e], buf.at[slot, r], sem.at[slot]).start(priority=prio)

    def issue(ref, grp, slot):
        per_it = 4 * SUBLANES

        def body(it, carry):
            for k in range(per_it):
                row_copy(ref[grp * rows + it * per_it + k], slot, it * per_it + k, k % 2)
            return carry

        lax.fori_loop(0, rows // per_it, body, 0)

    def issue_token(ref, grp, slot, t, part, n_parts):
        for r in range(t * hk + part * hk // n_parts, t * hk + (part + 1) * hk // n_parts):
            row_copy(ref[grp * rows + r], slot, r, r % 2)

    def wait(slot):
        pltpu.make_async_copy(buf.at[(slot + 1) % n_groups], buf.at[slot], sem.at[slot]).wait()

    @pl.when(i == 0)
    def _():
        for g0 in range(PEER_LOOKAHEAD):
            issue(idx_ref, g0, g0)

    m = m_ref[0]
    s_tok = []
    for t0 in range(0, tb, SUBLANES):
        s_tok += _sublane_transpose([s_ref[t0:t0 + SUBLANES, j * LANES:(j + 1) * LANES] for j in range(SUBLANES)])
    s_rows = jnp.stack(s_tok, axis=0)
    h_rows = s_rows * (1.0 + m[4][None]) + m[3][None]
    gates = g_ref[0]
    for grp in range(n_groups):
        slot = grp
        ahead = grp + PEER_LOOKAHEAD
        ahead_ref, ahead_grp = (idx_ref, ahead) if ahead < n_groups else (nxt_ref, ahead - n_groups)
        wait(slot)

        n_parts = 2 * pgs

        def issue_part(t_issue, part):
            if t_issue is not None:
                issue_token(ahead_ref, ahead_grp, ahead % n_groups, t_issue, part, n_parts)

        def expert_scores(t, t_issue):
            h_t = h_rows[grp * gt + t]
            scs = []
            for q in range(pgs):
                issue_part(t_issue, q)
                r0 = t * hk + q * SUBLANES
                part = _sublane_sums([buf[slot, r0 + k, 0] * h_t for k in range(SUBLANES)])
                scs.append(jnp.sum(part, axis=1, keepdims=True))
            return jnp.concatenate(scs, axis=0)

        def expert_mix(t, sc, t_issue):
            tok = grp * gt + t
            cbuf[tok] = jnp.broadcast_to(gates[:, tok:tok + 1] * _gelu_tanh(sc), (hk, LANES))
            accs = [None] * 4
            for r in range(hk):
                if r % SUBLANES == 0:
                    issue_part(t_issue, pgs + r // SUBLANES)
                term = jnp.broadcast_to(cbuf[tok, r:r + 1, :], (SUBLANES, LANES)) * buf[slot, t * hk + r, 1]
                accs[r % 4] = term if accs[r % 4] is None else accs[r % 4] + term
            obuf[tok] = (accs[0] + accs[1]) + (accs[2] + accs[3])

        sc_prev = expert_scores(0, None)
        for t in range(gt):
            if t + 1 < gt:
                sc_next = expert_scores(t + 1, t)
            else:
                sc_next = None
                for part in range(pgs):
                    issue_part(t, part)
            expert_mix(t, sc_prev, t)
            sc_prev = sc_next
    z = DEEPNORM_ALPHA * s_rows + m[5][None] * obuf[...]
    n_el = z.shape[1] * z.shape[2]
    mu = jnp.sum(jnp.sum(z, axis=2, keepdims=True), axis=1, keepdims=True) / n_el
    zc = z - mu
    var = jnp.sum(jnp.sum(zc * zc, axis=2, keepdims=True), axis=1, keepdims=True) / n_el
    y = zc * lax.rsqrt(var + LN_EPS) * lg_ref[...][None] + lb_ref[...][None]
    for t0 in range(0, tb, SUBLANES):
        tiles = _sublane_transpose([y[t0 + t] for t in range(SUBLANES)])
        for j in range(SUBLANES):
            out_ref[t0:t0 + SUBLANES, j * LANES:(j + 1) * LANES] = tiles[j]

    @pl.when(i == n_steps - 1)
    def _():
        for slot in range(PEER_LOOKAHEAD):
            wait(slot)


def _peer_gather(S, M, gates_t, eidx, table, ln_g, ln_b, ctx_len, skip_ctx):
    B, P, D = S.shape
    tb = PEER_TOK_BLOCK
    n_slots = tb // PEER_TOK_GROUP
    assert PEER_LOOKAHEAD < n_slots
    hk = PEER_HEADS * PEER_TOPK
    j0 = ctx_len // tb if skip_ctx else 0
    per_b = P // tb - j0
    n_steps = B * per_b
    nctx = ctx_len // tb
    sub = D // LANES
    assert sub == SUBLANES
    S2 = S.reshape(B * P, D)
    M3 = M.reshape(B * 2, SUBLANES, sub, LANES)
    row_blk = lambda i: (i // per_b) * (P // tb) + i % per_b + j0
    out = pl.pallas_call(
        _peer_gather_kernel,
        grid=(n_steps,),
        in_specs=[pl.BlockSpec((tb * hk,), lambda i: (i,), memory_space=pltpu.SMEM),
                  pl.BlockSpec((tb * hk,), lambda i: (jnp.minimum(i + 1, n_steps - 1),), memory_space=pltpu.SMEM),
                  pl.BlockSpec((1, hk, tb), lambda i: (i, 0, 0)),
                  pl.BlockSpec((tb, D), lambda i: (row_blk(i), 0)),
                  pl.BlockSpec((1, SUBLANES, sub, LANES),
                               lambda i: ((i // per_b) * 2 + jnp.minimum((i % per_b + j0) // nctx, 1), 0, 0, 0)),
                  pl.BlockSpec((sub, LANES), lambda i: (0, 0)),
                  pl.BlockSpec((sub, LANES), lambda i: (0, 0)),
                  pl.BlockSpec(memory_space=pl.ANY)],
        out_specs=pl.BlockSpec((tb, D), (lambda i: (i, 0)) if skip_ctx else (lambda i: (row_blk(i), 0))),
        out_shape=jax.ShapeDtypeStruct((n_steps * tb if skip_ctx else B * P, D), F32),
        scratch_shapes=[pltpu.VMEM((n_slots, PEER_TOK_GROUP * hk, 2, sub, LANES), F32),
                        pltpu.SemaphoreType.DMA((n_slots,)),
                        pltpu.VMEM((tb, sub, LANES), F32),
                        pltpu.VMEM((tb, hk, LANES), F32)],
        input_output_aliases={} if skip_ctx else {3: 0},
        compiler_params=_cparams(1),
    )(eidx, eidx, gates_t, S2, M3, ln_g.reshape(sub, LANES), ln_b.reshape(sub, LANES), table)
    return out.reshape(B, -1, D)


def _peer_ffn_resid_ln(S, M, wq, keys, u_tab, v_tab, ln_g, ln_b, ctx_len, skip_ctx):
    B, P, D = S.shape
    wq_t = wq.T.astype(BF16)
    gates, eidx = _peer_topk(S, M, wq_t, keys.astype(BF16), ctx_len, skip_ctx)
    hk = gates.shape[1]
    tb = PEER_TOK_BLOCK
    gates_t = jnp.transpose(gates.reshape(B, hk, -1, tb), (0, 2, 1, 3)).reshape(-1, hk, tb)
    eidx_flat = jnp.transpose(eidx, (0, 2, 1)).reshape(-1)
    table = jnp.stack([u_tab, v_tab], axis=1).reshape(-1, 2, D // LANES, LANES)
    return _peer_gather(S, M, gates_t, eidx_flat, table, ln_g, ln_b, ctx_len, skip_ctx)


def _deinterleave_heads(w):
    d_in, n = w.shape
    return w.reshape(d_in, n // HEAD_DIM, HEAD_DIM // 2, 2).transpose(0, 1, 3, 2).reshape(d_in, n)


def _rope_tables(ctx_len, n_lat):
    rows = n_lat // GRID_W
    row = jnp.repeat(jnp.arange(rows, dtype=F32), GRID_W)
    col = jnp.tile(jnp.arange(GRID_W, dtype=F32), rows)
    n_freq = HEAD_DIM // 4
    inv = ROPE_BASE ** (-jnp.arange(n_freq, dtype=F32) / n_freq)
    ang = jnp.concatenate([row[:, None] * inv, col[:, None] * inv], -1)
    cos, sin = jnp.cos(ang), jnp.sin(ang)
    reps = LANES // HEAD_DIM
    cos_t = jnp.tile(jnp.concatenate([cos, cos], -1), (1, reps))
    sin_t = jnp.tile(jnp.concatenate([-sin, sin], -1), (1, reps))
    cos_t = jnp.concatenate([jnp.ones((ctx_len, LANES), F32), cos_t], 0)
    sin_t = jnp.concatenate([jnp.zeros((ctx_len, LANES), F32), sin_t], 0)
    return cos_t, sin_t


def _mixer_layer(i, S, M, cos_t, sin_t, L, last, da_wqkv, da_wo, da_lambda, da_subln, wa_wqkv, wa_wo, wa_sink,
                 ssm_lam_re, ssm_lam_im, ssm_log_step, ssm_b_re, ssm_b_im, ssm_c_re, ssm_c_im, ssm_d, ssm_w_glu,
                 ln_g, ln_b):
    D = S.shape[2]
    kind, j = i % N_MIXERS, i // N_MIXERS
    if kind == 0:
        lam_init = 0.8 - 0.6 * math.exp(-0.3 * i)
        w = da_wqkv[j]
        n_qk = 2 * DA_HEADS * 2 * HEAD_DIM
        w = jnp.concatenate([_deinterleave_heads(w[:, :n_qk]), w[:, n_qk:]], axis=1).astype(BF16)
        qkv = _project(S, M, w, cos_t, sin_t, n_qk, L)
        O = _diff_attention(qkv, da_lambda[j], da_subln[j], lam_init, L, not last)
        return _out_proj_resid_ln(O, da_wo[j].astype(BF16), S, M, ln_g[i, 0], ln_b[i, 0], L, last)
    if kind == 1:
        w = wa_wqkv[j]
        nq = WA_Q_HEADS * HEAD_DIM
        nkv = WA_KV_HEADS * HEAD_DIM
        rep = lambda m: jnp.tile(m.reshape(D, WA_KV_HEADS, 1, HEAD_DIM), (1, 1, WA_REP, 1)).reshape(D, nq)
        w = jnp.concatenate([_deinterleave_heads(w[:, :nq]),
                             rep(_deinterleave_heads(w[:, nq:nq + nkv])),
                             rep(w[:, nq + nkv:])], axis=1).astype(BF16)
        qkv = _project(S, M, w, cos_t, sin_t, 2 * nq, L)
        O = _window_attention(qkv, wa_sink[j], L, not last)
        return _out_proj_resid_ln(O, wa_wo[j].astype(BF16), S, M, ln_g[i, 0], ln_b[i, 0], L, last)
    bmat, cmat, lam_r, lam_i = _s5_params(ssm_lam_re[j], ssm_lam_im[j], ssm_log_step[j], ssm_b_re[j],
                                          ssm_b_im[j], ssm_c_re[j], ssm_c_im[j])
    Mt = jnp.transpose(M[:, :, 0:2, :], (1, 2, 0, 3))
    Y_tm = _s5_scan(jnp.transpose(S, (1, 0, 2)), Mt, bmat, cmat, lam_r, lam_i, L)
    Y = jnp.transpose(Y_tm, (0, 2, 1, 3))
    return _glu_resid_ln(S, Y, ssm_d[j], ssm_w_glu[j].astype(BF16), M, ln_g[i, 0], ln_b[i, 0], L)


def kernel(x, c, ctx, c_ctx, mod_w, mod_b, ln_g, ln_b, peer_wq, peer_keys, peer_u, peer_v, da_wqkv, da_wo, da_lambda, da_subln, wa_wqkv, wa_wo, wa_sink, ssm_lam_re, ssm_lam_im, ssm_log_step, ssm_b_re, ssm_b_im, ssm_c_re, ssm_c_im, ssm_d, ssm_w_glu):
    B, T, D = x.shape
    L = ctx.shape[1]
    depth = mod_w.shape[0]
    cos_t, sin_t = _rope_tables(L, T)
    M_all = _mod_vectors(c, c_ctx, mod_w, mod_b)
    S = jnp.concatenate([ctx, x], axis=1)
    for i in range(depth):
        last = i == depth - 1
        S = _mixer_layer(i, S, M_all[i], cos_t, sin_t, L, last, da_wqkv, da_wo, da_lambda, da_subln, wa_wqkv, wa_wo,
                         wa_sink, ssm_lam_re, ssm_lam_im, ssm_log_step, ssm_b_re, ssm_b_im, ssm_c_re, ssm_c_im,
                         ssm_d, ssm_w_glu, ln_g, ln_b)
        S = _peer_ffn_resid_ln(S, M_all[i], peer_wq[i], peer_keys[i], peer_u[i], peer_v[i], ln_g[i, 1], ln_b[i, 1],
                               L, last)
    return S
```

```python
import functools
import math

import jax
import jax.numpy as jnp
from jax import lax
from jax.experimental import pallas as pl
from jax.experimental.pallas import tpu as pltpu

F32 = jnp.float32
BF16 = jnp.bfloat16

DEPTH = 4
N_MIXERS = 3
GRID_W = 64
HEAD_DIM = 64
ROPE_BASE = 10000.0
DA_HEADS = 8
WA_Q_HEADS = 16
WA_KV_HEADS = 4
WA_REP = WA_Q_HEADS // WA_KV_HEADS
WINDOW = 128
SSM_GROUP = 16
SSM_STATE = 64
PEER_HEADS = 8
PEER_NKEYS = 128
PEER_QDIM = 256
PEER_TOPK = 16
LN_EPS = 1e-5
DEEPNORM_ALPHA = (2 * DEPTH) ** 0.25
ATT_SCALE = HEAD_DIM ** -0.5

LANES = 128
SUBLANES = 8
ROW_TILE = 256
WIN_TILE = 128
SCAN_CHUNK = 128
SSM_SLAB_GROUPS = 8
DA_HEADS_PER_STEP = 2
PEER_TOK_BLOCK = 32
PEER_TOK_GROUP = 8
PEER_LOOKAHEAD = 2
EXPERT_RELAYOUT_BLOCK = 512
VMEM_LIMIT = 48 * 1024 * 1024
NEG_INF = float("-inf")


def _cparams(n_axes):
    return pltpu.CompilerParams(dimension_semantics=("arbitrary",) * n_axes, vmem_limit_bytes=VMEM_LIMIT)


def _gelu_tanh(x):
    return 0.5 * x * (1.0 + jnp.tanh(math.sqrt(2.0 / math.pi) * (x + 0.044715 * (x * x * x))))


def _sigmoid(x):
    return 1.0 / (1.0 + jnp.exp(-x))


def _layer_norm_rows(z, g, b):
    mu = jnp.mean(z, axis=-1, keepdims=True)
    zc = z - mu
    var = jnp.mean(zc * zc, axis=-1, keepdims=True)
    return zc * lax.rsqrt(var + LN_EPS) * g + b


def _modulate(x, m, shift_idx):
    return x * (1.0 + m[shift_idx + 1:shift_idx + 2]) + m[shift_idx:shift_idx + 1]


def _mod_kernel(a_ref, w_ref, b_ref, o_ref):
    a = a_ref[...]
    a = a * _sigmoid(a)
    o_ref[0] = jnp.dot(a, w_ref[0], preferred_element_type=F32, precision=lax.Precision.HIGHEST) + b_ref[0]


def _mod_vectors(c, c_ctx, mod_w, mod_b):
    B, D = c.shape
    depth, _, n6 = mod_w.shape
    rows = -(-(B + 1) // SUBLANES) * SUBLANES
    a = jnp.zeros((rows, D), F32).at[:B].set(c).at[B].set(c_ctx)
    tn = n6 // 4
    out = pl.pallas_call(
        _mod_kernel,
        grid=(depth, n6 // tn),
        in_specs=[pl.BlockSpec((rows, D), lambda i, j: (0, 0)),
                  pl.BlockSpec((1, D, tn), lambda i, j: (i, 0, j)),
                  pl.BlockSpec((1, 1, tn), lambda i, j: (i, 0, j))],
        out_specs=pl.BlockSpec((1, rows, tn), lambda i, j: (i, 0, j)),
        out_shape=jax.ShapeDtypeStruct((depth, rows, n6), F32),
        compiler_params=_cparams(2),
    )(a, mod_w, mod_b.reshape(depth, 1, n6))
    lat = out[:, :B].reshape(depth, B, 6, D)
    ctx = jnp.broadcast_to(out[:, B].reshape(depth, 1, 6, D), (depth, B, 6, D))
    m = jnp.stack([ctx, lat], axis=2)
    return jnp.pad(m, ((0, 0), (0, 0), (0, 0), (0, 2), (0, 0)))


def _mod_spec(D, nctx_blocks, j0):
    return pl.BlockSpec((1, 1, SUBLANES, D), lambda b, j: (b, jnp.minimum((j + j0) // nctx_blocks, 1), 0, 0))


def _proj_kernel(s_ref, m_ref, w_ref, cos_ref, sin_ref, o_ref, *, n_rope, tn):
    xb = _modulate(s_ref[0], m_ref[0, 0], 0).astype(BF16)
    tm = xb.shape[0]
    n_out = w_ref.shape[1]
    cos_t = cos_ref[...]
    sin_t = sin_ref[...]
    lane = lax.broadcasted_iota(jnp.int32, (tm, LANES), 1)
    first_half = (lane % HEAD_DIM) < (HEAD_DIM // 2)
    for c0 in range(0, n_out, tn):
        y = jnp.dot(xb, w_ref[:, c0:c0 + tn], preferred_element_type=F32)
        if c0 < n_rope:
            pieces = []
            for l0 in range(0, tn, LANES):
                yc = y[:, l0:l0 + LANES]
                partner = jnp.where(first_half,
                                    pltpu.roll(yc, LANES - HEAD_DIM // 2, 1),
                                    pltpu.roll(yc, HEAD_DIM // 2, 1))
                pieces.append(yc * cos_t + partner * sin_t)
            y = jnp.concatenate(pieces, axis=1)
        o_ref[0, :, c0:c0 + tn] = y.astype(o_ref.dtype)


def _project(S, M, w, cos_t, sin_t, n_rope, ctx_len):
    B, P, D = S.shape
    n_out = w.shape[1]
    tm = ROW_TILE
    kern = functools.partial(_proj_kernel, n_rope=n_rope, tn=512)
    return pl.pallas_call(
        kern,
        grid=(B, P // tm),
        in_specs=[pl.BlockSpec((1, tm, D), lambda b, j: (b, j, 0)),
                  _mod_spec(D, ctx_len // tm, 0),
                  pl.BlockSpec((D, n_out), lambda b, j: (0, 0)),
                  pl.BlockSpec((tm, LANES), lambda b, j: (j, 0)),
                  pl.BlockSpec((tm, LANES), lambda b, j: (j, 0))],
        out_specs=pl.BlockSpec((1, tm, n_out), lambda b, j: (b, j, 0)),
        out_shape=jax.ShapeDtypeStruct((B, P, n_out), BF16),
        compiler_params=_cparams(2),
    )(S, M, w, cos_t, sin_t)


def _dattn_kernel(lam_ref, g_ref, q_ref, k_ref, v_ref, o_ref, *, lam_init, ctx_len, ctx_queries):
    lp = lam_ref[...]
    lam = (jnp.exp(jnp.sum(lp[0:1] * lp[1:2], axis=-1, keepdims=True))
           - jnp.exp(jnp.sum(lp[2:3] * lp[3:4], axis=-1, keepdims=True)) + lam_init)
    n_heads = q_ref.shape[2] // LANES
    lane = lax.broadcasted_iota(jnp.int32, (q_ref.shape[1], LANES), 1)

    def attend_head(hh, nk):
        cols = slice(hh * LANES, (hh + 1) * LANES)
        q = q_ref[0, :, cols] * ATT_SCALE
        zero = jnp.zeros_like(q)
        k = k_ref[0, 0:nk, cols]
        v = v_ref[0, 0:nk, cols]
        probs = []
        for qm in (jnp.where(lane < HEAD_DIM, q, zero), jnp.where(lane >= HEAD_DIM, q, zero)):
            s = lax.dot_general(qm, k, (((1,), (1,)), ((), ())), preferred_element_type=F32)
            p = jnp.exp(s - jnp.max(s, axis=-1, keepdims=True))
            probs.append((p, 1.0 / jnp.sum(p, axis=-1, keepdims=True)))
        a = probs[0][0] * probs[0][1] - probs[1][0] * (lam * probs[1][1])
        o = jnp.dot(a.astype(BF16), v, preferred_element_type=F32)
        o = o * lax.rsqrt(jnp.mean(o * o, axis=-1, keepdims=True) + LN_EPS) * g_ref[...] * (1.0 - lam_init)
        o_ref[0, :, cols] = o.astype(o_ref.dtype)

    def attend(nk):
        for hh in range(n_heads):
            attend_head(hh, nk)

    if ctx_queries:
        qi = pl.program_id(2)

        @pl.when(qi == 0)
        def _():
            attend(ctx_len)

        @pl.when(qi > 0)
        def _():
            attend(k_ref.shape[1])
    else:
        attend(k_ref.shape[1])


def _diff_attention(qkv, lam_p, subln_g, lam_init, ctx_len, ctx_queries):
    B, P, _ = qkv.shape
    tq = ROW_TILE
    assert ctx_len == tq
    j0 = 0 if ctx_queries else 1
    H = DA_HEADS
    hg = H // DA_HEADS_PER_STEP
    width = DA_HEADS_PER_STEP * LANES
    kern = functools.partial(_dattn_kernel, lam_init=lam_init, ctx_len=ctx_len, ctx_queries=ctx_queries)
    return pl.pallas_call(
        kern,
        grid=(B, hg, P // tq - j0),
        in_specs=[pl.BlockSpec((4, HEAD_DIM), lambda b, h, i: (0, 0)),
                  pl.BlockSpec((1, 2 * HEAD_DIM), lambda b, h, i: (0, 0)),
                  pl.BlockSpec((1, tq, width), lambda b, h, i: (b, i + j0, h)),
                  pl.BlockSpec((1, P, width), lambda b, h, i: (b, 0, hg + h)),
                  pl.BlockSpec((1, P, width), lambda b, h, i: (b, 0, 2 * hg + h))],
        out_specs=pl.BlockSpec((1, tq, width), lambda b, h, i: (b, i + j0, h)),
        out_shape=jax.ShapeDtypeStruct((B, P, H * LANES), BF16),
        compiler_params=_cparams(3),
    )(lam_p, subln_g.reshape(1, -1), qkv, qkv, qkv)


def _wattn_kernel(sink_ref, q_ref, kc_ref, k0_ref, k1_ref, k2_ref, vc_ref, v0_ref, v1_ref, v2_ref, o_ref,
                  *, n_ctx_blocks, n_lat_blocks, j0):
    qb = pl.program_id(1) + j0
    tq = q_ref.shape[1]
    width = WA_REP * HEAD_DIM
    n_kv = q_ref.shape[2] // width
    lane = lax.broadcasted_iota(jnp.int32, (tq, width), 1)
    head_sel = [(lane >= r * HEAD_DIM) & (lane < (r + 1) * HEAD_DIM) for r in range(WA_REP)]
    dims = (((1,), (1,)), ((), ()))

    def queries(g):
        q = q_ref[0, :, g * width:(g + 1) * width] * ATT_SCALE
        zero = jnp.zeros_like(q)
        qm = jnp.concatenate([jnp.where(head_sel[r], q, zero) for r in range(WA_REP)], axis=0)
        sink_col = jnp.concatenate([jnp.full((tq, 1), sink_ref[g * WA_REP + r], F32) for r in range(WA_REP)],
                                   axis=0)
        return qm, sink_col

    def finish(g, o_all, denom):
        o_all = o_all / denom
        o = jnp.zeros((tq, width), F32)
        for r in range(WA_REP):
            o = o + jnp.where(head_sel[r], o_all[r * tq:(r + 1) * tq, :], 0.0)
        o_ref[0, :, g * width:(g + 1) * width] = o.astype(o_ref.dtype)

    @pl.when(qb < n_ctx_blocks)
    def _():
        for g in range(n_kv):
            cols = slice(g * width, (g + 1) * width)
            qm, sink_col = queries(g)
            s_c = lax.dot_general(qm, kc_ref[0, :, cols], dims, preferred_element_type=F32)
            m = jnp.maximum(jnp.max(s_c, axis=-1, keepdims=True), sink_col)
            p_c = jnp.exp(s_c - m)
            denom = jnp.sum(p_c, axis=-1, keepdims=True) + jnp.exp(sink_col - m)
            finish(g, jnp.dot(p_c.astype(BF16), vc_ref[0, :, cols], preferred_element_type=F32), denom)

    @pl.when(qb >= n_ctx_blocks)
    def _():
        lb = qb - n_ctx_blocks
        iq = lax.broadcasted_iota(jnp.int32, (WA_REP * tq, 3 * tq), 0) % tq
        ik = lax.broadcasted_iota(jnp.int32, (WA_REP * tq, 3 * tq), 1)
        lo = jnp.where(lb == 0, tq, 0)
        hi = jnp.where(lb == n_lat_blocks - 1, 2 * tq, 3 * tq)
        band = (ik >= iq) & (ik <= iq + 2 * WINDOW) & (ik >= lo) & (ik < hi)
        for g in range(n_kv):
            cols = slice(g * width, (g + 1) * width)
            qm, sink_col = queries(g)
            kw = jnp.concatenate([k0_ref[0, :, cols], k1_ref[0, :, cols], k2_ref[0, :, cols]], axis=0)
            vw = jnp.concatenate([v0_ref[0, :, cols], v1_ref[0, :, cols], v2_ref[0, :, cols]], axis=0)
            s_c = lax.dot_general(qm, kc_ref[0, :, cols], dims, preferred_element_type=F32)
            s_w = jnp.where(band, lax.dot_general(qm, kw, dims, preferred_element_type=F32), NEG_INF)
            m = jnp.maximum(jnp.maximum(jnp.max(s_c, axis=-1, keepdims=True),
                                        jnp.max(s_w, axis=-1, keepdims=True)), sink_col)
            p_c = jnp.exp(s_c - m)
            p_w = jnp.exp(s_w - m)
            denom = (jnp.sum(p_c, axis=-1, keepdims=True) + jnp.sum(p_w, axis=-1, keepdims=True)
                     + jnp.exp(sink_col - m))
            o_all = (jnp.dot(p_c.astype(BF16), vc_ref[0, :, cols], preferred_element_type=F32)
                     + jnp.dot(p_w.astype(BF16), vw, preferred_element_type=F32))
            finish(g, o_all, denom)


def _window_attention(qkv, sink, ctx_len, ctx_queries):
    B, P, _ = qkv.shape
    tq = WIN_TILE
    assert tq == WINDOW
    nctx = ctx_len // tq
    nlat = P // tq - nctx
    j0 = 0 if ctx_queries else nctx
    full = WA_KV_HEADS * WA_REP * HEAD_DIM

    def lat_spec(off, region):
        return pl.BlockSpec((1, tq, full),
                            lambda b, j: (b, jnp.clip(j + j0 - nctx + off, 0, nlat - 1) + nctx, region))

    row = lambda b, j: (b, j + j0, 0)
    kern = functools.partial(_wattn_kernel, n_ctx_blocks=nctx, n_lat_blocks=nlat, j0=j0)
    return pl.pallas_call(
        kern,
        grid=(B, P // tq - j0),
        in_specs=([pl.BlockSpec(memory_space=pltpu.SMEM), pl.BlockSpec((1, tq, full), row),
                   pl.BlockSpec((1, ctx_len, full), lambda b, j: (b, 0, 1))]
                  + [lat_spec(off, 1) for off in (-1, 0, 1)]
                  + [pl.BlockSpec((1, ctx_len, full), lambda b, j: (b, 0, 2))]
                  + [lat_spec(off, 2) for off in (-1, 0, 1)]),
        out_specs=pl.BlockSpec((1, tq, full), row),
        out_shape=jax.ShapeDtypeStruct((B, P, full), BF16),
        compiler_params=_cparams(2),
    )(sink, qkv, qkv, qkv, qkv, qkv, qkv, qkv, qkv, qkv)


def _resid_ln_kernel(o_ref, w_ref, s_ref, m_ref, g_ref, b_ref, out_ref):
    y = jnp.dot(o_ref[0], w_ref[...], preferred_element_type=F32)
    z = DEEPNORM_ALPHA * s_ref[0] + m_ref[0, 0][2:3] * y
    out_ref[0] = _layer_norm_rows(z, g_ref[...], b_ref[...])


def _out_proj_resid_ln(O, w, S, M, ln_g, ln_b, ctx_len, skip_ctx):
    B, P, D = S.shape
    kdim = O.shape[2]
    tm = ROW_TILE
    j0 = ctx_len // tm if skip_ctx else 0
    row = lambda b, j: (b, j + j0, 0)
    return pl.pallas_call(
        _resid_ln_kernel,
        grid=(B, P // tm - j0),
        in_specs=[pl.BlockSpec((1, tm, kdim), row),
                  pl.BlockSpec((kdim, D), lambda b, j: (0, 0)),
                  pl.BlockSpec((1, tm, D), row),
                  _mod_spec(D, ctx_len // tm, j0),
                  pl.BlockSpec((1, D), lambda b, j: (0, 0)),
                  pl.BlockSpec((1, D), lambda b, j: (0, 0))],
        out_specs=pl.BlockSpec((1, tm, D), row),
        out_shape=jax.ShapeDtypeStruct((B, P, D), F32),
        input_output_aliases={2: 0},
        compiler_params=_cparams(2),
    )(O, w, S, M, ln_g.reshape(1, D), ln_b.reshape(1, D))


def _s5_kernel(s_ref, mt_ref, bm_ref, cm_ref, lr_ref, li_ref, y_ref, xbuf, state):
    d = pl.program_id(0)
    c = pl.program_id(2)
    tc, nb, w_in = s_ref.shape
    half = lr_ref.shape[-1]

    @pl.when(c == 0)
    def _():
        state[...] = jnp.zeros_like(state)

    mt = mt_ref[0]
    u = s_ref[...] * (1.0 + mt[1][None]) + mt[0][None]
    u2 = u.reshape(tc * nb, w_in).astype(BF16)
    xbuf[...] = jnp.dot(u2, bm_ref[0, 0], preferred_element_type=F32)
    lr = jnp.broadcast_to(lr_ref[0, 0], (nb, half))
    li = jnp.broadcast_to(li_ref[0, 0], (nb, half))

    def step(t, carry):
        sr, si = carry
        tt = jnp.where(d == 0, t, tc - 1 - t)
        r0 = pl.multiple_of(tt * nb, nb)
        nr = lr * sr - li * si + xbuf[pl.ds(r0, nb), 0:half]
        ni = lr * si + li * sr + xbuf[pl.ds(r0, nb), half:2 * half]
        xbuf[pl.ds(r0, nb), 0:half] = nr
        xbuf[pl.ds(r0, nb), half:2 * half] = ni
        return nr, ni

    sr, si = lax.fori_loop(0, tc, step, (state[:, 0:half], state[:, half:2 * half]), unroll=4)
    state[:, 0:half] = sr
    state[:, half:2 * half] = si
    y = jnp.dot(xbuf[...].astype(BF16), cm_ref[0, 0], preferred_element_type=F32)
    y_ref[0] = y.reshape(tc, nb, w_in)


def _s5_scan(S_tm, Mt, bmat, cmat, lam_r, lam_i, ctx_len):
    P, B, D = S_tm.shape
    tc = SCAN_CHUNK
    w_in = SSM_SLAB_GROUPS * SSM_GROUP
    n_slab = D // w_in
    half = SSM_SLAB_GROUPS * SSM_STATE
    nch = P // tc
    nctx = ctx_len // tc
    nlat = nch - nctx

    def chunk(d, c):
        q = nch - 1 - c
        back = jnp.where(q < nlat, q + nctx, q - nlat)
        return jnp.where(d == 0, c, back)

    return pl.pallas_call(
        _s5_kernel,
        grid=(2, n_slab, nch),
        in_specs=[pl.BlockSpec((tc, B, w_in), lambda d, s, c: (chunk(d, c), 0, s)),
                  pl.BlockSpec((1, 2, B, w_in), lambda d, s, c: (jnp.minimum(chunk(d, c) // nctx, 1), 0, 0, s)),
                  pl.BlockSpec((1, 1, w_in, 2 * half), lambda d, s, c: (d, s, 0, 0)),
                  pl.BlockSpec((1, 1, 2 * half, w_in), lambda d, s, c: (d, s, 0, 0)),
                  pl.BlockSpec((1, 1, 1, half), lambda d, s, c: (d, s, 0, 0)),
                  pl.BlockSpec((1, 1, 1, half), lambda d, s, c: (d, s, 0, 0))],
        out_specs=pl.BlockSpec((1, tc, B, w_in), lambda d, s, c: (d, chunk(d, c), 0, s)),
        out_shape=jax.ShapeDtypeStruct((2, P, B, D), F32),
        scratch_shapes=[pltpu.VMEM((tc * B, 2 * half), F32), pltpu.VMEM((B, 2 * half), F32)],
        compiler_params=_cparams(3),
    )(S_tm, Mt, bmat, cmat, lam_r, lam_i)


def _s5_params(lam_re, lam_im, log_step, b_re, b_im, c_re, c_im):
    lam = lax.complex(lam_re.astype(F32), lam_im.astype(F32))
    step = jnp.exp(log_step.astype(F32))[..., None]
    lam_bar = jnp.exp(lam * step)
    b_bar = lax.complex(b_re.astype(F32), b_im.astype(F32)) * ((lam_bar - 1) / lam)[..., None]
    G, Pst, Hg = b_bar.shape[1:]
    ng = SSM_SLAB_GROUPS
    ns = G // ng
    eye = jnp.eye(ng, dtype=F32)

    def bdiag_in(x):
        x = x.reshape(2, ns, ng, Pst, Hg)
        return jnp.einsum('rsgph,gk->rsghkp', x, eye).reshape(2, ns, ng * Hg, ng * Pst)

    def bdiag_out(x):
        x = x.reshape(2, ns, ng, Hg, Pst)
        return jnp.einsum('rsghp,gk->rsgpkh', x, eye).reshape(2, ns, ng * Pst, ng * Hg)

    bmat = jnp.concatenate([bdiag_in(jnp.real(b_bar)), bdiag_in(jnp.imag(b_bar))], axis=-1).astype(BF16)
    cmat = jnp.concatenate([bdiag_out(c_re.astype(F32)), bdiag_out(-c_im.astype(F32))], axis=-2).astype(BF16)
    lam_r = jnp.real(lam_bar).reshape(2, ns, 1, ng * Pst)
    lam_i = jnp.imag(lam_bar).reshape(2, ns, 1, ng * Pst)
    return bmat, cmat, lam_r, lam_i


def _glu_ln_kernel(s_ref, y0_ref, y1_ref, d_ref, w_ref, m_ref, g_ref, b_ref, out_ref):
    m = m_ref[0, 0]
    s = s_ref[0]
    y = d_ref[...] * _modulate(s, m, 0) + y0_ref[0, 0] + y1_ref[0, 0]
    z = jnp.dot(_gelu_tanh(y).astype(BF16), w_ref[...], preferred_element_type=F32)
    D = s.shape[1]
    o = z[:, 0:D] * _sigmoid(z[:, D:2 * D])
    out_ref[0] = _layer_norm_rows(DEEPNORM_ALPHA * s + m[2:3] * o, g_ref[...], b_ref[...])


def _glu_resid_ln(S, Y, d_skip, w_glu, M, ln_g, ln_b, ctx_len):
    B, P, D = S.shape
    tm = ROW_TILE
    row = lambda b, j: (b, j, 0)
    return pl.pallas_call(
        _glu_ln_kernel,
        grid=(B, P // tm),
        in_specs=[pl.BlockSpec((1, tm, D), row),
                  pl.BlockSpec((1, 1, tm, D), lambda b, j: (0, b, j, 0)),
                  pl.BlockSpec((1, 1, tm, D), lambda b, j: (1, b, j, 0)),
                  pl.BlockSpec((1, D), lambda b, j: (0, 0)),
                  pl.BlockSpec((D, 2 * D), lambda b, j: (0, 0)),
                  _mod_spec(D, ctx_len // tm, 0),
                  pl.BlockSpec((1, D), lambda b, j: (0, 0)),
                  pl.BlockSpec((1, D), lambda b, j: (0, 0))],
        out_specs=pl.BlockSpec((1, tm, D), row),
        out_shape=jax.ShapeDtypeStruct((B, P, D), F32),
        compiler_params=_cparams(2),
    )(S, Y, Y, d_skip.reshape(1, D), w_glu, M, ln_g.reshape(1, D), ln_b.reshape(1, D))


def _peer_topk_kernel(s_ref, m_ref, wqt_ref, keys_ref, g_out, e_out, qt):
    tn = s_ref.shape[1]
    nk = PEER_NKEYS
    k = PEER_TOPK
    xb = _modulate(s_ref[0], m_ref[0, 0], 3).astype(BF16)
    qt[...] = lax.dot_general(wqt_ref[...], xb, (((1,), (1,)), ((), ())), preferred_element_type=F32)

    def head(h, carry):
        krow = lax.broadcasted_iota(jnp.int32, (nk, tn), 0).astype(F32)
        rank = lax.broadcasted_iota(jnp.int32, (k, tn), 0)
        sub = lax.broadcasted_iota(jnp.int32, (SUBLANES, tn), 0)
        subf = sub.astype(F32)
        tops = []
        for j in range(2):
            qs = qt[pl.ds(pl.multiple_of(h * 2 * nk + j * nk, nk), nk), :].astype(BF16)
            s = jnp.dot(keys_ref[j, h], qs, preferred_element_type=F32)
            top = []
            for it in range(k):
                mx = jnp.max(s, axis=0, keepdims=True)
                ix = jnp.min(jnp.where(s == mx, krow, float(nk)), axis=0, keepdims=True)
                s = jnp.where(krow == ix, NEG_INF, s)
                top.append((mx, ix))
            tops.append(top)
        def stack8(items):
            out = jnp.zeros((SUBLANES, tn), F32)
            for s_i, item in enumerate(items):
                out = jnp.where(sub == s_i, item, out)
            return out

        def pack(first, second, comb):
            a_lo = first[:SUBLANES]
            a_hi = pltpu.roll(stack8(first[SUBLANES:]), 2, 0)
            b_lo = stack8(second[:SUBLANES])
            b_hi = stack8(second[SUBLANES:])
            b_0 = second[0]
            return [comb(a_lo[0], b_lo), comb(a_lo[0], b_hi), comb(a_lo[1], b_lo),
                    jnp.where(sub < 5, comb(a_lo[2], b_lo), comb(a_lo[4], pltpu.roll(b_lo, 5, 0))),
                    jnp.where(sub < 4, comb(a_lo[3], b_lo),
                              jnp.where(sub < 6, comb(a_lo[5], pltpu.roll(b_lo, 4, 0)),
                                        comb(a_lo[6], pltpu.roll(b_lo, 6, 0)))),
                    jnp.where(sub < 2, comb(a_lo[7], b_lo), comb(a_hi, b_0)),
                    comb(a_hi, b_0)]

        vals = pack([t[0] for t in tops[0]], [t[0] for t in tops[1]], lambda a, b: a + b)
        vals[6] = jnp.where(sub < 2, vals[6], NEG_INF)
        cand = jnp.concatenate(vals, axis=0)
        cexp = jnp.concatenate(pack([t[1] for t in tops[0]], [t[1] for t in tops[1]],
                                    lambda a, b: a * float(nk) + b), axis=0)
        pos_const = jnp.concatenate(
            [subf, 8.0 + subf, 16.0 + subf,
             jnp.where(sub < 5, 32.0 + subf, 59.0 + subf),
             jnp.where(sub < 4, 48.0 + subf, jnp.where(sub < 6, 76.0 + subf, 90.0 + subf)),
             jnp.where(sub < 2, 112.0 + subf, 16.0 * (subf + 6.0)),
             16.0 * (subf + 14.0)], axis=0)
        ts = jnp.zeros((k, tn), F32)
        te = jnp.zeros((k, tn), F32)
        best = None
        for it in range(k):
            mx = jnp.max(cand, axis=0, keepdims=True)
            sel = jnp.min(jnp.where(cand == mx, pos_const, float(k * k)), axis=0, keepdims=True)
            hit = pos_const == sel
            ex = jnp.max(jnp.where(hit, cexp, -1.0), axis=0, keepdims=True)
            cand = jnp.where(hit, NEG_INF, cand)
            ts = jnp.where(rank == it, mx, ts)
            te = jnp.where(rank == it, ex, te)
            best = mx if best is None else best
        p = jnp.exp(ts - best)
        gate = p / jnp.sum(p, axis=0, keepdims=True)
        r0 = pl.multiple_of(h * k, k)
        g_out[0, pl.ds(r0, k), :] = gate
        e_out[0, pl.ds(r0, k), :] = te.astype(jnp.int32)
        return carry

    lax.fori_loop(0, PEER_HEADS, head, 0, unroll=4)


def _peer_topk(S, M, wq_t, keys, ctx_len, skip_ctx):
    B, P, D = S.shape
    tn = ROW_TILE
    j0 = ctx_len // tn if skip_ctx else 0
    nblk = P // tn - j0
    hk = PEER_HEADS * PEER_TOPK
    out_spec = pl.BlockSpec((1, hk, tn), lambda b, j: (b, 0, j))
    return pl.pallas_call(
        _peer_topk_kernel,
        grid=(B, nblk),
        in_specs=[pl.BlockSpec((1, tn, D), lambda b, j: (b, j + j0, 0)),
                  _mod_spec(D, ctx_len // tn, j0),
                  pl.BlockSpec(wq_t.shape, lambda b, j: (0, 0)),
                  pl.BlockSpec(keys.shape, lambda b, j: (0, 0, 0, 0))],
        out_specs=[out_spec, out_spec],
        out_shape=[jax.ShapeDtypeStruct((B, hk, nblk * tn), F32),
                   jax.ShapeDtypeStruct((B, hk, nblk * tn), jnp.int32)],
        scratch_shapes=[pltpu.VMEM((wq_t.shape[0], tn), F32)],
        compiler_params=_cparams(2),
    )(S, M, wq_t, keys)


def _sublane_sums(vregs):
    sub = lax.broadcasted_iota(jnp.int32, (SUBLANES, LANES), 0)
    level = list(vregs)
    half = SUBLANES // 2
    while half >= 1:
        lower = (sub % (2 * half)) < half
        nxt = []
        for k in range(len(level) // 2):
            a, b = level[k], level[k + len(level) // 2]
            stay = jnp.where(lower, a, b)
            move = jnp.where(lower, b, a)
            if 2 * half == SUBLANES:
                moved = pltpu.roll(move, half, 0)
            else:
                moved = jnp.where(lower, pltpu.roll(move, SUBLANES - half, 0), pltpu.roll(move, half, 0))
            nxt.append(stay + moved)
        level = nxt
        half //= 2
    return level[0]


def _sublane_transpose(vregs):
    sub = lax.broadcasted_iota(jnp.int32, (SUBLANES, LANES), 0)
    v = list(vregs)
    half = SUBLANES // 2
    while half >= 1:
        clear = (sub & half) == 0
        for i in range(SUBLANES):
            if i & half == 0:
                a, b = v[i], v[i + half]
                v[i] = jnp.where(clear, a, pltpu.roll(b, half, 0))
                v[i + half] = jnp.where(clear, pltpu.roll(a, SUBLANES - half, 0), b)
        half //= 2
    return v


def _peer_gather_kernel(idx_ref, nxt_ref, g_ref, s_ref, m_ref, lg_ref, lb_ref, tab_ref, out_ref, buf, sem, obuf,
                        cbuf):
    i = pl.program_id(0)
    n_steps = pl.num_programs(0)
    tb = s_ref.shape[0]
    hk = g_ref.shape[1]
    gt = PEER_TOK_GROUP
    rows = gt * hk
    n_groups = tb // gt
    pgs = hk // SUBLANES

    def row_copy(e, slot, r, prio):
        pltpu.make_async_copy(tab_ref.at[e], buf.at[slot, r], sem.at[slot]).start(priority=prio)

    def issue(ref, grp, slot):
        per_it = 4 * SUBLANES

        def body(it, carry):
            for k in range(per_it):
                row_copy(ref[grp * rows + it * per_it + k], slot, it * per_it + k, k % 2)
            return carry

        lax.fori_loop(0, rows // per_it, body, 0)

    def issue_token(ref, grp, slot, t, part, n_parts):
        for r in range(t * hk + part * hk // n_parts, t * hk + (part + 1) * hk // n_parts):
            row_copy(ref[grp * rows + r], slot, r, r % 2)

    def wait(slot):
        pltpu.make_async_copy(buf.at[(slot + 1) % n_groups], buf.at[slot], sem.at[slot]).wait()

    @pl.when(i == 0)
    def _():
        for g0 in range(PEER_LOOKAHEAD):
            issue(idx_ref, g0, g0)

    m = m_ref[0]
    s_tok = []
    for t0 in range(0, tb, SUBLANES):
        s_tok += _sublane_transpose([s_ref[t0:t0 + SUBLANES, j * LANES:(j + 1) * LANES] for j in range(SUBLANES)])
    s_rows = jnp.stack(s_tok, axis=0)
    h_rows = s_rows * (1.0 + m[4][None]) + m[3][None]
    gates = g_ref[0]
    for grp in range(n_groups):
        slot = grp
        ahead = grp + PEER_LOOKAHEAD
        ahead_ref, ahead_grp = (idx_ref, ahead) if ahead < n_groups else (nxt_ref, ahead - n_groups)
        wait(slot)

        n_parts = 2 * pgs

        def issue_part(t_issue, part):
            if t_issue is not None:
                issue_token(ahead_ref, ahead_grp, ahead % n_groups, t_issue, part, n_parts)

        def expert_scores(t, t_issue):
            h_t = h_rows[grp * gt + t]
            scs = []
            for q in range(pgs):
                issue_part(t_issue, q)
                r0 = t * hk + q * SUBLANES
                part = _sublane_sums([buf[slot, r0 + k, 0] * h_t for k in range(SUBLANES)])
                scs.append(jnp.sum(part, axis=1, keepdims=True))
            return jnp.concatenate(scs, axis=0)

        def expert_mix(t, sc, t_issue):
            tok = grp * gt + t
            cbuf[tok] = jnp.broadcast_to(gates[:, tok:tok + 1] * _gelu_tanh(sc), (hk, LANES))
            accs = [None] * 4
            for r in range(hk):
                if r % SUBLANES == 0:
                    issue_part(t_issue, pgs + r // SUBLANES)
                term = jnp.broadcast_to(cbuf[tok, r:r + 1, :], (SUBLANES, LANES)) * buf[slot, t * hk + r, 1]
                accs[r % 4] = term if accs[r % 4] is None else accs[r % 4] + term
            obuf[tok] = (accs[0] + accs[1]) + (accs[2] + accs[3])

        sc_prev = expert_scores(0, None)
        for t in range(gt):
            if t + 1 < gt:
                sc_next = expert_scores(t + 1, t)
            else:
                sc_next = None
                for part in range(pgs):
                    issue_part(t, part)
            expert_mix(t, sc_prev, t)
            sc_prev = sc_next
    z = DEEPNORM_ALPHA * s_rows + m[5][None] * obuf[...]
    n_el = z.shape[1] * z.shape[2]
    mu = jnp.sum(jnp.sum(z, axis=2, keepdims=True), axis=1, keepdims=True) / n_el
    zc = z - mu
    var = jnp.sum(jnp.sum(zc * zc, axis=2, keepdims=True), axis=1, keepdims=True) / n_el
    y = zc * lax.rsqrt(var + LN_EPS) * lg_ref[...][None] + lb_ref[...][None]
    for t0 in range(0, tb, SUBLANES):
        tiles = _sublane_transpose([y[t0 + t] for t in range(SUBLANES)])
        for j in range(SUBLANES):
            out_ref[t0:t0 + SUBLANES, j * LANES:(j + 1) * LANES] = tiles[j]

    @pl.when(i == n_steps - 1)
    def _():
        for slot in range(PEER_LOOKAHEAD):
            wait(slot)


def _peer_gather(S, M, gates_t, eidx, table, ln_g, ln_b, ctx_len, skip_ctx):
    B, P, D = S.shape
    tb = PEER_TOK_BLOCK
    n_slots = tb // PEER_TOK_GROUP
    assert PEER_LOOKAHEAD < n_slots
    hk = PEER_HEADS * PEER_TOPK
    j0 = ctx_len // tb if skip_ctx else 0
    per_b = P // tb - j0
    n_steps = B * per_b
    nctx = ctx_len // tb
    sub = D // LANES
    assert sub == SUBLANES
    S2 = S.reshape(B * P, D)
    M3 = M.reshape(B * 2, SUBLANES, sub, LANES)
    row_blk = lambda i: (i // per_b) * (P // tb) + i % per_b + j0
    out = pl.pallas_call(
        _peer_gather_kernel,
        grid=(n_steps,),
        in_specs=[pl.BlockSpec((tb * hk,), lambda i: (i,), memory_space=pltpu.SMEM),
                  pl.BlockSpec((tb * hk,), lambda i: (jnp.minimum(i + 1, n_steps - 1),), memory_space=pltpu.SMEM),
                  pl.BlockSpec((1, hk, tb), lambda i: (i, 0, 0)),
                  pl.BlockSpec((tb, D), lambda i: (row_blk(i), 0)),
                  pl.BlockSpec((1, SUBLANES, sub, LANES),
                               lambda i: ((i // per_b) * 2 + jnp.minimum((i % per_b + j0) // nctx, 1), 0, 0, 0)),
                  pl.BlockSpec((sub, LANES), lambda i: (0, 0)),
                  pl.BlockSpec((sub, LANES), lambda i: (0, 0)),
                  pl.BlockSpec(memory_space=pl.ANY)],
        out_specs=pl.BlockSpec((tb, D), (lambda i: (i, 0)) if skip_ctx else (lambda i: (row_blk(i), 0))),
        out_shape=jax.ShapeDtypeStruct((n_steps * tb if skip_ctx else B * P, D), F32),
        scratch_shapes=[pltpu.VMEM((n_slots, PEER_TOK_GROUP * hk, 2, sub, LANES), F32),
                        pltpu.SemaphoreType.DMA((n_slots,)),
                        pltpu.VMEM((tb, sub, LANES), F32),
                        pltpu.VMEM((tb, hk, LANES), F32)],
        input_output_aliases={} if skip_ctx else {3: 0},
        compiler_params=_cparams(1),
    )(eidx, eidx, gates_t, S2, M3, ln_g.reshape(sub, LANES), ln_b.reshape(sub, LANES), table)
    return out.reshape(B, -1, D)


def _peer_ffn_resid_ln(S, M, wq, keys, u_tab, v_tab, ln_g, ln_b, ctx_len, skip_ctx):
    B, P, D = S.shape
    wq_t = wq.T.astype(BF16)
    gates, eidx = _peer_topk(S, M, wq_t, keys.astype(BF16), ctx_len, skip_ctx)
    hk = gates.shape[1]
    tb = PEER_TOK_BLOCK
    gates_t = jnp.transpose(gates.reshape(B, hk, -1, tb), (0, 2, 1, 3)).reshape(-1, hk, tb)
    eidx_flat = jnp.transpose(eidx, (0, 2, 1)).reshape(-1)
    table = _expert_table(u_tab, v_tab)
    return _peer_gather(S, M, gates_t, eidx_flat, table, ln_g, ln_b, ctx_len, skip_ctx)


def _expert_table_kernel(u_ref, v_ref, o_ref):
    n_groups = u_ref.shape[0] // SUBLANES

    def group(g, carry):
        r0 = pl.multiple_of(g * SUBLANES, SUBLANES)
        for half, ref in enumerate((u_ref, v_ref)):
            rows = _sublane_transpose([ref[pl.ds(r0, SUBLANES), j * LANES:(j + 1) * LANES] for j in range(SUBLANES)])
            for k in range(SUBLANES):
                o_ref[r0 + k, half] = rows[k]
        return carry

    lax.fori_loop(0, n_groups, group, 0)


def _expert_table(u_tab, v_tab):
    E, D = u_tab.shape
    assert D == SUBLANES * LANES
    te = EXPERT_RELAYOUT_BLOCK
    return pl.pallas_call(
        _expert_table_kernel,
        grid=(E // te,),
        in_specs=[pl.BlockSpec((te, D), lambda i: (i, 0)), pl.BlockSpec((te, D), lambda i: (i, 0))],
        out_specs=pl.BlockSpec((te, 2, SUBLANES, LANES), lambda i: (i, 0, 0, 0)),
        out_shape=jax.ShapeDtypeStruct((E, 2, SUBLANES, LANES), u_tab.dtype),
        compiler_params=_cparams(1),
    )(u_tab, v_tab)


def _deinterleave_heads(w):
    d_in, n = w.shape
    return w.reshape(d_in, n // HEAD_DIM, HEAD_DIM // 2, 2).transpose(0, 1, 3, 2).reshape(d_in, n)


def _rope_tables(ctx_len, n_lat):
    rows = n_lat // GRID_W
    row = jnp.repeat(jnp.arange(rows, dtype=F32), GRID_W)
    col = jnp.tile(jnp.arange(GRID_W, dtype=F32), rows)
    n_freq = HEAD_DIM // 4
    inv = ROPE_BASE ** (-jnp.arange(n_freq, dtype=F32) / n_freq)
    ang = jnp.concatenate([row[:, None] * inv, col[:, None] * inv], -1)
    cos, sin = jnp.cos(ang), jnp.sin(ang)
    reps = LANES // HEAD_DIM
    cos_t = jnp.tile(jnp.concatenate([cos, cos], -1), (1, reps))
    sin_t = jnp.tile(jnp.concatenate([-sin, sin], -1), (1, reps))
    cos_t = jnp.concatenate([jnp.ones((ctx_len, LANES), F32), cos_t], 0)
    sin_t = jnp.concatenate([jnp.zeros((ctx_len, LANES), F32), sin_t], 0)
    return cos_t, sin_t


def _mixer_layer(i, S, M, cos_t, sin_t, L, last, da_wqkv, da_wo, da_lambda, da_subln, wa_wqkv, wa_wo, wa_sink,
                 ssm_lam_re, ssm_lam_im, ssm_log_step, ssm_b_re, ssm_b_im, ssm_c_re, ssm_c_im, ssm_d, ssm_w_glu,
                 ln_g, ln_b):
    D = S.shape[2]
    kind, j = i % N_MIXERS, i // N_MIXERS
    if kind == 0:
        lam_init = 0.8 - 0.6 * math.exp(-0.3 * i)
        w = da_wqkv[j]
        n_qk = 2 * DA_HEADS * 2 * HEAD_DIM
        w = jnp.concatenate([_deinterleave_heads(w[:, :n_qk]), w[:, n_qk:]], axis=1).astype(BF16)
        qkv = _project(S, M, w, cos_t, sin_t, n_qk, L)
        O = _diff_attention(qkv, da_lambda[j], da_subln[j], lam_init, L, not last)
        return _out_proj_resid_ln(O, da_wo[j].astype(BF16), S, M, ln_g[i, 0], ln_b[i, 0], L, last)
    if kind == 1:
        w = wa_wqkv[j]
        nq = WA_Q_HEADS * HEAD_DIM
        nkv = WA_KV_HEADS * HEAD_DIM
        rep = lambda m: jnp.tile(m.reshape(D, WA_KV_HEADS, 1, HEAD_DIM), (1, 1, WA_REP, 1)).reshape(D, nq)
        w = jnp.concatenate([_deinterleave_heads(w[:, :nq]),
                             rep(_deinterleave_heads(w[:, nq:nq + nkv])),
                             rep(w[:, nq + nkv:])], axis=1).astype(BF16)
        qkv = _project(S, M, w, cos_t, sin_t, 2 * nq, L)
        O = _window_attention(qkv, wa_sink[j], L, not last)
        return _out_proj_resid_ln(O, wa_wo[j].astype(BF16), S, M, ln_g[i, 0], ln_b[i, 0], L, last)
    bmat, cmat, lam_r, lam_i = _s5_params(ssm_lam_re[j], ssm_lam_im[j], ssm_log_step[j], ssm_b_re[j],
                                          ssm_b_im[j], ssm_c_re[j], ssm_c_im[j])
    Mt = jnp.transpose(M[:, :, 0:2, :], (1, 2, 0, 3))
    Y_tm = _s5_scan(jnp.transpose(S, (1, 0, 2)), Mt, bmat, cmat, lam_r, lam_i, L)
    Y = jnp.transpose(Y_tm, (0, 2, 1, 3))
    return _glu_resid_ln(S, Y, ssm_d[j], ssm_w_glu[j].astype(BF16), M, ln_g[i, 0], ln_b[i, 0], L)


def kernel(x, c, ctx, c_ctx, mod_w, mod_b, ln_g, ln_b, peer_wq, peer_keys, peer_u, peer_v, da_wqkv, da_wo, da_lambda, da_subln, wa_wqkv, wa_wo, wa_sink, ssm_lam_re, ssm_lam_im, ssm_log_step, ssm_b_re, ssm_b_im, ssm_c_re, ssm_c_im, ssm_d, ssm_w_glu):
    B, T, D = x.shape
    L = ctx.shape[1]
    depth = mod_w.shape[0]
    cos_t, sin_t = _rope_tables(L, T)
    M_all = _mod_vectors(c, c_ctx, mod_w, mod_b)
    S = jnp.concatenate([ctx, x], axis=1)
    for i in range(depth):
        last = i == depth - 1
        S = _mixer_layer(i, S, M_all[i], cos_t, sin_t, L, last, da_wqkv, da_wo, da_lambda, da_subln, wa_wqkv, wa_wo,
                         wa_sink, ssm_lam_re, ssm_lam_im, ssm_log_step, ssm_b_re, ssm_b_im, ssm_c_re, ssm_c_im,
                         ssm_d, ssm_w_glu, ln_g, ln_b)
        S = _peer_ffn_resid_ln(S, M_all[i], peer_wq[i], peer_keys[i], peer_u[i], peer_v[i], ln_g[i, 1], ln_b[i, 1],
                               L, last)
    return S
```

```python
import functools
import math

import jax
import jax.numpy as jnp
from jax import lax
from jax.experimental import pallas as pl
from jax.experimental.pallas import tpu as pltpu

F32 = jnp.float32
BF16 = jnp.bfloat16

DEPTH = 4
N_MIXERS = 3
GRID_W = 64
HEAD_DIM = 64
ROPE_BASE = 10000.0
DA_HEADS = 8
WA_Q_HEADS = 16
WA_KV_HEADS = 4
WA_REP = WA_Q_HEADS // WA_KV_HEADS
WINDOW = 128
SSM_GROUP = 16
SSM_STATE = 64
PEER_HEADS = 8
PEER_NKEYS = 128
PEER_QDIM = 256
PEER_TOPK = 16
LN_EPS = 1e-5
DEEPNORM_ALPHA = (2 * DEPTH) ** 0.25
ATT_SCALE = HEAD_DIM ** -0.5

LANES = 128
SUBLANES = 8
ROW_TILE = 256
WIN_TILE = 128
SCAN_CHUNK = 128
SSM_SLAB_GROUPS = 8
DA_HEADS_PER_STEP = 2
PEER_TOK_BLOCK = 32
PEER_TOK_GROUP = 8
PEER_LOOKAHEAD = 2
EXPERT_RELAYOUT_BLOCK = 512
VMEM_LIMIT = 48 * 1024 * 1024
NEG_INF = float("-inf")


def _cparams(n_axes):
    return pltpu.CompilerParams(dimension_semantics=("arbitrary",) * n_axes, vmem_limit_bytes=VMEM_LIMIT)


def _gelu_tanh(x):
    return 0.5 * x * (1.0 + jnp.tanh(math.sqrt(2.0 / math.pi) * (x + 0.044715 * (x * x * x))))


def _sigmoid(x):
    return 1.0 / (1.0 + jnp.exp(-x))


def _layer_norm_rows(z, g, b):
    mu = jnp.mean(z, axis=-1, keepdims=True)
    zc = z - mu
    var = jnp.mean(zc * zc, axis=-1, keepdims=True)
    return zc * lax.rsqrt(var + LN_EPS) * g + b


def _modulate(x, m, shift_idx):
    return x * (1.0 + m[shift_idx + 1:shift_idx + 2]) + m[shift_idx:shift_idx + 1]


def _mod_kernel(a_ref, w_ref, b_ref, o_ref):
    a = a_ref[...]
    a = a * _sigmoid(a)
    o_ref[0] = jnp.dot(a, w_ref[0], preferred_element_type=F32, precision=lax.Precision.HIGHEST) + b_ref[0]


def _mod_vectors(c, c_ctx, mod_w, mod_b):
    B, D = c.shape
    depth, _, n6 = mod_w.shape
    rows = -(-(B + 1) // SUBLANES) * SUBLANES
    a = jnp.zeros((rows, D), F32).at[:B].set(c).at[B].set(c_ctx)
    tn = n6 // 4
    out = pl.pallas_call(
        _mod_kernel,
        grid=(depth, n6 // tn),
        in_specs=[pl.BlockSpec((rows, D), lambda i, j: (0, 0)),
                  pl.BlockSpec((1, D, tn), lambda i, j: (i, 0, j)),
                  pl.BlockSpec((1, 1, tn), lambda i, j: (i, 0, j))],
        out_specs=pl.BlockSpec((1, rows, tn), lambda i, j: (i, 0, j)),
        out_shape=jax.ShapeDtypeStruct((depth, rows, n6), F32),
        compiler_params=_cparams(2),
    )(a, mod_w, mod_b.reshape(depth, 1, n6))
    lat = out[:, :B].reshape(depth, B, 6, D)
    ctx = jnp.broadcast_to(out[:, B].reshape(depth, 1, 6, D), (depth, B, 6, D))
    m = jnp.stack([ctx, lat], axis=2)
    return jnp.pad(m, ((0, 0), (0, 0), (0, 0), (0, 2), (0, 0)))


def _mod_spec(D, nctx_blocks, j0):
    return pl.BlockSpec((1, 1, SUBLANES, D), lambda b, j: (b, jnp.minimum((j + j0) // nctx_blocks, 1), 0, 0))


def _proj_kernel(s_ref, m_ref, w_ref, cos_ref, sin_ref, o_ref, *, n_rope, tn):
    xb = _modulate(s_ref[0], m_ref[0, 0], 0).astype(BF16)
    tm = xb.shape[0]
    n_out = w_ref.shape[1]
    cos_t = cos_ref[...]
    sin_t = sin_ref[...]
    lane = lax.broadcasted_iota(jnp.int32, (tm, LANES), 1)
    first_half = (lane % HEAD_DIM) < (HEAD_DIM // 2)
    for c0 in range(0, n_out, tn):
        y = jnp.dot(xb, w_ref[:, c0:c0 + tn], preferred_element_type=F32)
        if c0 < n_rope:
            pieces = []
            for l0 in range(0, tn, LANES):
                yc = y[:, l0:l0 + LANES]
                partner = jnp.where(first_half,
                                    pltpu.roll(yc, LANES - HEAD_DIM // 2, 1),
                                    pltpu.roll(yc, HEAD_DIM // 2, 1))
                pieces.append(yc * cos_t + partner * sin_t)
            y = jnp.concatenate(pieces, axis=1)
        o_ref[0, :, c0:c0 + tn] = y.astype(o_ref.dtype)


def _project(S, M, w, cos_t, sin_t, n_rope, ctx_len):
    B, P, D = S.shape
    n_out = w.shape[1]
    tm = ROW_TILE
    kern = functools.partial(_proj_kernel, n_rope=n_rope, tn=512)
    return pl.pallas_call(
        kern,
        grid=(B, P // tm),
        in_specs=[pl.BlockSpec((1, tm, D), lambda b, j: (b, j, 0)),
                  _mod_spec(D, ctx_len // tm, 0),
                  pl.BlockSpec((D, n_out), lambda b, j: (0, 0)),
                  pl.BlockSpec((tm, LANES), lambda b, j: (j, 0)),
                  pl.BlockSpec((tm, LANES), lambda b, j: (j, 0))],
        out_specs=pl.BlockSpec((1, tm, n_out), lambda b, j: (b, j, 0)),
        out_shape=jax.ShapeDtypeStruct((B, P, n_out), BF16),
        compiler_params=_cparams(2),
    )(S, M, w, cos_t, sin_t)


def _dattn_kernel(lam_ref, g_ref, q_ref, k_ref, v_ref, o_ref, *, lam_init, ctx_len, ctx_queries):
    lp = lam_ref[...]
    lam = (jnp.exp(jnp.sum(lp[0:1] * lp[1:2], axis=-1, keepdims=True))
           - jnp.exp(jnp.sum(lp[2:3] * lp[3:4], axis=-1, keepdims=True)) + lam_init)
    n_heads = q_ref.shape[2] // LANES
    lane = lax.broadcasted_iota(jnp.int32, (q_ref.shape[1], LANES), 1)

    def attend_head(hh, nk):
        cols = slice(hh * LANES, (hh + 1) * LANES)
        q = q_ref[0, :, cols] * ATT_SCALE
        zero = jnp.zeros_like(q)
        k = k_ref[0, 0:nk, cols]
        v = v_ref[0, 0:nk, cols]
        probs = []
        for qm in (jnp.where(lane < HEAD_DIM, q, zero), jnp.where(lane >= HEAD_DIM, q, zero)):
            s = lax.dot_general(qm, k, (((1,), (1,)), ((), ())), preferred_element_type=F32)
            p = jnp.exp(s - jnp.max(s, axis=-1, keepdims=True))
            probs.append((p, 1.0 / jnp.sum(p, axis=-1, keepdims=True)))
        a = probs[0][0] * probs[0][1] - probs[1][0] * (lam * probs[1][1])
        o = jnp.dot(a.astype(BF16), v, preferred_element_type=F32)
        o = o * lax.rsqrt(jnp.mean(o * o, axis=-1, keepdims=True) + LN_EPS) * g_ref[...] * (1.0 - lam_init)
        o_ref[0, :, cols] = o.astype(o_ref.dtype)

    def attend(nk):
        for hh in range(n_heads):
            attend_head(hh, nk)

    if ctx_queries:
        qi = pl.program_id(2)

        @pl.when(qi == 0)
        def _():
            attend(ctx_len)

        @pl.when(qi > 0)
        def _():
            attend(k_ref.shape[1])
    else:
        attend(k_ref.shape[1])


def _diff_attention(qkv, lam_p, subln_g, lam_init, ctx_len, ctx_queries):
    B, P, _ = qkv.shape
    tq = ROW_TILE
    assert ctx_len == tq
    j0 = 0 if ctx_queries else 1
    H = DA_HEADS
    hg = H // DA_HEADS_PER_STEP
    width = DA_HEADS_PER_STEP * LANES
    kern = functools.partial(_dattn_kernel, lam_init=lam_init, ctx_len=ctx_len, ctx_queries=ctx_queries)
    return pl.pallas_call(
        kern,
        grid=(B, hg, P // tq - j0),
        in_specs=[pl.BlockSpec((4, HEAD_DIM), lambda b, h, i: (0, 0)),
                  pl.BlockSpec((1, 2 * HEAD_DIM), lambda b, h, i: (0, 0)),
                  pl.BlockSpec((1, tq, width), lambda b, h, i: (b, i + j0, h)),
                  pl.BlockSpec((1, P, width), lambda b, h, i: (b, 0, hg + h)),
                  pl.BlockSpec((1, P, width), lambda b, h, i: (b, 0, 2 * hg + h))],
        out_specs=pl.BlockSpec((1, tq, width), lambda b, h, i: (b, i + j0, h)),
        out_shape=jax.ShapeDtypeStruct((B, P, H * LANES), BF16),
        compiler_params=_cparams(3),
    )(lam_p, subln_g.reshape(1, -1), qkv, qkv, qkv)


def _wattn_kernel(sink_ref, q_ref, kc_ref, k0_ref, k1_ref, k2_ref, vc_ref, v0_ref, v1_ref, v2_ref, o_ref,
                  *, n_ctx_blocks, n_lat_blocks, j0):
    qb = pl.program_id(1) + j0
    tq = q_ref.shape[1]
    width = WA_REP * HEAD_DIM
    n_kv = q_ref.shape[2] // width
    lane = lax.broadcasted_iota(jnp.int32, (tq, width), 1)
    head_sel = [(lane >= r * HEAD_DIM) & (lane < (r + 1) * HEAD_DIM) for r in range(WA_REP)]
    dims = (((1,), (1,)), ((), ()))

    def queries(g):
        q = q_ref[0, :, g * width:(g + 1) * width] * ATT_SCALE
        zero = jnp.zeros_like(q)
        qm = jnp.concatenate([jnp.where(head_sel[r], q, zero) for r in range(WA_REP)], axis=0)
        sink_col = jnp.concatenate([jnp.full((tq, 1), sink_ref[g * WA_REP + r], F32) for r in range(WA_REP)],
                                   axis=0)
        return qm, sink_col

    def finish(g, o_all, denom):
        o_all = o_all / denom
        o = jnp.zeros((tq, width), F32)
        for r in range(WA_REP):
            o = o + jnp.where(head_sel[r], o_all[r * tq:(r + 1) * tq, :], 0.0)
        o_ref[0, :, g * width:(g + 1) * width] = o.astype(o_ref.dtype)

    @pl.when(qb < n_ctx_blocks)
    def _():
        for g in range(n_kv):
            cols = slice(g * width, (g + 1) * width)
            qm, sink_col = queries(g)
            s_c = lax.dot_general(qm, kc_ref[0, :, cols], dims, preferred_element_type=F32)
            m = jnp.maximum(jnp.max(s_c, axis=-1, keepdims=True), sink_col)
            p_c = jnp.exp(s_c - m)
            denom = jnp.sum(p_c, axis=-1, keepdims=True) + jnp.exp(sink_col - m)
            finish(g, jnp.dot(p_c.astype(BF16), vc_ref[0, :, cols], preferred_element_type=F32), denom)

    @pl.when(qb >= n_ctx_blocks)
    def _():
        lb = qb - n_ctx_blocks
        iq = lax.broadcasted_iota(jnp.int32, (WA_REP * tq, 3 * tq), 0) % tq
        ik = lax.broadcasted_iota(jnp.int32, (WA_REP * tq, 3 * tq), 1)
        lo = jnp.where(lb == 0, tq, 0)
        hi = jnp.where(lb == n_lat_blocks - 1, 2 * tq, 3 * tq)
        band = (ik >= iq) & (ik <= iq + 2 * WINDOW) & (ik >= lo) & (ik < hi)
        for g in range(n_kv):
            cols = slice(g * width, (g + 1) * width)
            qm, sink_col = queries(g)
            kw = jnp.concatenate([k0_ref[0, :, cols], k1_ref[0, :, cols], k2_ref[0, :, cols]], axis=0)
            vw = jnp.concatenate([v0_ref[0, :, cols], v1_ref[0, :, cols], v2_ref[0, :, cols]], axis=0)
            s_c = lax.dot_general(qm, kc_ref[0, :, cols], dims, preferred_element_type=F32)
            s_w = jnp.where(band, lax.dot_general(qm, kw, dims, preferred_element_type=F32), NEG_INF)
            m = jnp.maximum(jnp.maximum(jnp.max(s_c, axis=-1, keepdims=True),
                                        jnp.max(s_w, axis=-1, keepdims=True)), sink_col)
            p_c = jnp.exp(s_c - m)
            p_w = jnp.exp(s_w - m)
            denom = (jnp.sum(p_c, axis=-1, keepdims=True) + jnp.sum(p_w, axis=-1, keepdims=True)
                     + jnp.exp(sink_col - m))
            o_all = (jnp.dot(p_c.astype(BF16), vc_ref[0, :, cols], preferred_element_type=F32)
                     + jnp.dot(p_w.astype(BF16), vw, preferred_element_type=F32))
            finish(g, o_all, denom)


def _window_attention(qkv, sink, ctx_len, ctx_queries):
    B, P, _ = qkv.shape
    tq = WIN_TILE
    assert tq == WINDOW
    nctx = ctx_len // tq
    nlat = P // tq - nctx
    j0 = 0 if ctx_queries else nctx
    full = WA_KV_HEADS * WA_REP * HEAD_DIM

    def lat_spec(off, region):
        return pl.BlockSpec((1, tq, full),
                            lambda b, j: (b, jnp.clip(j + j0 - nctx + off, 0, nlat - 1) + nctx, region))

    row = lambda b, j: (b, j + j0, 0)
    kern = functools.partial(_wattn_kernel, n_ctx_blocks=nctx, n_lat_blocks=nlat, j0=j0)
    return pl.pallas_call(
        kern,
        grid=(B, P // tq - j0),
        in_specs=([pl.BlockSpec(memory_space=pltpu.SMEM), pl.BlockSpec((1, tq, full), row),
                   pl.BlockSpec((1, ctx_len, full), lambda b, j: (b, 0, 1))]
                  + [lat_spec(off, 1) for off in (-1, 0, 1)]
                  + [pl.BlockSpec((1, ctx_len, full), lambda b, j: (b, 0, 2))]
                  + [lat_spec(off, 2) for off in (-1, 0, 1)]),
        out_specs=pl.BlockSpec((1, tq, full), row),
        out_shape=jax.ShapeDtypeStruct((B, P, full), BF16),
        compiler_params=_cparams(2),
    )(sink, qkv, qkv, qkv, qkv, qkv, qkv, qkv, qkv, qkv)


def _resid_ln_kernel(o_ref, w_ref, s_ref, m_ref, g_ref, b_ref, out_ref):
    y = jnp.dot(o_ref[0], w_ref[...], preferred_element_type=F32)
    z = DEEPNORM_ALPHA * s_ref[0] + m_ref[0, 0][2:3] * y
    out_ref[0] = _layer_norm_rows(z, g_ref[...], b_ref[...])


def _out_proj_resid_ln(O, w, S, M, ln_g, ln_b, ctx_len, skip_ctx):
    B, P, D = S.shape
    kdim = O.shape[2]
    tm = ROW_TILE
    j0 = ctx_len // tm if skip_ctx else 0
    row = lambda b, j: (b, j + j0, 0)
    return pl.pallas_call(
        _resid_ln_kernel,
        grid=(B, P // tm - j0),
        in_specs=[pl.BlockSpec((1, tm, kdim), row),
                  pl.BlockSpec((kdim, D), lambda b, j: (0, 0)),
                  pl.BlockSpec((1, tm, D), row),
                  _mod_spec(D, ctx_len // tm, j0),
                  pl.BlockSpec((1, D), lambda b, j: (0, 0)),
                  pl.BlockSpec((1, D), lambda b, j: (0, 0))],
        out_specs=pl.BlockSpec((1, tm, D), row),
        out_shape=jax.ShapeDtypeStruct((B, P, D), F32),
        input_output_aliases={2: 0},
        compiler_params=_cparams(2),
    )(O, w, S, M, ln_g.reshape(1, D), ln_b.reshape(1, D))


def _s5_kernel(s_ref, mt_ref, bm_ref, cm_ref, lr_ref, li_ref, y_ref, xbuf, state):
    d = pl.program_id(0)
    c = pl.program_id(2)
    tc, nb, w_in = s_ref.shape
    half = lr_ref.shape[-1]

    @pl.when(c == 0)
    def _():
        state[...] = jnp.zeros_like(state)

    mt = mt_ref[0]
    u = s_ref[...] * (1.0 + mt[1][None]) + mt[0][None]
    u2 = u.reshape(tc * nb, w_in).astype(BF16)
    xbuf[...] = jnp.dot(u2, bm_ref[0, 0], preferred_element_type=F32)
    lr = jnp.broadcast_to(lr_ref[0, 0], (nb, half))
    li = jnp.broadcast_to(li_ref[0, 0], (nb, half))

    def step(t, carry):
        sr, si = carry
        tt = jnp.where(d == 0, t, tc - 1 - t)
        r0 = pl.multiple_of(tt * nb, nb)
        nr = lr * sr - li * si + xbuf[pl.ds(r0, nb), 0:half]
        ni = lr * si + li * sr + xbuf[pl.ds(r0, nb), half:2 * half]
        xbuf[pl.ds(r0, nb), 0:half] = nr
        xbuf[pl.ds(r0, nb), half:2 * half] = ni
        return nr, ni

    sr, si = lax.fori_loop(0, tc, step, (state[:, 0:half], state[:, half:2 * half]), unroll=4)
    state[:, 0:half] = sr
    state[:, half:2 * half] = si
    y = jnp.dot(xbuf[...].astype(BF16), cm_ref[0, 0], preferred_element_type=F32)
    y_ref[0] = y.reshape(tc, nb, w_in)


def _s5_scan(S_tm, Mt, bmat, cmat, lam_r, lam_i, ctx_len):
    P, B, D = S_tm.shape
    tc = SCAN_CHUNK
    w_in = SSM_SLAB_GROUPS * SSM_GROUP
    n_slab = D // w_in
    half = SSM_SLAB_GROUPS * SSM_STATE
    nch = P // tc
    nctx = ctx_len // tc
    nlat = nch - nctx

    def chunk(d, c):
        q = nch - 1 - c
        back = jnp.where(q < nlat, q + nctx, q - nlat)
        return jnp.where(d == 0, c, back)

    return pl.pallas_call(
        _s5_kernel,
        grid=(2, n_slab, nch),
        in_specs=[pl.BlockSpec((tc, B, w_in), lambda d, s, c: (chunk(d, c), 0, s)),
                  pl.BlockSpec((1, 2, B, w_in), lambda d, s, c: (jnp.minimum(chunk(d, c) // nctx, 1), 0, 0, s)),
                  pl.BlockSpec((1, 1, w_in, 2 * half), lambda d, s, c: (d, s, 0, 0)),
                  pl.BlockSpec((1, 1, 2 * half, w_in), lambda d, s, c: (d, s, 0, 0)),
                  pl.BlockSpec((1, 1, 1, half), lambda d, s, c: (d, s, 0, 0)),
                  pl.BlockSpec((1, 1, 1, half), lambda d, s, c: (d, s, 0, 0))],
        out_specs=pl.BlockSpec((1, tc, B, w_in), lambda d, s, c: (d, chunk(d, c), 0, s)),
        out_shape=jax.ShapeDtypeStruct((2, P, B, D), F32),
        scratch_shapes=[pltpu.VMEM((tc * B, 2 * half), F32), pltpu.VMEM((B, 2 * half), F32)],
        compiler_params=_cparams(3),
    )(S_tm, Mt, bmat, cmat, lam_r, lam_i)


def _s5_params(lam_re, lam_im, log_step, b_re, b_im, c_re, c_im):
    lam = lax.complex(lam_re.astype(F32), lam_im.astype(F32))
    step = jnp.exp(log_step.astype(F32))[..., None]
    lam_bar = jnp.exp(lam * step)
    b_bar = lax.complex(b_re.astype(F32), b_im.astype(F32)) * ((lam_bar - 1) / lam)[..., None]
    G, Pst, Hg = b_bar.shape[1:]
    ng = SSM_SLAB_GROUPS
    ns = G // ng
    eye = jnp.eye(ng, dtype=F32)

    def bdiag_in(x):
        x = x.reshape(2, ns, ng, Pst, Hg)
        return jnp.einsum('rsgph,gk->rsghkp', x, eye).reshape(2, ns, ng * Hg, ng * Pst)

    def bdiag_out(x):
        x = x.reshape(2, ns, ng, Hg, Pst)
        return jnp.einsum('rsghp,gk->rsgpkh', x, eye).reshape(2, ns, ng * Pst, ng * Hg)

    bmat = jnp.concatenate([bdiag_in(jnp.real(b_bar)), bdiag_in(jnp.imag(b_bar))], axis=-1).astype(BF16)
    cmat = jnp.concatenate([bdiag_out(c_re.astype(F32)), bdiag_out(-c_im.astype(F32))], axis=-2).astype(BF16)
    lam_r = jnp.real(lam_bar).reshape(2, ns, 1, ng * Pst)
    lam_i = jnp.imag(lam_bar).reshape(2, ns, 1, ng * Pst)
    return bmat, cmat, lam_r, lam_i


def _glu_ln_kernel(s_ref, y0_ref, y1_ref, d_ref, w_ref, m_ref, g_ref, b_ref, out_ref):
    m = m_ref[0, 0]
    s = s_ref[0]
    y = d_ref[...] * _modulate(s, m, 0) + y0_ref[0, 0] + y1_ref[0, 0]
    z = jnp.dot(_gelu_tanh(y).astype(BF16), w_ref[...], preferred_element_type=F32)
    D = s.shape[1]
    o = z[:, 0:D] * _sigmoid(z[:, D:2 * D])
    out_ref[0] = _layer_norm_rows(DEEPNORM_ALPHA * s + m[2:3] * o, g_ref[...], b_ref[...])


def _glu_resid_ln(S, Y, d_skip, w_glu, M, ln_g, ln_b, ctx_len):
    B, P, D = S.shape
    tm = ROW_TILE
    row = lambda b, j: (b, j, 0)
    return pl.pallas_call(
        _glu_ln_kernel,
        grid=(B, P // tm),
        in_specs=[pl.BlockSpec((1, tm, D), row),
                  pl.BlockSpec((1, 1, tm, D), lambda b, j: (0, b, j, 0)),
                  pl.BlockSpec((1, 1, tm, D), lambda b, j: (1, b, j, 0)),
                  pl.BlockSpec((1, D), lambda b, j: (0, 0)),
                  pl.BlockSpec((D, 2 * D), lambda b, j: (0, 0)),
                  _mod_spec(D, ctx_len // tm, 0),
                  pl.BlockSpec((1, D), lambda b, j: (0, 0)),
                  pl.BlockSpec((1, D), lambda b, j: (0, 0))],
        out_specs=pl.BlockSpec((1, tm, D), row),
        out_shape=jax.ShapeDtypeStruct((B, P, D), F32),
        compiler_params=_cparams(2),
    )(S, Y, Y, d_skip.reshape(1, D), w_glu, M, ln_g.reshape(1, D), ln_b.reshape(1, D))


def _peer_topk_kernel(s_ref, m_ref, wqt_ref, keys_ref, g_out, e_out, qt):
    tn = s_ref.shape[1]
    nk = PEER_NKEYS
    k = PEER_TOPK
    xb = _modulate(s_ref[0], m_ref[0, 0], 3).astype(BF16)
    qt[...] = lax.dot_general(wqt_ref[...], xb, (((1,), (1,)), ((), ())), preferred_element_type=F32)

    def head(h, carry):
        krow = lax.broadcasted_iota(jnp.int32, (nk, tn), 0).astype(F32)
        rank = lax.broadcasted_iota(jnp.int32, (k, tn), 0)
        sub = lax.broadcasted_iota(jnp.int32, (SUBLANES, tn), 0)
        subf = sub.astype(F32)
        tops = []
        for j in range(2):
            qs = qt[pl.ds(pl.multiple_of(h * 2 * nk + j * nk, nk), nk), :].astype(BF16)
            s = jnp.dot(keys_ref[j, h], qs, preferred_element_type=F32)
            top = []
            for it in range(k):
                mx = jnp.max(s, axis=0, keepdims=True)
                ix = jnp.min(jnp.where(s == mx, krow, float(nk)), axis=0, keepdims=True)
                s = jnp.where(krow == ix, NEG_INF, s)
                top.append((mx, ix))
            tops.append(top)
        def stack8(items):
            out = jnp.zeros((SUBLANES, tn), F32)
            for s_i, item in enumerate(items):
                out = jnp.where(sub == s_i, item, out)
            return out

        def pack(first, second, comb):
            a_lo = first[:SUBLANES]
            a_hi = pltpu.roll(stack8(first[SUBLANES:]), 2, 0)
            b_lo = stack8(second[:SUBLANES])
            b_hi = stack8(second[SUBLANES:])
            b_0 = second[0]
            return [comb(a_lo[0], b_lo), comb(a_lo[0], b_hi), comb(a_lo[1], b_lo),
                    jnp.where(sub < 5, comb(a_lo[2], b_lo), comb(a_lo[4], pltpu.roll(b_lo, 5, 0))),
                    jnp.where(sub < 4, comb(a_lo[3], b_lo),
                              jnp.where(sub < 6, comb(a_lo[5], pltpu.roll(b_lo, 4, 0)),
                                        comb(a_lo[6], pltpu.roll(b_lo, 6, 0)))),
                    jnp.where(sub < 2, comb(a_lo[7], b_lo), comb(a_hi, b_0)),
                    comb(a_hi, b_0)]

        vals = pack([t[0] for t in tops[0]], [t[0] for t in tops[1]], lambda a, b: a + b)
        vals[6] = jnp.where(sub < 2, vals[6], NEG_INF)
        cand = jnp.concatenate(vals, axis=0)
        cexp = jnp.concatenate(pack([t[1] for t in tops[0]], [t[1] for t in tops[1]],
                                    lambda a, b: a * float(nk) + b), axis=0)
        pos_const = jnp.concatenate(
            [subf, 8.0 + subf, 16.0 + subf,
             jnp.where(sub < 5, 32.0 + subf, 59.0 + subf),
             jnp.where(sub < 4, 48.0 + subf, jnp.where(sub < 6, 76.0 + subf, 90.0 + subf)),
             jnp.where(sub < 2, 112.0 + subf, 16.0 * (subf + 6.0)),
             16.0 * (subf + 14.0)], axis=0)
        ts = jnp.zeros((k, tn), F32)
        te = jnp.zeros((k, tn), F32)
        best = None
        for it in range(k):
            mx = jnp.max(cand, axis=0, keepdims=True)
            sel = jnp.min(jnp.where(cand == mx, pos_const, float(k * k)), axis=0, keepdims=True)
            hit = pos_const == sel
            ex = jnp.max(jnp.where(hit, cexp, -1.0), axis=0, keepdims=True)
            cand = jnp.where(hit, NEG_INF, cand)
            ts = jnp.where(rank == it, mx, ts)
            te = jnp.where(rank == it, ex, te)
            best = mx if best is None else best
        p = jnp.exp(ts - best)
        gate = p / jnp.sum(p, axis=0, keepdims=True)
        r0 = pl.multiple_of(h * k, k)
        g_out[0, pl.ds(r0, k), :] = gate
        e_out[0, pl.ds(r0, k), :] = te.astype(jnp.int32)
        return carry

    lax.fori_loop(0, PEER_HEADS, head, 0, unroll=4)


def _peer_topk(S, M, wq_t, keys, ctx_len, skip_ctx):
    B, P, D = S.shape
    tn = ROW_TILE
    j0 = ctx_len // tn if skip_ctx else 0
    nblk = P // tn - j0
    hk = PEER_HEADS * PEER_TOPK
    out_spec = pl.BlockSpec((1, hk, tn), lambda b, j: (b, 0, j))
    return pl.pallas_call(
        _peer_topk_kernel,
        grid=(B, nblk),
        in_specs=[pl.BlockSpec((1, tn, D), lambda b, j: (b, j + j0, 0)),
                  _mod_spec(D, ctx_len // tn, j0),
                  pl.BlockSpec(wq_t.shape, lambda b, j: (0, 0)),
                  pl.BlockSpec(keys.shape, lambda b, j: (0, 0, 0, 0))],
        out_specs=[out_spec, out_spec],
        out_shape=[jax.ShapeDtypeStruct((B, hk, nblk * tn), F32),
                   jax.ShapeDtypeStruct((B, hk, nblk * tn), jnp.int32)],
        scratch_shapes=[pltpu.VMEM((wq_t.shape[0], tn), F32)],
        compiler_params=_cparams(2),
    )(S, M, wq_t, keys)


def _sublane_sums(vregs):
    sub = lax.broadcasted_iota(jnp.int32, (SUBLANES, LANES), 0)
    level = list(vregs)
    half = SUBLANES // 2
    while half >= 1:
        lower = (sub % (2 * half)) < half
        nxt = []
        for k in range(len(level) // 2):
            a, b = level[k], level[k + len(level) // 2]
            stay = jnp.where(lower, a, b)
            move = jnp.where(lower, b, a)
            if 2 * half == SUBLANES:
                moved = pltpu.roll(move, half, 0)
            else:
                moved = jnp.where(lower, pltpu.roll(move, SUBLANES - half, 0), pltpu.roll(move, half, 0))
            nxt.append(stay + moved)
        level = nxt
        half //= 2
    return level[0]


def _sublane_transpose(vregs):
    sub = lax.broadcasted_iota(jnp.int32, (SUBLANES, LANES), 0)
    v = list(vregs)
    half = SUBLANES // 2
    while half >= 1:
        clear = (sub & half) == 0
        for i in range(SUBLANES):
            if i & half == 0:
                a, b = v[i], v[i + half]
                v[i] = jnp.where(clear, a, pltpu.roll(b, half, 0))
                v[i + half] = jnp.where(clear, pltpu.roll(a, SUBLANES - half, 0), b)
        half //= 2
    return v


def _peer_gather_kernel(idx_ref, nxt_ref, g_ref, s_ref, m_ref, lg_ref, lb_ref, tab_ref, out_ref, buf, sem, obuf,
                        cbuf):
    i = pl.program_id(0)
    n_steps = pl.num_programs(0)
    tb = s_ref.shape[0]
    hk = g_ref.shape[1]
    gt = PEER_TOK_GROUP
    rows = gt * hk
    n_groups = tb // gt
    pgs = hk // SUBLANES

    def row_copy(e, slot, r, prio):
        pltpu.make_async_copy(tab_ref.at[e], buf.at[slot, r], sem.at[slot]).start(priority=prio)

    def issue(ref, grp, slot):
        per_it = 4 * SUBLANES

        def body(it, carry):
            for k in range(per_it):
                row_copy(ref[grp * rows + it * per_it + k], slot, it * per_it + k, k % 2)
            return carry

        lax.fori_loop(0, rows // per_it, body, 0)

    def issue_token(ref, grp, slot, t, part, n_parts):
        for r in range(t * hk + part * hk // n_parts, t * hk + (part + 1) * hk // n_parts):
            row_copy(ref[grp * rows + r], slot, r, r % 2)

    def wait(slot):
        pltpu.make_async_copy(buf.at[(slot + 1) % n_groups], buf.at[slot], sem.at[slot]).wait()

    @pl.when(i == 0)
    def _():
        for g0 in range(PEER_LOOKAHEAD):
            issue(idx_ref, g0, g0)

    m = m_ref[0]
    s_tok = []
    for t0 in range(0, tb, SUBLANES):
        s_tok += _sublane_transpose([s_ref[t0:t0 + SUBLANES, j * LANES:(j + 1) * LANES] for j in range(SUBLANES)])
    s_rows = jnp.stack(s_tok, axis=0)
    h_rows = s_rows * (1.0 + m[4][None]) + m[3][None]
    gates = g_ref[0]
    for grp in range(n_groups):
        slot = grp
        ahead = grp + PEER_LOOKAHEAD
        ahead_ref, ahead_grp = (idx_ref, ahead) if ahead < n_groups else (nxt_ref, ahead - n_groups)
        wait(slot)

        n_parts = 2 * pgs

        def issue_part(t_issue, part):
            if t_issue is not None:
                issue_token(ahead_ref, ahead_grp, ahead % n_groups, t_issue, part, n_parts)

        def expert_scores(t, t_issue):
            h_t = h_rows[grp * gt + t]
            scs = []
            for q in range(pgs):
                issue_part(t_issue, q)
                r0 = t * hk + q * SUBLANES
                part = _sublane_sums([buf[slot, r0 + k, 0] * h_t for k in range(SUBLANES)])
                scs.append(jnp.sum(part, axis=1, keepdims=True))
            return jnp.concatenate(scs, axis=0)

        def expert_mix(t, sc, t_issue):
            tok = grp * gt + t
            cbuf[tok] = jnp.broadcast_to(gates[:, tok:tok + 1] * _gelu_tanh(sc), (hk, LANES))
            accs = [None] * 4
            for r in range(hk):
                if r % SUBLANES == 0:
                    issue_part(t_issue, pgs + r // SUBLANES)
                term = jnp.broadcast_to(cbuf[tok, r:r + 1, :], (SUBLANES, LANES)) * buf[slot, t * hk + r, 1]
                accs[r % 4] = term if accs[r % 4] is None else accs[r % 4] + term
            obuf[tok] = (accs[0] + accs[1]) + (accs[2] + accs[3])

        sc_prev = expert_scores(0, None)
        for t in range(gt):
            if t + 1 < gt:
                sc_next = expert_scores(t + 1, t)
            else:
                sc_next = None
                for part in range(pgs):
                    issue_part(t, part)
            expert_mix(t, sc_prev, t)
            sc_prev = sc_next
    z = DEEPNORM_ALPHA * s_rows + m[5][None] * obuf[...]
    n_el = z.shape[1] * z.shape[2]
    mu = jnp.sum(jnp.sum(z, axis=2, keepdims=True), axis=1, keepdims=True) / n_el
    zc = z - mu
    var = jnp.sum(jnp.sum(zc * zc, axis=2, keepdims=True), axis=1, keepdims=True) / n_el
    y = zc * lax.rsqrt(var + LN_EPS) * lg_ref[...][None] + lb_ref[...][None]
    for t0 in range(0, tb, SUBLANES):
        tiles = _sublane_transpose([y[t0 + t] for t in range(SUBLANES)])
        for j in range(SUBLANES):
            out_ref[t0:t0 + SUBLANES, j * LANES:(j + 1) * LANES] = tiles[j]

    @pl.when(i == n_steps - 1)
    def _():
        for slot in range(PEER_LOOKAHEAD):
            wait(slot)


def _peer_gather(S, M, gates_t, eidx, table, ln_g, ln_b, ctx_len, skip_ctx):
    B, P, D = S.shape
    tb = PEER_TOK_BLOCK
    n_slots = tb // PEER_TOK_GROUP
    assert PEER_LOOKAHEAD < n_slots
    hk = PEER_HEADS * PEER_TOPK
    j0 = ctx_len // tb if skip_ctx else 0
    per_b = P // tb - j0
    n_steps = B * per_b
    nctx = ctx_len // tb
    sub = D // LANES
    assert sub == SUBLANES
    S2 = S.reshape(B * P, D)
    M3 = M.reshape(B * 2, SUBLANES, sub, LANES)
    row_blk = lambda i: (i // per_b) * (P // tb) + i % per_b + j0
    out = pl.pallas_call(
        _peer_gather_kernel,
        grid=(n_steps,),
        in_specs=[pl.BlockSpec((tb * hk,), lambda i: (i,), memory_space=pltpu.SMEM),
                  pl.BlockSpec((tb * hk,), lambda i: (jnp.minimum(i + 1, n_steps - 1),), memory_space=pltpu.SMEM),
                  pl.BlockSpec((1, hk, tb), lambda i: (i, 0, 0)),
                  pl.BlockSpec((tb, D), lambda i: (row_blk(i), 0)),
                  pl.BlockSpec((1, SUBLANES, sub, LANES),
                               lambda i: ((i // per_b) * 2 + jnp.minimum((i % per_b + j0) // nctx, 1), 0, 0, 0)),
                  pl.BlockSpec((sub, LANES), lambda i: (0, 0)),
                  pl.BlockSpec((sub, LANES), lambda i: (0, 0)),
                  pl.BlockSpec(memory_space=pl.ANY)],
        out_specs=pl.BlockSpec((tb, D), (lambda i: (i, 0)) if skip_ctx else (lambda i: (row_blk(i), 0))),
        out_shape=jax.ShapeDtypeStruct((n_steps * tb if skip_ctx else B * P, D), F32),
        scratch_shapes=[pltpu.VMEM((n_slots, PEER_TOK_GROUP * hk, 2, sub, LANES), F32),
                        pltpu.SemaphoreType.DMA((n_slots,)),
                        pltpu.VMEM((tb, sub, LANES), F32),
                        pltpu.VMEM((tb, hk, LANES), F32)],
        input_output_aliases={} if skip_ctx else {3: 0},
        compiler_params=_cparams(1),
    )(eidx, eidx, gates_t, S2, M3, ln_g.reshape(sub, LANES), ln_b.reshape(sub, LANES), table)
    return out.reshape(B, -1, D)


def _peer_ffn_resid_ln(S, M, wq, keys, u_all, v_all, layer, ln_g, ln_b, ctx_len, skip_ctx):
    B, P, D = S.shape
    wq_t = wq.T.astype(BF16)
    gates, eidx = _peer_topk(S, M, wq_t, keys.astype(BF16), ctx_len, skip_ctx)
    hk = gates.shape[1]
    tb = PEER_TOK_BLOCK
    gates_t = jnp.transpose(gates.reshape(B, hk, -1, tb), (0, 2, 1, 3)).reshape(-1, hk, tb)
    eidx_flat = jnp.transpose(eidx, (0, 2, 1)).reshape(-1)
    table = _expert_table(u_all, v_all, layer)
    return _peer_gather(S, M, gates_t, eidx_flat, table, ln_g, ln_b, ctx_len, skip_ctx)


def _expert_table_kernel(u_ref, v_ref, o_ref):
    n_groups = u_ref.shape[0] // SUBLANES

    def group(g, carry):
        r0 = pl.multiple_of(g * SUBLANES, SUBLANES)
        for half, ref in enumerate((u_ref, v_ref)):
            rows = _sublane_transpose([ref[pl.ds(r0, SUBLANES), j * LANES:(j + 1) * LANES] for j in range(SUBLANES)])
            for k in range(SUBLANES):
                o_ref[r0 + k, half] = rows[k]
        return carry

    lax.fori_loop(0, n_groups, group, 0)


def _expert_table(u_all, v_all, layer):
    _, E, D = u_all.shape
    assert D == SUBLANES * LANES
    te = EXPERT_RELAYOUT_BLOCK
    in_spec = pl.BlockSpec((None, te, D), lambda i: (layer, i, 0))
    return pl.pallas_call(
        _expert_table_kernel,
        grid=(E // te,),
        in_specs=[in_spec, in_spec],
        out_specs=pl.BlockSpec((te, 2, SUBLANES, LANES), lambda i: (i, 0, 0, 0)),
        out_shape=jax.ShapeDtypeStruct((E, 2, SUBLANES, LANES), u_all.dtype),
        compiler_params=_cparams(1),
    )(u_all, v_all)


def _deinterleave_heads(w):
    d_in, n = w.shape
    return w.reshape(d_in, n // HEAD_DIM, HEAD_DIM // 2, 2).transpose(0, 1, 3, 2).reshape(d_in, n)


def _rope_tables(ctx_len, n_lat):
    rows = n_lat // GRID_W
    row = jnp.repeat(jnp.arange(rows, dtype=F32), GRID_W)
    col = jnp.tile(jnp.arange(GRID_W, dtype=F32), rows)
    n_freq = HEAD_DIM // 4
    inv = ROPE_BASE ** (-jnp.arange(n_freq, dtype=F32) / n_freq)
    ang = jnp.concatenate([row[:, None] * inv, col[:, None] * inv], -1)
    cos, sin = jnp.cos(ang), jnp.sin(ang)
    reps = LANES // HEAD_DIM
    cos_t = jnp.tile(jnp.concatenate([cos, cos], -1), (1, reps))
    sin_t = jnp.tile(jnp.concatenate([-sin, sin], -1), (1, reps))
    cos_t = jnp.concatenate([jnp.ones((ctx_len, LANES), F32), cos_t], 0)
    sin_t = jnp.concatenate([jnp.zeros((ctx_len, LANES), F32), sin_t], 0)
    return cos_t, sin_t


def _mixer_layer(i, S, M, cos_t, sin_t, L, last, da_wqkv, da_wo, da_lambda, da_subln, wa_wqkv, wa_wo, wa_sink,
                 ssm_lam_re, ssm_lam_im, ssm_log_step, ssm_b_re, ssm_b_im, ssm_c_re, ssm_c_im, ssm_d, ssm_w_glu,
                 ln_g, ln_b):
    D = S.shape[2]
    kind, j = i % N_MIXERS, i // N_MIXERS
    if kind == 0:
        lam_init = 0.8 - 0.6 * math.exp(-0.3 * i)
        w = da_wqkv[j]
        n_qk = 2 * DA_HEADS * 2 * HEAD_DIM
        w = jnp.concatenate([_deinterleave_heads(w[:, :n_qk]), w[:, n_qk:]], axis=1).astype(BF16)
        qkv = _project(S, M, w, cos_t, sin_t, n_qk, L)
        O = _diff_attention(qkv, da_lambda[j], da_subln[j], lam_init, L, not last)
        return _out_proj_resid_ln(O, da_wo[j].astype(BF16), S, M, ln_g[i, 0], ln_b[i, 0], L, last)
    if kind == 1:
        w = wa_wqkv[j]
        nq = WA_Q_HEADS * HEAD_DIM
        nkv = WA_KV_HEADS * HEAD_DIM
        rep = lambda m: jnp.tile(m.reshape(D, WA_KV_HEADS, 1, HEAD_DIM), (1, 1, WA_REP, 1)).reshape(D, nq)
        w = jnp.concatenate([_deinterleave_heads(w[:, :nq]),
                             rep(_deinterleave_heads(w[:, nq:nq + nkv])),
                             rep(w[:, nq + nkv:])], axis=1).astype(BF16)
        qkv = _project(S, M, w, cos_t, sin_t, 2 * nq, L)
        O = _window_attention(qkv, wa_sink[j], L, not last)
        return _out_proj_resid_ln(O, wa_wo[j].astype(BF16), S, M, ln_g[i, 0], ln_b[i, 0], L, last)
    bmat, cmat, lam_r, lam_i = _s5_params(ssm_lam_re[j], ssm_lam_im[j], ssm_log_step[j], ssm_b_re[j],
                                          ssm_b_im[j], ssm_c_re[j], ssm_c_im[j])
    Mt = jnp.transpose(M[:, :, 0:2, :], (1, 2, 0, 3))
    Y_tm = _s5_scan(jnp.transpose(S, (1, 0, 2)), Mt, bmat, cmat, lam_r, lam_i, L)
    Y = jnp.transpose(Y_tm, (0, 2, 1, 3))
    return _glu_resid_ln(S, Y, ssm_d[j], ssm_w_glu[j].astype(BF16), M, ln_g[i, 0], ln_b[i, 0], L)


def kernel(x, c, ctx, c_ctx, mod_w, mod_b, ln_g, ln_b, peer_wq, peer_keys, peer_u, peer_v, da_wqkv, da_wo, da_lambda, da_subln, wa_wqkv, wa_wo, wa_sink, ssm_lam_re, ssm_lam_im, ssm_log_step, ssm_b_re, ssm_b_im, ssm_c_re, ssm_c_im, ssm_d, ssm_w_glu):
    B, T, D = x.shape
    L = ctx.shape[1]
    depth = mod_w.shape[0]
    cos_t, sin_t = _rope_tables(L, T)
    M_all = _mod_vectors(c, c_ctx, mod_w, mod_b)
    S = jnp.concatenate([ctx, x], axis=1)
    for i in range(depth):
        last = i == depth - 1
        S = _mixer_layer(i, S, M_all[i], cos_t, sin_t, L, last, da_wqkv, da_wo, da_lambda, da_subln, wa_wqkv, wa_wo,
                         wa_sink, ssm_lam_re, ssm_lam_im, ssm_log_step, ssm_b_re, ssm_b_im, ssm_c_re, ssm_c_im,
                         ssm_d, ssm_w_glu, ln_g, ln_b)
        S = _peer_ffn_resid_ln(S, M_all[i], peer_wq[i], peer_keys[i], peer_u, peer_v, i, ln_g[i, 1], ln_b[i, 1],
                               L, last)
    return S
```

```python
import functools
import math

import jax
import jax.numpy as jnp
from jax import lax
from jax.experimental import pallas as pl
from jax.experimental.pallas import tpu as pltpu

F32 = jnp.float32
BF16 = jnp.bfloat16

DEPTH = 4
N_MIXERS = 3
GRID_W = 64
HEAD_DIM = 64
ROPE_BASE = 10000.0
DA_HEADS = 8
WA_Q_HEADS = 16
WA_KV_HEADS = 4
WA_REP = WA_Q_HEADS // WA_KV_HEADS
WINDOW = 128
SSM_GROUP = 16
SSM_STATE = 64
PEER_HEADS = 8
PEER_NKEYS = 128
PEER_QDIM = 256
PEER_TOPK = 16
LN_EPS = 1e-5
DEEPNORM_ALPHA = (2 * DEPTH) ** 0.25
ATT_SCALE = HEAD_DIM ** -0.5

LANES = 128
SUBLANES = 8
ROW_TILE = 256
WIN_TILE = 128
SCAN_CHUNK = 256
SSM_SLAB_GROUPS = 8
DA_HEADS_PER_STEP = 4
PEER_TOK_BLOCK = 32
PEER_TOK_GROUP = 8
PEER_LOOKAHEAD = 2
EXPERT_RELAYOUT_BLOCK = 512
VMEM_LIMIT = 48 * 1024 * 1024
NEG_INF = float("-inf")


def _cparams(n_axes):
    return pltpu.CompilerParams(dimension_semantics=("arbitrary",) * n_axes, vmem_limit_bytes=VMEM_LIMIT)


def _gelu_tanh(x):
    return 0.5 * x * (1.0 + jnp.tanh(math.sqrt(2.0 / math.pi) * (x + 0.044715 * (x * x * x))))


def _sigmoid(x):
    return 1.0 / (1.0 + jnp.exp(-x))


def _layer_norm_rows(z, g, b):
    mu = jnp.mean(z, axis=-1, keepdims=True)
    zc = z - mu
    var = jnp.mean(zc * zc, axis=-1, keepdims=True)
    return zc * lax.rsqrt(var + LN_EPS) * g + b


def _modulate(x, m, shift_idx):
    return x * (1.0 + m[shift_idx + 1:shift_idx + 2]) + m[shift_idx:shift_idx + 1]


def _mod_kernel(a_ref, w_ref, b_ref, o_ref):
    a = a_ref[...]
    a = a * _sigmoid(a)
    o_ref[0] = jnp.dot(a, w_ref[0], preferred_element_type=F32, precision=lax.Precision.HIGHEST) + b_ref[0]


def _mod_vectors(c, c_ctx, mod_w, mod_b):
    B, D = c.shape
    depth, _, n6 = mod_w.shape
    rows = -(-(B + 1) // SUBLANES) * SUBLANES
    a = jnp.zeros((rows, D), F32).at[:B].set(c).at[B].set(c_ctx)
    tn = n6 // 4
    out = pl.pallas_call(
        _mod_kernel,
        grid=(depth, n6 // tn),
        in_specs=[pl.BlockSpec((rows, D), lambda i, j: (0, 0)),
                  pl.BlockSpec((1, D, tn), lambda i, j: (i, 0, j)),
                  pl.BlockSpec((1, 1, tn), lambda i, j: (i, 0, j))],
        out_specs=pl.BlockSpec((1, rows, tn), lambda i, j: (i, 0, j)),
        out_shape=jax.ShapeDtypeStruct((depth, rows, n6), F32),
        compiler_params=_cparams(2),
    )(a, mod_w, mod_b.reshape(depth, 1, n6))
    lat = out[:, :B].reshape(depth, B, 6, D)
    ctx = jnp.broadcast_to(out[:, B].reshape(depth, 1, 6, D), (depth, B, 6, D))
    m = jnp.stack([ctx, lat], axis=2)
    return jnp.pad(m, ((0, 0), (0, 0), (0, 0), (0, 2), (0, 0)))


def _mod_spec(D, nctx_blocks, j0):
    return pl.BlockSpec((1, 1, SUBLANES, D), lambda b, j: (b, jnp.minimum((j + j0) // nctx_blocks, 1), 0, 0))


def _proj_kernel(s_ref, m_ref, w_ref, cos_ref, sin_ref, o_ref, *, n_rope, tn):
    xb = _modulate(s_ref[0], m_ref[0, 0], 0).astype(BF16)
    tm = xb.shape[0]
    n_out = w_ref.shape[1]
    cos_t = cos_ref[...]
    sin_t = sin_ref[...]
    lane = lax.broadcasted_iota(jnp.int32, (tm, LANES), 1)
    first_half = (lane % HEAD_DIM) < (HEAD_DIM // 2)
    for c0 in range(0, n_out, tn):
        y = jnp.dot(xb, w_ref[:, c0:c0 + tn], preferred_element_type=F32)
        if c0 < n_rope:
            pieces = []
            for l0 in range(0, tn, LANES):
                yc = y[:, l0:l0 + LANES]
                partner = jnp.where(first_half,
                                    pltpu.roll(yc, LANES - HEAD_DIM // 2, 1),
                                    pltpu.roll(yc, HEAD_DIM // 2, 1))
                pieces.append(yc * cos_t + partner * sin_t)
            y = jnp.concatenate(pieces, axis=1)
        o_ref[0, :, c0:c0 + tn] = y.astype(o_ref.dtype)


def _project(S, M, w, cos_t, sin_t, n_rope, ctx_len):
    B, P, D = S.shape
    n_out = w.shape[1]
    tm = ROW_TILE
    kern = functools.partial(_proj_kernel, n_rope=n_rope, tn=512)
    return pl.pallas_call(
        kern,
        grid=(B, P // tm),
        in_specs=[pl.BlockSpec((1, tm, D), lambda b, j: (b, j, 0)),
                  _mod_spec(D, ctx_len // tm, 0),
                  pl.BlockSpec((D, n_out), lambda b, j: (0, 0)),
                  pl.BlockSpec((tm, LANES), lambda b, j: (j, 0)),
                  pl.BlockSpec((tm, LANES), lambda b, j: (j, 0))],
        out_specs=pl.BlockSpec((1, tm, n_out), lambda b, j: (b, j, 0)),
        out_shape=jax.ShapeDtypeStruct((B, P, n_out), BF16),
        compiler_params=_cparams(2),
    )(S, M, w, cos_t, sin_t)


def _dattn_kernel(lam_ref, g_ref, q_ref, k_ref, v_ref, o_ref, *, lam_init, ctx_len, ctx_queries):
    lp = lam_ref[...]
    lam = (jnp.exp(jnp.sum(lp[0:1] * lp[1:2], axis=-1, keepdims=True))
           - jnp.exp(jnp.sum(lp[2:3] * lp[3:4], axis=-1, keepdims=True)) + lam_init)
    n_heads = q_ref.shape[2] // LANES
    lane = lax.broadcasted_iota(jnp.int32, (q_ref.shape[1], LANES), 1)

    def attend_head(hh, nk):
        cols = slice(hh * LANES, (hh + 1) * LANES)
        q = q_ref[0, :, cols] * ATT_SCALE
        zero = jnp.zeros_like(q)
        k = k_ref[0, 0:nk, cols]
        v = v_ref[0, 0:nk, cols]
        probs = []
        for qm in (jnp.where(lane < HEAD_DIM, q, zero), jnp.where(lane >= HEAD_DIM, q, zero)):
            s = lax.dot_general(qm, k, (((1,), (1,)), ((), ())), preferred_element_type=F32)
            p = jnp.exp(s - jnp.max(s, axis=-1, keepdims=True))
            probs.append((p, 1.0 / jnp.sum(p, axis=-1, keepdims=True)))
        a = probs[0][0] * probs[0][1] - probs[1][0] * (lam * probs[1][1])
        o = jnp.dot(a.astype(BF16), v, preferred_element_type=F32)
        o = o * lax.rsqrt(jnp.mean(o * o, axis=-1, keepdims=True) + LN_EPS) * g_ref[...] * (1.0 - lam_init)
        o_ref[0, :, cols] = o.astype(o_ref.dtype)

    def attend(nk):
        for hh in range(n_heads):
            attend_head(hh, nk)

    if ctx_queries:
        qi = pl.program_id(2)

        @pl.when(qi == 0)
        def _():
            attend(ctx_len)

        @pl.when(qi > 0)
        def _():
            attend(k_ref.shape[1])
    else:
        attend(k_ref.shape[1])


def _diff_attention(qkv, lam_p, subln_g, lam_init, ctx_len, ctx_queries):
    B, P, _ = qkv.shape
    tq = ROW_TILE
    assert ctx_len == tq
    j0 = 0 if ctx_queries else 1
    H = DA_HEADS
    hg = H // DA_HEADS_PER_STEP
    width = DA_HEADS_PER_STEP * LANES
    kern = functools.partial(_dattn_kernel, lam_init=lam_init, ctx_len=ctx_len, ctx_queries=ctx_queries)
    return pl.pallas_call(
        kern,
        grid=(B, hg, P // tq - j0),
        in_specs=[pl.BlockSpec((4, HEAD_DIM), lambda b, h, i: (0, 0)),
                  pl.BlockSpec((1, 2 * HEAD_DIM), lambda b, h, i: (0, 0)),
                  pl.BlockSpec((1, tq, width), lambda b, h, i: (b, i + j0, h)),
                  pl.BlockSpec((1, P, width), lambda b, h, i: (b, 0, hg + h)),
                  pl.BlockSpec((1, P, width), lambda b, h, i: (b, 0, 2 * hg + h))],
        out_specs=pl.BlockSpec((1, tq, width), lambda b, h, i: (b, i + j0, h)),
        out_shape=jax.ShapeDtypeStruct((B, P, H * LANES), BF16),
        compiler_params=_cparams(3),
    )(lam_p, subln_g.reshape(1, -1), qkv, qkv, qkv)


def _wattn_kernel(sink_ref, q_ref, kc_ref, k0_ref, k1_ref, k2_ref, vc_ref, v0_ref, v1_ref, v2_ref, o_ref,
                  *, n_ctx_blocks, n_lat_blocks, j0):
    qb = pl.program_id(1) + j0
    tq = q_ref.shape[1]
    width = WA_REP * HEAD_DIM
    n_kv = q_ref.shape[2] // width
    lane = lax.broadcasted_iota(jnp.int32, (tq, width), 1)
    head_sel = [(lane >= r * HEAD_DIM) & (lane < (r + 1) * HEAD_DIM) for r in range(WA_REP)]
    dims = (((1,), (1,)), ((), ()))

    def queries(g):
        q = q_ref[0, :, g * width:(g + 1) * width] * ATT_SCALE
        zero = jnp.zeros_like(q)
        qm = jnp.concatenate([jnp.where(head_sel[r], q, zero) for r in range(WA_REP)], axis=0)
        sink_col = jnp.concatenate([jnp.full((tq, 1), sink_ref[g * WA_REP + r], F32) for r in range(WA_REP)],
                                   axis=0)
        return qm, sink_col

    def finish(g, o_all, denom):
        o_all = o_all / denom
        o = jnp.zeros((tq, width), F32)
        for r in range(WA_REP):
            o = o + jnp.where(head_sel[r], o_all[r * tq:(r + 1) * tq, :], 0.0)
        o_ref[0, :, g * width:(g + 1) * width] = o.astype(o_ref.dtype)

    @pl.when(qb < n_ctx_blocks)
    def _():
        for g in range(n_kv):
            cols = slice(g * width, (g + 1) * width)
            qm, sink_col = queries(g)
            s_c = lax.dot_general(qm, kc_ref[0, :, cols], dims, preferred_element_type=F32)
            m = jnp.maximum(jnp.max(s_c, axis=-1, keepdims=True), sink_col)
            p_c = jnp.exp(s_c - m)
            denom = jnp.sum(p_c, axis=-1, keepdims=True) + jnp.exp(sink_col - m)
            finish(g, jnp.dot(p_c.astype(BF16), vc_ref[0, :, cols], preferred_element_type=F32), denom)

    @pl.when(qb >= n_ctx_blocks)
    def _():
        lb = qb - n_ctx_blocks
        iq = lax.broadcasted_iota(jnp.int32, (WA_REP * tq, 3 * tq), 0) % tq
        ik = lax.broadcasted_iota(jnp.int32, (WA_REP * tq, 3 * tq), 1)
        lo = jnp.where(lb == 0, tq, 0)
        hi = jnp.where(lb == n_lat_blocks - 1, 2 * tq, 3 * tq)
        band = (ik >= iq) & (ik <= iq + 2 * WINDOW) & (ik >= lo) & (ik < hi)
        for g in range(n_kv):
            cols = slice(g * width, (g + 1) * width)
            qm, sink_col = queries(g)
            kw = jnp.concatenate([k0_ref[0, :, cols], k1_ref[0, :, cols], k2_ref[0, :, cols]], axis=0)
            vw = jnp.concatenate([v0_ref[0, :, cols], v1_ref[0, :, cols], v2_ref[0, :, cols]], axis=0)
            s_c = lax.dot_general(qm, kc_ref[0, :, cols], dims, preferred_element_type=F32)
            s_w = jnp.where(band, lax.dot_general(qm, kw, dims, preferred_element_type=F32), NEG_INF)
            m = jnp.maximum(jnp.maximum(jnp.max(s_c, axis=-1, keepdims=True),
                                        jnp.max(s_w, axis=-1, keepdims=True)), sink_col)
            p_c = jnp.exp(s_c - m)
            p_w = jnp.exp(s_w - m)
            denom = (jnp.sum(p_c, axis=-1, keepdims=True) + jnp.sum(p_w, axis=-1, keepdims=True)
                     + jnp.exp(sink_col - m))
            o_all = (jnp.dot(p_c.astype(BF16), vc_ref[0, :, cols], preferred_element_type=F32)
                     + jnp.dot(p_w.astype(BF16), vw, preferred_element_type=F32))
            finish(g, o_all, denom)


def _window_attention(qkv, sink, ctx_len, ctx_queries):
    B, P, _ = qkv.shape
    tq = WIN_TILE
    assert tq == WINDOW
    nctx = ctx_len // tq
    nlat = P // tq - nctx
    j0 = 0 if ctx_queries else nctx
    full = WA_KV_HEADS * WA_REP * HEAD_DIM

    def lat_spec(off, region):
        return pl.BlockSpec((1, tq, full),
                            lambda b, j: (b, jnp.clip(j + j0 - nctx + off, 0, nlat - 1) + nctx, region))

    row = lambda b, j: (b, j + j0, 0)
    kern = functools.partial(_wattn_kernel, n_ctx_blocks=nctx, n_lat_blocks=nlat, j0=j0)
    return pl.pallas_call(
        kern,
        grid=(B, P // tq - j0),
        in_specs=([pl.BlockSpec(memory_space=pltpu.SMEM), pl.BlockSpec((1, tq, full), row),
                   pl.BlockSpec((1, ctx_len, full), lambda b, j: (b, 0, 1))]
                  + [lat_spec(off, 1) for off in (-1, 0, 1)]
                  + [pl.BlockSpec((1, ctx_len, full), lambda b, j: (b, 0, 2))]
                  + [lat_spec(off, 2) for off in (-1, 0, 1)]),
        out_specs=pl.BlockSpec((1, tq, full), row),
        out_shape=jax.ShapeDtypeStruct((B, P, full), BF16),
        compiler_params=_cparams(2),
    )(sink, qkv, qkv, qkv, qkv, qkv, qkv, qkv, qkv, qkv)


def _resid_ln_kernel(o_ref, w_ref, s_ref, m_ref, g_ref, b_ref, out_ref):
    y = jnp.dot(o_ref[0], w_ref[...], preferred_element_type=F32)
    z = DEEPNORM_ALPHA * s_ref[0] + m_ref[0, 0][2:3] * y
    out_ref[0] = _layer_norm_rows(z, g_ref[...], b_ref[...])


def _out_proj_resid_ln(O, w, S, M, ln_g, ln_b, ctx_len, skip_ctx):
    B, P, D = S.shape
    kdim = O.shape[2]
    tm = ROW_TILE
    j0 = ctx_len // tm if skip_ctx else 0
    row = lambda b, j: (b, j + j0, 0)
    return pl.pallas_call(
        _resid_ln_kernel,
        grid=(B, P // tm - j0),
        in_specs=[pl.BlockSpec((1, tm, kdim), row),
                  pl.BlockSpec((kdim, D), lambda b, j: (0, 0)),
                  pl.BlockSpec((1, tm, D), row),
                  _mod_spec(D, ctx_len // tm, j0),
                  pl.BlockSpec((1, D), lambda b, j: (0, 0)),
                  pl.BlockSpec((1, D), lambda b, j: (0, 0))],
        out_specs=pl.BlockSpec((1, tm, D), row),
        out_shape=jax.ShapeDtypeStruct((B, P, D), F32),
        input_output_aliases={2: 0},
        compiler_params=_cparams(2),
    )(O, w, S, M, ln_g.reshape(1, D), ln_b.reshape(1, D))


def _s5_kernel(s_ref, mt_ref, bm_ref, cm_ref, lr_ref, li_ref, y_ref, xbuf, state):
    d = pl.program_id(0)
    c = pl.program_id(2)
    tc, nb, w_in = s_ref.shape
    half = lr_ref.shape[-1]

    @pl.when(c == 0)
    def _():
        state[...] = jnp.zeros_like(state)

    mt = mt_ref[0]
    u = s_ref[...] * (1.0 + mt[1][None]) + mt[0][None]
    u2 = u.reshape(tc * nb, w_in).astype(BF16)
    xbuf[...] = jnp.dot(u2, bm_ref[0, 0], preferred_element_type=F32)
    lr = jnp.broadcast_to(lr_ref[0, 0], (nb, half))
    li = jnp.broadcast_to(li_ref[0, 0], (nb, half))

    def step(t, carry):
        sr, si = carry
        tt = jnp.where(d == 0, t, tc - 1 - t)
        r0 = pl.multiple_of(tt * nb, nb)
        nr = lr * sr - li * si + xbuf[pl.ds(r0, nb), 0:half]
        ni = lr * si + li * sr + xbuf[pl.ds(r0, nb), half:2 * half]
        xbuf[pl.ds(r0, nb), 0:half] = nr
        xbuf[pl.ds(r0, nb), half:2 * half] = ni
        return nr, ni

    sr, si = lax.fori_loop(0, tc, step, (state[:, 0:half], state[:, half:2 * half]), unroll=4)
    state[:, 0:half] = sr
    state[:, half:2 * half] = si
    y = jnp.dot(xbuf[...].astype(BF16), cm_ref[0, 0], preferred_element_type=F32)
    y_ref[0] = y.reshape(tc, nb, w_in)


def _s5_scan(S_tm, Mt, bmat, cmat, lam_r, lam_i, ctx_len):
    P, B, D = S_tm.shape
    tc = SCAN_CHUNK
    w_in = SSM_SLAB_GROUPS * SSM_GROUP
    n_slab = D // w_in
    half = SSM_SLAB_GROUPS * SSM_STATE
    nch = P // tc
    nctx = ctx_len // tc
    nlat = nch - nctx

    def chunk(d, c):
        q = nch - 1 - c
        back = jnp.where(q < nlat, q + nctx, q - nlat)
        return jnp.where(d == 0, c, back)

    return pl.pallas_call(
        _s5_kernel,
        grid=(2, n_slab, nch),
        in_specs=[pl.BlockSpec((tc, B, w_in), lambda d, s, c: (chunk(d, c), 0, s)),
                  pl.BlockSpec((1, 2, B, w_in), lambda d, s, c: (jnp.minimum(chunk(d, c) // nctx, 1), 0, 0, s)),
                  pl.BlockSpec((1, 1, w_in, 2 * half), lambda d, s, c: (d, s, 0, 0)),
                  pl.BlockSpec((1, 1, 2 * half, w_in), lambda d, s, c: (d, s, 0, 0)),
                  pl.BlockSpec((1, 1, 1, half), lambda d, s, c: (d, s, 0, 0)),
                  pl.BlockSpec((1, 1, 1, half), lambda d, s, c: (d, s, 0, 0))],
        out_specs=pl.BlockSpec((1, tc, B, w_in), lambda d, s, c: (d, chunk(d, c), 0, s)),
        out_shape=jax.ShapeDtypeStruct((2, P, B, D), F32),
        scratch_shapes=[pltpu.VMEM((tc * B, 2 * half), F32), pltpu.VMEM((B, 2 * half), F32)],
        compiler_params=_cparams(3),
    )(S_tm, Mt, bmat, cmat, lam_r, lam_i)


def _s5_params(lam_re, lam_im, log_step, b_re, b_im, c_re, c_im):
    lam = lax.complex(lam_re.astype(F32), lam_im.astype(F32))
    step = jnp.exp(log_step.astype(F32))[..., None]
    lam_bar = jnp.exp(lam * step)
    b_bar = lax.complex(b_re.astype(F32), b_im.astype(F32)) * ((lam_bar - 1) / lam)[..., None]
    G, Pst, Hg = b_bar.shape[1:]
    ng = SSM_SLAB_GROUPS
    ns = G // ng
    eye = jnp.eye(ng, dtype=F32)

    def bdiag_in(x):
        x = x.reshape(2, ns, ng, Pst, Hg)
        return jnp.einsum('rsgph,gk->rsghkp', x, eye).reshape(2, ns, ng * Hg, ng * Pst)

    def bdiag_out(x):
        x = x.reshape(2, ns, ng, Hg, Pst)
        return jnp.einsum('rsghp,gk->rsgpkh', x, eye).reshape(2, ns, ng * Pst, ng * Hg)

    bmat = jnp.concatenate([bdiag_in(jnp.real(b_bar)), bdiag_in(jnp.imag(b_bar))], axis=-1).astype(BF16)
    cmat = jnp.concatenate([bdiag_out(c_re.astype(F32)), bdiag_out(-c_im.astype(F32))], axis=-2).astype(BF16)
    lam_r = jnp.real(lam_bar).reshape(2, ns, 1, ng * Pst)
    lam_i = jnp.imag(lam_bar).reshape(2, ns, 1, ng * Pst)
    return bmat, cmat, lam_r, lam_i


def _glu_ln_kernel(s_ref, y0_ref, y1_ref, d_ref, w_ref, m_ref, g_ref, b_ref, out_ref):
    m = m_ref[0, 0]
    s = s_ref[0]
    y = d_ref[...] * _modulate(s, m, 0) + y0_ref[0, 0] + y1_ref[0, 0]
    z = jnp.dot(_gelu_tanh(y).astype(BF16), w_ref[...], preferred_element_type=F32)
    D = s.shape[1]
    o = z[:, 0:D] * _sigmoid(z[:, D:2 * D])
    out_ref[0] = _layer_norm_rows(DEEPNORM_ALPHA * s + m[2:3] * o, g_ref[...], b_ref[...])


def _glu_resid_ln(S, Y, d_skip, w_glu, M, ln_g, ln_b, ctx_len):
    B, P, D = S.shape
    tm = ROW_TILE
    row = lambda b, j: (b, j, 0)
    return pl.pallas_call(
        _glu_ln_kernel,
        grid=(B, P // tm),
        in_specs=[pl.BlockSpec((1, tm, D), row),
                  pl.BlockSpec((1, 1, tm, D), lambda b, j: (0, b, j, 0)),
                  pl.BlockSpec((1, 1, tm, D), lambda b, j: (1, b, j, 0)),
                  pl.BlockSpec((1, D), lambda b, j: (0, 0)),
                  pl.BlockSpec((D, 2 * D), lambda b, j: (0, 0)),
                  _mod_spec(D, ctx_len // tm, 0),
                  pl.BlockSpec((1, D), lambda b, j: (0, 0)),
                  pl.BlockSpec((1, D), lambda b, j: (0, 0))],
        out_specs=pl.BlockSpec((1, tm, D), row),
        out_shape=jax.ShapeDtypeStruct((B, P, D), F32),
        compiler_params=_cparams(2),
    )(S, Y, Y, d_skip.reshape(1, D), w_glu, M, ln_g.reshape(1, D), ln_b.reshape(1, D))


def _peer_topk_kernel(s_ref, m_ref, wqt_ref, keys_ref, g_out, e_out, qt):
    tn = s_ref.shape[1]
    nk = PEER_NKEYS
    k = PEER_TOPK
    xb = _modulate(s_ref[0], m_ref[0, 0], 3).astype(BF16)
    qt[...] = lax.dot_general(wqt_ref[...], xb, (((1,), (1,)), ((), ())), preferred_element_type=F32)

    def head(h, carry):
        krow = lax.broadcasted_iota(jnp.int32, (nk, tn), 0).astype(F32)
        rank = lax.broadcasted_iota(jnp.int32, (k, tn), 0)
        sub = lax.broadcasted_iota(jnp.int32, (SUBLANES, tn), 0)
        subf = sub.astype(F32)
        tops = []
        for j in range(2):
            qs = qt[pl.ds(pl.multiple_of(h * 2 * nk + j * nk, nk), nk), :].astype(BF16)
            s = jnp.dot(keys_ref[j, h], qs, preferred_element_type=F32)
            top = []
            for it in range(k):
                mx = jnp.max(s, axis=0, keepdims=True)
                ix = jnp.min(jnp.where(s == mx, krow, float(nk)), axis=0, keepdims=True)
                s = jnp.where(krow == ix, NEG_INF, s)
                top.append((mx, ix))
            tops.append(top)
        def stack8(items):
            out = jnp.zeros((SUBLANES, tn), F32)
            for s_i, item in enumerate(items):
                out = jnp.where(sub == s_i, item, out)
            return out

        def pack(first, second, comb):
            a_lo = first[:SUBLANES]
            a_hi = pltpu.roll(stack8(first[SUBLANES:]), 2, 0)
            b_lo = stack8(second[:SUBLANES])
            b_hi = stack8(second[SUBLANES:])
            b_0 = second[0]
            return [comb(a_lo[0], b_lo), comb(a_lo[0], b_hi), comb(a_lo[1], b_lo),
                    jnp.where(sub < 5, comb(a_lo[2], b_lo), comb(a_lo[4], pltpu.roll(b_lo, 5, 0))),
                    jnp.where(sub < 4, comb(a_lo[3], b_lo),
                              jnp.where(sub < 6, comb(a_lo[5], pltpu.roll(b_lo, 4, 0)),
                                        comb(a_lo[6], pltpu.roll(b_lo, 6, 0)))),
                    jnp.where(sub < 2, comb(a_lo[7], b_lo), comb(a_hi, b_0)),
                    comb(a_hi, b_0)]

        vals = pack([t[0] for t in tops[0]], [t[0] for t in tops[1]], lambda a, b: a + b)
        vals[6] = jnp.where(sub < 2, vals[6], NEG_INF)
        cand = jnp.concatenate(vals, axis=0)
        cexp = jnp.concatenate(pack([t[1] for t in tops[0]], [t[1] for t in tops[1]],
                                    lambda a, b: a * float(nk) + b), axis=0)
        pos_const = jnp.concatenate(
            [subf, 8.0 + subf, 16.0 + subf,
             jnp.where(sub < 5, 32.0 + subf, 59.0 + subf),
             jnp.where(sub < 4, 48.0 + subf, jnp.where(sub < 6, 76.0 + subf, 90.0 + subf)),
             jnp.where(sub < 2, 112.0 + subf, 16.0 * (subf + 6.0)),
             16.0 * (subf + 14.0)], axis=0)
        ts = jnp.zeros((k, tn), F32)
        te = jnp.zeros((k, tn), F32)
        best = None
        for it in range(k):
            mx = jnp.max(cand, axis=0, keepdims=True)
            sel = jnp.min(jnp.where(cand == mx, pos_const, float(k * k)), axis=0, keepdims=True)
            hit = pos_const == sel
            ex = jnp.max(jnp.where(hit, cexp, -1.0), axis=0, keepdims=True)
            cand = jnp.where(hit, NEG_INF, cand)
            ts = jnp.where(rank == it, mx, ts)
            te = jnp.where(rank == it, ex, te)
            best = mx if best is None else best
        p = jnp.exp(ts - best)
        gate = p / jnp.sum(p, axis=0, keepdims=True)
        r0 = pl.multiple_of(h * k, k)
        g_out[0, pl.ds(r0, k), :] = gate
        e_out[0, pl.ds(r0, k), :] = te.astype(jnp.int32)
        return carry

    lax.fori_loop(0, PEER_HEADS, head, 0, unroll=4)


def _peer_topk(S, M, wq_t, keys, ctx_len, skip_ctx):
    B, P, D = S.shape
    tn = ROW_TILE
    j0 = ctx_len // tn if skip_ctx else 0
    nblk = P // tn - j0
    hk = PEER_HEADS * PEER_TOPK
    out_spec = pl.BlockSpec((1, hk, tn), lambda b, j: (b, 0, j))
    return pl.pallas_call(
        _peer_topk_kernel,
        grid=(B, nblk),
        in_specs=[pl.BlockSpec((1, tn, D), lambda b, j: (b, j + j0, 0)),
                  _mod_spec(D, ctx_len // tn, j0),
                  pl.BlockSpec(wq_t.shape, lambda b, j: (0, 0)),
                  pl.BlockSpec(keys.shape, lambda b, j: (0, 0, 0, 0))],
        out_specs=[out_spec, out_spec],
        out_shape=[jax.ShapeDtypeStruct((B, hk, nblk * tn), F32),
                   jax.ShapeDtypeStruct((B, hk, nblk * tn), jnp.int32)],
        scratch_shapes=[pltpu.VMEM((wq_t.shape[0], tn), F32)],
        compiler_params=_cparams(2),
    )(S, M, wq_t, keys)


def _sublane_sums(vregs):
    sub = lax.broadcasted_iota(jnp.int32, (SUBLANES, LANES), 0)
    level = list(vregs)
    half = SUBLANES // 2
    while half >= 1:
        lower = (sub % (2 * half)) < half
        nxt = []
        for k in range(len(level) // 2):
            a, b = level[k], level[k + len(level) // 2]
            stay = jnp.where(lower, a, b)
            move = jnp.where(lower, b, a)
            if 2 * half == SUBLANES:
                moved = pltpu.roll(move, half, 0)
            else:
                moved = jnp.where(lower, pltpu.roll(move, SUBLANES - half, 0), pltpu.roll(move, half, 0))
            nxt.append(stay + moved)
        level = nxt
        half //= 2
    return level[0]


def _sublane_transpose(vregs):
    sub = lax.broadcasted_iota(jnp.int32, (SUBLANES, LANES), 0)
    v = list(vregs)
    half = SUBLANES // 2
    while half >= 1:
        clear = (sub & half) == 0
        for i in range(SUBLANES):
            if i & half == 0:
                a, b = v[i], v[i + half]
                v[i] = jnp.where(clear, a, pltpu.roll(b, half, 0))
                v[i + half] = jnp.where(clear, pltpu.roll(a, SUBLANES - half, 0), b)
        half //= 2
    return v


def _peer_gather_kernel(idx_ref, nxt_ref, g_ref, s_ref, m_ref, lg_ref, lb_ref, tab_ref, out_ref, buf, sem, obuf,
                        cbuf):
    i = pl.program_id(0)
    n_steps = pl.num_programs(0)
    tb = s_ref.shape[0]
    hk = g_ref.shape[1]
    gt = PEER_TOK_GROUP
    rows = gt * hk
    n_groups = tb // gt
    pgs = hk // SUBLANES

    def row_copy(e, slot, r, prio):
        pltpu.make_async_copy(tab_ref.at[e], buf.at[slot, r], sem.at[slot]).start(priority=prio)

    def issue(ref, grp, slot):
        per_it = 4 * SUBLANES

        def body(it, carry):
            for k in range(per_it):
                row_copy(ref[grp * rows + it * per_it + k], slot, it * per_it + k, k % 2)
            return carry

        lax.fori_loop(0, rows // per_it, body, 0)

    def issue_token(ref, grp, slot, t, part, n_parts):
        for r in range(t * hk + part * hk // n_parts, t * hk + (part + 1) * hk // n_parts):
            row_copy(ref[grp * rows + r], slot, r, r % 2)

    def wait(slot):
        pltpu.make_async_copy(buf.at[(slot + 1) % n_groups], buf.at[slot], sem.at[slot]).wait()

    @pl.when(i == 0)
    def _():
        for g0 in range(PEER_LOOKAHEAD):
            issue(idx_ref, g0, g0)

    m = m_ref[0]
    s_tok = []
    for t0 in range(0, tb, SUBLANES):
        s_tok += _sublane_transpose([s_ref[t0:t0 + SUBLANES, j * LANES:(j + 1) * LANES] for j in range(SUBLANES)])
    s_rows = jnp.stack(s_tok, axis=0)
    h_rows = s_rows * (1.0 + m[4][None]) + m[3][None]
    gates = g_ref[0]
    for grp in range(n_groups):
        slot = grp
        ahead = grp + PEER_LOOKAHEAD
        ahead_ref, ahead_grp = (idx_ref, ahead) if ahead < n_groups else (nxt_ref, ahead - n_groups)
        wait(slot)

        n_parts = 2 * pgs

        def issue_part(t_issue, part):
            if t_issue is not None:
                issue_token(ahead_ref, ahead_grp, ahead % n_groups, t_issue, part, n_parts)

        def expert_scores(t, t_issue):
            h_t = h_rows[grp * gt + t]
            scs = []
            for q in range(pgs):
                issue_part(t_issue, q)
                r0 = t * hk + q * SUBLANES
                part = _sublane_sums([buf[slot, r0 + k, 0] * h_t for k in range(SUBLANES)])
                scs.append(jnp.sum(part, axis=1, keepdims=True))
            return jnp.concatenate(scs, axis=0)

        def expert_mix(t, sc, t_issue):
            tok = grp * gt + t
            cbuf[tok] = jnp.broadcast_to(gates[:, tok:tok + 1] * _gelu_tanh(sc), (hk, LANES))
            accs = [None] * 4
            for r in range(hk):
                if r % SUBLANES == 0:
                    issue_part(t_issue, pgs + r // SUBLANES)
                term = jnp.broadcast_to(cbuf[tok, r:r + 1, :], (SUBLANES, LANES)) * buf[slot, t * hk + r, 1]
                accs[r % 4] = term if accs[r % 4] is None else accs[r % 4] + term
            obuf[tok] = (accs[0] + accs[1]) + (accs[2] + accs[3])

        sc_prev = expert_scores(0, None)
        for t in range(gt):
            if t + 1 < gt:
                sc_next = expert_scores(t + 1, t)
            else:
                sc_next = None
                for part in range(pgs):
                    issue_part(t, part)
            expert_mix(t, sc_prev, t)
            sc_prev = sc_next
    z = DEEPNORM_ALPHA * s_rows + m[5][None] * obuf[...]
    n_el = z.shape[1] * z.shape[2]
    mu = jnp.sum(jnp.sum(z, axis=2, keepdims=True), axis=1, keepdims=True) / n_el
    zc = z - mu
    var = jnp.sum(jnp.sum(zc * zc, axis=2, keepdims=True), axis=1, keepdims=True) / n_el
    y = zc * lax.rsqrt(var + LN_EPS) * lg_ref[...][None] + lb_ref[...][None]
    for t0 in range(0, tb, SUBLANES):
        tiles = _sublane_transpose([y[t0 + t] for t in range(SUBLANES)])
        for j in range(SUBLANES):
            out_ref[t0:t0 + SUBLANES, j * LANES:(j + 1) * LANES] = tiles[j]

    @pl.when(i == n_steps - 1)
    def _():
        for slot in range(PEER_LOOKAHEAD):
            wait(slot)


def _peer_gather(S, M, gates_t, eidx, table, ln_g, ln_b, ctx_len, skip_ctx):
    B, P, D = S.shape
    tb = PEER_TOK_BLOCK
    n_slots = tb // PEER_TOK_GROUP
    assert PEER_LOOKAHEAD < n_slots
    hk = PEER_HEADS * PEER_TOPK
    j0 = ctx_len // tb if skip_ctx else 0
    per_b = P // tb - j0
    n_steps = B * per_b
    nctx = ctx_len // tb
    sub = D // LANES
    assert sub == SUBLANES
    S2 = S.reshape(B * P, D)
    M3 = M.reshape(B * 2, SUBLANES, sub, LANES)
    row_blk = lambda i: (i // per_b) * (P // tb) + i % per_b + j0
    out = pl.pallas_call(
        _peer_gather_kernel,
        grid=(n_steps,),
        in_specs=[pl.BlockSpec((tb * hk,), lambda i: (i,), memory_space=pltpu.SMEM),
                  pl.BlockSpec((tb * hk,), lambda i: (jnp.minimum(i + 1, n_steps - 1),), memory_space=pltpu.SMEM),
                  pl.BlockSpec((1, hk, tb), lambda i: (i, 0, 0)),
                  pl.BlockSpec((tb, D), lambda i: (row_blk(i), 0)),
                  pl.BlockSpec((1, SUBLANES, sub, LANES),
                               lambda i: ((i // per_b) * 2 + jnp.minimum((i % per_b + j0) // nctx, 1), 0, 0, 0)),
                  pl.BlockSpec((sub, LANES), lambda i: (0, 0)),
                  pl.BlockSpec((sub, LANES), lambda i: (0, 0)),
                  pl.BlockSpec(memory_space=pl.ANY)],
        out_specs=pl.BlockSpec((tb, D), (lambda i: (i, 0)) if skip_ctx else (lambda i: (row_blk(i), 0))),
        out_shape=jax.ShapeDtypeStruct((n_steps * tb if skip_ctx else B * P, D), F32),
        scratch_shapes=[pltpu.VMEM((n_slots, PEER_TOK_GROUP * hk, 2, sub, LANES), F32),
                        pltpu.SemaphoreType.DMA((n_slots,)),
                        pltpu.VMEM((tb, sub, LANES), F32),
                        pltpu.VMEM((tb, hk, LANES), F32)],
        input_output_aliases={} if skip_ctx else {3: 0},
        compiler_params=_cparams(1),
    )(eidx, eidx, gates_t, S2, M3, ln_g.reshape(sub, LANES), ln_b.reshape(sub, LANES), table)
    return out.reshape(B, -1, D)


def _peer_ffn_resid_ln(S, M, wq, keys, u_all, v_all, layer, ln_g, ln_b, ctx_len, skip_ctx):
    B, P, D = S.shape
    wq_t = wq.T.astype(BF16)
    gates, eidx = _peer_topk(S, M, wq_t, keys.astype(BF16), ctx_len, skip_ctx)
    hk = gates.shape[1]
    tb = PEER_TOK_BLOCK
    gates_t = jnp.transpose(gates.reshape(B, hk, -1, tb), (0, 2, 1, 3)).reshape(-1, hk, tb)
    eidx_flat = jnp.transpose(eidx, (0, 2, 1)).reshape(-1)
    table = _expert_table(u_all, v_all, layer)
    return _peer_gather(S, M, gates_t, eidx_flat, table, ln_g, ln_b, ctx_len, skip_ctx)


def _expert_table_kernel(u_ref, v_ref, o_ref):
    n_groups = u_ref.shape[0] // SUBLANES

    def group(g, carry):
        r0 = pl.multiple_of(g * SUBLANES, SUBLANES)
        for half, ref in enumerate((u_ref, v_ref)):
            rows = _sublane_transpose([ref[pl.ds(r0, SUBLANES), j * LANES:(j + 1) * LANES] for j in range(SUBLANES)])
            for k in range(SUBLANES):
                o_ref[r0 + k, half] = rows[k]
        return carry

    lax.fori_loop(0, n_groups, group, 0)


def _expert_table(u_all, v_all, layer):
    _, E, D = u_all.shape
    assert D == SUBLANES * LANES
    te = EXPERT_RELAYOUT_BLOCK
    in_spec = pl.BlockSpec((None, te, D), lambda i: (layer, i, 0))
    return pl.pallas_call(
        _expert_table_kernel,
        grid=(E // te,),
        in_specs=[in_spec, in_spec],
        out_specs=pl.BlockSpec((te, 2, SUBLANES, LANES), lambda i: (i, 0, 0, 0)),
        out_shape=jax.ShapeDtypeStruct((E, 2, SUBLANES, LANES), u_all.dtype),
        compiler_params=_cparams(1),
    )(u_all, v_all)


def _deinterleave_heads(w):
    d_in, n = w.shape
    return w.reshape(d_in, n // HEAD_DIM, HEAD_DIM // 2, 2).transpose(0, 1, 3, 2).reshape(d_in, n)


def _rope_tables(ctx_len, n_lat):
    rows = n_lat // GRID_W
    row = jnp.repeat(jnp.arange(rows, dtype=F32), GRID_W)
    col = jnp.tile(jnp.arange(GRID_W, dtype=F32), rows)
    n_freq = HEAD_DIM // 4
    inv = ROPE_BASE ** (-jnp.arange(n_freq, dtype=F32) / n_freq)
    ang = jnp.concatenate([row[:, None] * inv, col[:, None] * inv], -1)
    cos, sin = jnp.cos(ang), jnp.sin(ang)
    reps = LANES // HEAD_DIM
    cos_t = jnp.tile(jnp.concatenate([cos, cos], -1), (1, reps))
    sin_t = jnp.tile(jnp.concatenate([-sin, sin], -1), (1, reps))
    cos_t = jnp.concatenate([jnp.ones((ctx_len, LANES), F32), cos_t], 0)
    sin_t = jnp.concatenate([jnp.zeros((ctx_len, LANES), F32), sin_t], 0)
    return cos_t, sin_t


def _mixer_layer(i, S, M, cos_t, sin_t, L, last, da_wqkv, da_wo, da_lambda, da_subln, wa_wqkv, wa_wo, wa_sink,
                 ssm_lam_re, ssm_lam_im, ssm_log_step, ssm_b_re, ssm_b_im, ssm_c_re, ssm_c_im, ssm_d, ssm_w_glu,
                 ln_g, ln_b):
    D = S.shape[2]
    kind, j = i % N_MIXERS, i // N_MIXERS
    if kind == 0:
        lam_init = 0.8 - 0.6 * math.exp(-0.3 * i)
        w = da_wqkv[j]
        n_qk = 2 * DA_HEADS * 2 * HEAD_DIM
        w = jnp.concatenate([_deinterleave_heads(w[:, :n_qk]), w[:, n_qk:]], axis=1).astype(BF16)
        qkv = _project(S, M, w, cos_t, sin_t, n_qk, L)
        O = _diff_attention(qkv, da_lambda[j], da_subln[j], lam_init, L, not last)
        return _out_proj_resid_ln(O, da_wo[j].astype(BF16), S, M, ln_g[i, 0], ln_b[i, 0], L, last)
    if kind == 1:
        w = wa_wqkv[j]
        nq = WA_Q_HEADS * HEAD_DIM
        nkv = WA_KV_HEADS * HEAD_DIM
        rep = lambda m: jnp.tile(m.reshape(D, WA_KV_HEADS, 1, HEAD_DIM), (1, 1, WA_REP, 1)).reshape(D, nq)
        w = jnp.concatenate([_deinterleave_heads(w[:, :nq]),
                             rep(_deinterleave_heads(w[:, nq:nq + nkv])),
                             rep(w[:, nq + nkv:])], axis=1).astype(BF16)
        qkv = _project(S, M, w, cos_t, sin_t, 2 * nq, L)
        O = _window_attention(qkv, wa_sink[j], L, not last)
        return _out_proj_resid_ln(O, wa_wo[j].astype(BF16), S, M, ln_g[i, 0], ln_b[i, 0], L, last)
    bmat, cmat, lam_r, lam_i = _s5_params(ssm_lam_re[j], ssm_lam_im[j], ssm_log_step[j], ssm_b_re[j],
                                          ssm_b_im[j], ssm_c_re[j], ssm_c_im[j])
    Mt = jnp.transpose(M[:, :, 0:2, :], (1, 2, 0, 3))
    Y_tm = _s5_scan(jnp.transpose(S, (1, 0, 2)), Mt, bmat, cmat, lam_r, lam_i, L)
    Y = jnp.transpose(Y_tm, (0, 2, 1, 3))
    return _glu_resid_ln(S, Y, ssm_d[j], ssm_w_glu[j].astype(BF16), M, ln_g[i, 0], ln_b[i, 0], L)


def kernel(x, c, ctx, c_ctx, mod_w, mod_b, ln_g, ln_b, peer_wq, peer_keys, peer_u, peer_v, da_wqkv, da_wo, da_lambda, da_subln, wa_wqkv, wa_wo, wa_sink, ssm_lam_re, ssm_lam_im, ssm_log_step, ssm_b_re, ssm_b_im, ssm_c_re, ssm_c_im, ssm_d, ssm_w_glu):
    B, T, D = x.shape
    L = ctx.shape[1]
    depth = mod_w.shape[0]
    cos_t, sin_t = _rope_tables(L, T)
    M_all = _mod_vectors(c, c_ctx, mod_w, mod_b)
    S = jnp.concatenate([ctx, x], axis=1)
    for i in range(depth):
        last = i == depth - 1
        S = _mixer_layer(i, S, M_all[i], cos_t, sin_t, L, last, da_wqkv, da_wo, da_lambda, da_subln, wa_wqkv, wa_wo,
                         wa_sink, ssm_lam_re, ssm_lam_im, ssm_log_step, ssm_b_re, ssm_b_im, ssm_c_re, ssm_c_im,
                         ssm_d, ssm_w_glu, ln_g, ln_b)
        S = _peer_ffn_resid_ln(S, M_all[i], peer_wq[i], peer_keys[i], peer_u, peer_v, i, ln_g[i, 1], ln_b[i, 1],
                               L, last)
    return S
```
